```python
import math
import jax, jax.numpy as jnp
from jax import lax
import numpy as np

D_MODEL = 1024
BATCH = 1
SEQ = 16384
DEPTH = 1

GRID_W = 64
HEAD_DIM = 64
N_HEADS_DIL = 8
N_HEADS_NA = 8
WIDTH_DIL = N_HEADS_DIL * HEAD_DIM
WIDTH_NA = N_HEADS_NA * HEAD_DIM
DIL_PATTERNS = ((128, 1), (512, 4), (2048, 16))
QUERY_BLOCK = 64
NA_ROWS_MAX = 8
NA_COLS = 16
T5_BUCKETS = 32
T5_MAX_DISTANCE = 1024
D_FF = 2816
CONV_WIDTH = 3
RMS_EPS = 1e-6
IN_SPLITS = (WIDTH_DIL, WIDTH_DIL, WIDTH_DIL, WIDTH_NA, WIDTH_NA, WIDTH_NA, D_MODEL, D_MODEL)
IN_COLS = sum(IN_SPLITS)

kernel_name = "hybrid_dilated_neighbourhood_gated_encoder"


def rmsnorm(x, g):
    x32 = x.astype(jnp.float32)
    y = x32 * lax.rsqrt(jnp.mean(x32 * x32, axis=-1, keepdims=True) + RMS_EPS)
    return (y * g.astype(jnp.float32)).astype(x.dtype)


def t5_bucket(rel):
    n = T5_BUCKETS // 2
    max_exact = n // 2
    sign_part = jnp.where(rel > 0, n, 0)
    a = jnp.abs(rel)
    af = jnp.maximum(a, 1).astype(jnp.float32)
    large = max_exact + (jnp.log(af / max_exact) / math.log(T5_MAX_DISTANCE / max_exact)
                         * (n - max_exact)).astype(jnp.int32)
    large = jnp.minimum(large, n - 1)
    return sign_part + jnp.where(a < max_exact, a, large)


def dilated_window_attention(q, k, v, rel_bias, window, dilation):
    B, S, H, Dh = q.shape
    L = S // dilation
    R = window // (2 * dilation)
    nb = -(-L // QUERY_BLOCK)
    Lp = nb * QUERY_BLOCK
    KB = QUERY_BLOCK + 2 * R

    def to_res(t):
        return t.reshape(B, L, dilation, H, Dh).transpose(0, 2, 1, 3, 4)

    qr, kr, vr = to_res(q), to_res(k), to_res(v)
    qb = jnp.pad(qr, ((0, 0), (0, 0), (0, Lp - L), (0, 0), (0, 0))).reshape(B, dilation, nb, QUERY_BLOCK, H, Dh)
    pad_kv = ((0, 0), (0, 0), (R, Lp - L + R), (0, 0), (0, 0))
    key_idx = jnp.arange(nb)[:, None] * QUERY_BLOCK + jnp.arange(KB)[None, :]
    kb = jnp.pad(kr, pad_kv)[:, :, key_idx]
    vb = jnp.pad(vr, pad_kv)[:, :, key_idx]

    delta = jnp.arange(KB)[None, :] - jnp.arange(QUERY_BLOCK)[:, None] - R
    key_pos = key_idx - R
    mask = ((key_pos >= 0) & (key_pos < L))[:, None, :] & (jnp.abs(delta) <= R)[None]
    bias = rel_bias[t5_bucket(delta * dilation)].astype(jnp.float32).transpose(2, 0, 1)

    scale = HEAD_DIM ** -0.5
    s = jnp.einsum('bgnqhd,bgnkhd->bghnqk', qb, kb).astype(jnp.float32) * scale + bias[:, None]
    s = jnp.where(mask, s, -jnp.inf)
    m = jnp.max(s, axis=-1, keepdims=True)
    p = jnp.exp(s - m)
    l = jnp.sum(p, axis=-1, keepdims=True)
    o = jnp.einsum('bghnqk,bgnkhd->bgnqhd', (p / l).astype(v.dtype), vb)
    lse = (m + jnp.log(l))[..., 0]

    o = o.reshape(B, dilation, Lp, H, Dh)[:, :, :L].transpose(0, 2, 1, 3, 4).reshape(B, S, H, Dh)
    lse = lse.transpose(0, 1, 3, 4, 2).reshape(B, dilation, Lp, H)[:, :, :L]
    lse = lse.transpose(0, 2, 1, 3).reshape(B, S, H)
    return o, lse


def mixture_of_dilations(q, k, v, rel_bias):
    outs, lses = [], []
    for window, dilation in DIL_PATTERNS:
        o, lse = dilated_window_attention(q, k, v, rel_bias, window, dilation)
        outs.append(o)
        lses.append(lse)
    w = jax.nn.softmax(jnp.stack(lses, axis=0), axis=0)
    return jnp.einsum('gbsh,gbshd->bshd', w.astype(q.dtype), jnp.stack(outs, axis=0))


def neighbourhood_attention_2d(q, k, v, rpb):
    B, S, H, Dh = q.shape
    rows = S // GRID_W
    kh = min(NA_ROWS_MAX, rows)
    q5 = q.reshape(B, rows, GRID_W, H, Dh)
    k5 = k.reshape(B, rows, GRID_W, H, Dh)
    v5 = v.reshape(B, rows, GRID_W, H, Dh)

    r = jnp.arange(rows)
    row_start = jnp.clip(r - kh // 2, 0, rows - kh)
    row_idx = row_start[:, None] + jnp.arange(kh)[None, :]
    kr = k5[:, row_idx]
    vr = v5[:, row_idx]

    cq = jnp.arange(GRID_W)
    col_start = jnp.clip(cq - NA_COLS // 2, 0, GRID_W - NA_COLS)
    col_mask = (cq[None, :] >= col_start[:, None]) & (cq[None, :] < col_start[:, None] + NA_COLS)

    dr = row_idx - r[:, None]
    dc = jnp.clip(cq[None, :] - cq[:, None], -(NA_COLS - 1), NA_COLS - 1)
    bias = rpb.astype(jnp.float32)[:, (dr + NA_ROWS_MAX - 1)[:, :, None, None], (dc + NA_COLS - 1)[None, None]]
    bias = bias.transpose(0, 1, 3, 2, 4)

    scale = HEAD_DIM ** -0.5
    s = jnp.einsum('brqhd,brakhd->bhrqak', q5, kr).astype(jnp.float32) * scale + bias[None]
    s = jnp.where(col_mask[:, None, :], s, -jnp.inf)
    p = jax.nn.softmax(s, axis=(-2, -1))
    o = jnp.einsum('bhrqak,brakhd->brqhd', p.astype(v.dtype), vr)
    return o.reshape(B, S, H, Dh)


def depthwise_conv_seq(u, w, b):
    C = u.shape[-1]
    pad = CONV_WIDTH // 2
    y = lax.conv_general_dilated(u, w.astype(u.dtype)[:, None, :], window_strides=(1,),
                                 padding=((pad, pad),), dimension_numbers=('NWC', 'WIO', 'NWC'),
                                 feature_group_count=C)
    return y + b.astype(u.dtype)


def setup_inputs(seed: int = 0) -> dict:
    key = jax.random.key(seed)
    ks = jax.random.split(key, 20)
    f32 = jnp.float32
    nrm = lambda k, shape, s: (jax.random.normal(k, shape, f32) * s)
    return {
        "x": nrm(ks[0], (BATCH, SEQ, D_MODEL), 1.0),
        "c": nrm(ks[1], (BATCH, D_MODEL), 1.0),
        "w_ada": nrm(ks[2], (DEPTH, D_MODEL, 6 * D_MODEL), D_MODEL ** -0.5),
        "b_ada": nrm(ks[3], (DEPTH, 6 * D_MODEL), 0.02),
        "g_mix": 1.0 + nrm(ks[4], (DEPTH, D_MODEL), 0.02),
        "w_in": nrm(ks[5], (DEPTH, D_MODEL, IN_COLS), D_MODEL ** -0.5),
        "rel_bias": nrm(ks[6], (T5_BUCKETS, N_HEADS_DIL), 0.5),
        "na_rpb": nrm(ks[7], (DEPTH, N_HEADS_NA, 2 * NA_ROWS_MAX - 1, 2 * NA_COLS - 1), 0.5),
        "w_branch_dil": nrm(ks[8], (DEPTH, WIDTH_DIL, D_MODEL), WIDTH_DIL ** -0.5),
        "w_branch_na": nrm(ks[9], (DEPTH, WIDTH_NA, D_MODEL), WIDTH_NA ** -0.5),
        "w_out": nrm(ks[10], (DEPTH, D_MODEL, D_MODEL), D_MODEL ** -0.5),
        "g_ffn": 1.0 + nrm(ks[11], (DEPTH, D_MODEL), 0.02),
        "w_up": nrm(ks[12], (DEPTH, D_MODEL, 2 * D_FF), D_MODEL ** -0.5),
        "conv_w": nrm(ks[13], (DEPTH, CONV_WIDTH, 2 * D_FF), CONV_WIDTH ** -0.5),
        "conv_b": nrm(ks[14], (DEPTH, 2 * D_FF), 0.01),
        "w_down": nrm(ks[15], (DEPTH, D_FF, D_MODEL), D_FF ** -0.5),
        "g_final": 1.0 + nrm(ks[16], (D_MODEL,), 0.02),
    }


def reference(x, c, w_ada, b_ada, g_mix, w_in, rel_bias, na_rpb, w_branch_dil, w_branch_na,
              w_out, g_ffn, w_up, conv_w, conv_b, w_down, g_final):
    B, S, D = x.shape
    split_at = list(np.cumsum(IN_SPLITS)[:-1])
    for i in range(DEPTH):
        ada = jax.nn.silu(c) @ w_ada[i] + b_ada[i]
        shift1, scale1, gate1, shift2, scale2, gate2 = [t[:, None, :] for t in jnp.split(ada, 6, axis=-1)]

        h = rmsnorm(x, g_mix[i]) * (1 + scale1) + shift1
        proj = h @ w_in[i]
        qa, ka, va, qb, kb, vb, ga, gb = jnp.split(proj, split_at, axis=-1)
        heads = lambda t, n: t.reshape(B, S, n, HEAD_DIM)
        oa = mixture_of_dilations(heads(qa, N_HEADS_DIL), heads(ka, N_HEADS_DIL), heads(va, N_HEADS_DIL), rel_bias)
        ob = neighbourhood_attention_2d(heads(qb, N_HEADS_NA), heads(kb, N_HEADS_NA), heads(vb, N_HEADS_NA), na_rpb[i])
        ya = oa.reshape(B, S, WIDTH_DIL) @ w_branch_dil[i]
        yb = ob.reshape(B, S, WIDTH_NA) @ w_branch_na[i]
        merged = jax.nn.sigmoid(ga) * ya + jax.nn.sigmoid(gb) * yb
        x = x + gate1 * (merged @ w_out[i])

        h2 = rmsnorm(x, g_ffn[i]) * (1 + scale2) + shift2
        u = depthwise_conv_seq(h2 @ w_up[i], conv_w[i], conv_b[i])
        u_val, u_gate = jnp.split(u, 2, axis=-1)
        x = x + gate2 * ((jax.nn.gelu(u_gate, approximate=True) * u_val) @ w_down[i])
    return rmsnorm(x, g_final)
```

```python
import functools
import math

import numpy as np
import jax
import jax.numpy as jnp
from jax import lax
from jax.experimental import pallas as pl
from jax.experimental.pallas import tpu as pltpu

F32 = jnp.float32
BF16 = jnp.bfloat16

D_MODEL = 1024
SEQ = 16384
GRID_W = 64
ROWS = SEQ // GRID_W
HEAD_DIM = 64
N_HEADS = 8
WIDTH = N_HEADS * HEAD_DIM
DILATIONS = (1, 4, 16)
BAND = 64
NA_KH = 8
NA_COLS = 16
NA_ROWS_MAX = 8
T5_BUCKETS = 32
T5_MAX_DISTANCE = 1024
D_FF = 2816
RMS_EPS = 1e-6
IN_COLS = 6 * WIDTH + 2 * D_MODEL
NEG = -1e30

VMEM_LIMIT = 56 * 1024 * 1024

TM_IN = 512
TQ_BAND = 256
TM_MERGE = 512
TM_FFN = 512
TF_FFN = 256
HALO = 16
LANES = 128


def _params(sem):
    return pltpu.CompilerParams(dimension_semantics=sem, vmem_limit_bytes=VMEM_LIMIT)


def _ada_kernel(c_ref, w_ref, b_ref, o_ref):
    c = c_ref[...]
    s = c / (1.0 + jnp.exp(-c))
    o_ref[...] = jnp.sum(w_ref[...] * s, axis=0, keepdims=True) + b_ref[...]


def _ada(c, w_ada, b_ada):
    n_out = w_ada.shape[1]
    tn = D_MODEL
    return pl.pallas_call(
        _ada_kernel,
        grid=(n_out // tn,),
        in_specs=[
            pl.BlockSpec((D_MODEL, 1), lambda j: (0, 0)),
            pl.BlockSpec((D_MODEL, tn), lambda j: (0, j)),
            pl.BlockSpec((1, tn), lambda j: (0, j)),
        ],
        out_specs=pl.BlockSpec((1, tn), lambda j: (0, j)),
        out_shape=jax.ShapeDtypeStruct((1, n_out), F32),
        compiler_params=_params(("arbitrary",)),
        name="ada",
    )(c.reshape(D_MODEL, 1), w_ada, b_ada.reshape(1, n_out))


def _rms_modulate(x, mod_ref):
    y = x * lax.rsqrt(jnp.mean(x * x, axis=-1, keepdims=True) + RMS_EPS)
    return (y * mod_ref[0:1, :]) * (1.0 + mod_ref[1:2, :]) + mod_ref[2:3, :]


def _inproj_kernel(x_ref, mod_ref, w_ref, anat_ref, am4_ref, am16_ref, b_ref, g_ref, h_scr, res_scr):
    tm = x_ref.shape[0]
    h_scr[...] = _rms_modulate(x_ref[...], mod_ref).astype(BF16)
    for n in range(IN_COLS // WIDTH):
        cols = slice(n * WIDTH, (n + 1) * WIDTH)
        r = jnp.dot(h_scr[...], w_ref[:, cols], preferred_element_type=F32)
        if n < 3:
            anat_ref[:, cols] = r.astype(BF16)
            for t in range(WIDTH // LANES):
                res_scr[t] = r[:, t * LANES:(t + 1) * LANES]
            for d, ref in ((4, am4_ref), (16, am16_ref)):
                for c in range(d):
                    for t in range(WIDTH // LANES):
                        lanes = slice(n * WIDTH + t * LANES, n * WIDTH + (t + 1) * LANES)
                        ref[c, :, lanes] = res_scr[t, pl.ds(c, tm // d, stride=d), :].astype(BF16)
        elif n < 6:
            b_ref[:, (n - 3) * WIDTH:(n - 2) * WIDTH] = r.astype(BF16)
        else:
            g_ref[:, (n - 6) * WIDTH:(n - 5) * WIDTH] = (1.0 / (1.0 + jnp.exp(-r))).astype(BF16)


def _inproj(x, mod1, w_in):
    tm = TM_IN
    nt = SEQ // tm
    qkv = 3 * WIDTH
    return pl.pallas_call(
        _inproj_kernel,
        grid=(nt,),
        in_specs=[
            pl.BlockSpec((tm, D_MODEL), lambda i: (i, 0)),
            pl.BlockSpec((3, D_MODEL), lambda i: (0, 0)),
            pl.BlockSpec((D_MODEL, IN_COLS), lambda i: (0, 0), pipeline_mode=pl.Buffered(1)),
        ],
        out_specs=[
            pl.BlockSpec((tm, qkv), lambda i: (i, 0)),
            pl.BlockSpec((4, tm // 4, qkv), lambda i: (0, i, 0)),
            pl.BlockSpec((16, tm // 16, qkv), lambda i: (0, i, 0)),
            pl.BlockSpec((tm, qkv), lambda i: (i, 0)),
            pl.BlockSpec((tm, 2 * D_MODEL), lambda i: (i, 0)),
        ],
        out_shape=[
            jax.ShapeDtypeStruct((SEQ, qkv), BF16),
            jax.ShapeDtypeStruct((4, SEQ // 4, qkv), BF16),
            jax.ShapeDtypeStruct((16, SEQ // 16, qkv), BF16),
            jax.ShapeDtypeStruct((SEQ, qkv), BF16),
            jax.ShapeDtypeStruct((SEQ, 2 * D_MODEL), BF16),
        ],
        scratch_shapes=[pltpu.VMEM((tm, D_MODEL), BF16),
                        pltpu.VMEM((WIDTH // LANES, tm, LANES), F32)],
        compiler_params=_params(("arbitrary",)),
        name="inproj",
    )(x, mod1, w_in)


def _band_kernel(q_ref, kl_ref, km_ref, kr_ref, vl_ref, vm_ref, vr_ref, bias_ref,
                 o_ref, lse_ref, kcat, vcat, *, seg_len):
    tq = q_ref.shape[0]
    nk = tq + 2 * BAND
    start = pl.program_id(0) * tq
    for cat, (l_ref, m_ref, r_ref) in ((kcat, (kl_ref, km_ref, kr_ref)), (vcat, (vl_ref, vm_ref, vr_ref))):
        cat[0:BAND, :] = l_ref[...]
        cat[BAND:BAND + tq, :] = m_ref[...]
        cat[BAND + tq:nk, :] = r_ref[...]
    left_out = (start % seg_len) == 0
    right_out = ((start + tq) % seg_len) == 0
    col = lax.broadcasted_iota(jnp.int32, (1, nk), 1)
    edge = jnp.where(((col < BAND) & left_out) | ((col >= BAND + tq) & right_out), NEG, 0.0)
    lse_ref[...] = jnp.zeros_like(lse_ref)
    for h in range(N_HEADS):
        hs = slice(h * HEAD_DIM, (h + 1) * HEAD_DIM)
        s = lax.dot_general(q_ref[:, hs], kcat[:, hs], (((1,), (1,)), ((), ())),
                            preferred_element_type=F32)
        s = s + bias_ref[h] + edge
        m = jnp.max(s, axis=-1, keepdims=True)
        p = jnp.exp(s - m)
        l = jnp.sum(p, axis=-1, keepdims=True)
        o = jnp.dot(p.astype(BF16), vcat[:, hs], preferred_element_type=F32)
        o_ref[:, hs] = o / l
        lse_ref[:, h:h + 1] = m + jnp.log(l)


def _band_attention(qkv, bias, seg_len):
    tq = TQ_BAND
    nt = SEQ // tq
    per = tq // BAND
    nhalo = SEQ // BAND
    left = lambda c: (lambda i: (jnp.maximum(i * per - 1, 0), c))
    right = lambda c: (lambda i: (jnp.minimum((i + 1) * per, nhalo - 1), c))
    main = lambda c: (lambda i: (i, c))
    kv_specs = []
    for c in (1, 2):
        kv_specs += [pl.BlockSpec((BAND, WIDTH), left(c)), pl.BlockSpec((tq, WIDTH), main(c)),
                     pl.BlockSpec((BAND, WIDTH), right(c))]
    nk = tq + 2 * BAND
    return pl.pallas_call(
        functools.partial(_band_kernel, seg_len=seg_len),
        grid=(nt,),
        in_specs=[pl.BlockSpec((tq, WIDTH), main(0))] + kv_specs
        + [pl.BlockSpec((N_HEADS, tq, nk), lambda i: (0, 0, 0))],
        out_specs=[pl.BlockSpec((tq, WIDTH), lambda i: (i, 0)),
                   pl.BlockSpec((tq, LANES), lambda i: (i, 0))],
        out_shape=[jax.ShapeDtypeStruct((SEQ, WIDTH), F32),
                   jax.ShapeDtypeStruct((SEQ, LANES), F32)],
        scratch_shapes=[pltpu.VMEM((nk, WIDTH), BF16), pltpu.VMEM((nk, WIDTH), BF16)],
        compiler_params=_params(("arbitrary",)),
        name=f"band_l{seg_len}",
    )(qkv, qkv, qkv, qkv, qkv, qkv, qkv, bias)


def _t5_bucket(rel):
    n = T5_BUCKETS // 2
    max_exact = n // 2
    sign_part = jnp.where(rel > 0, n, 0)
    a = jnp.abs(rel)
    af = jnp.maximum(a, 1).astype(F32)
    large = max_exact + (jnp.log(af / max_exact) / math.log(T5_MAX_DISTANCE / max_exact)
                         * (n - max_exact)).astype(jnp.int32)
    large = jnp.minimum(large, n - 1)
    return sign_part + jnp.where(a < max_exact, a, large)


def _band_bias(rel_bias, dilation):
    tq = TQ_BAND
    delta = jnp.arange(tq + 2 * BAND)[None, :] - BAND - jnp.arange(tq)[:, None]
    b = rel_bias[_t5_bucket(delta * dilation)].astype(F32).transpose(2, 0, 1)
    return jnp.where((jnp.abs(delta) <= BAND)[None], b, NEG)


def _na_kernel(q_ref, kp_ref, kc_ref, kn_ref, vp_ref, vc_ref, vn_ref, bias_ref, o_ref, kcat, vcat):
    tile = q_ref.shape[0]
    i = pl.program_id(0)
    for cat, refs in ((kcat, (kp_ref, kc_ref, kn_ref)), (vcat, (vp_ref, vc_ref, vn_ref))):
        for n, ref in enumerate(refs):
            cat[n * tile:(n + 1) * tile, :] = ref[...]
    nkeys = NA_KH * GRID_W

    def row_body(t, carry):
        r = i * NA_KH + t
        rs = jnp.clip(r - NA_KH // 2, 0, ROWS - NA_KH)
        off = pl.multiple_of((rs - (i - 1) * NA_KH) * GRID_W, GRID_W)
        var = rs - r + (NA_KH - 1)
        qoff = pl.multiple_of(t * GRID_W, GRID_W)
        for h in range(N_HEADS):
            hs = slice(h * HEAD_DIM, (h + 1) * HEAD_DIM)
            s = lax.dot_general(q_ref[pl.ds(qoff, GRID_W), hs], kcat[pl.ds(off, nkeys), hs],
                                (((1,), (1,)), ((), ())), preferred_element_type=F32)
            s = s + bias_ref[var, h]
            m = jnp.max(s, axis=-1, keepdims=True)
            p = jnp.exp(s - m)
            l = jnp.sum(p, axis=-1, keepdims=True)
            o = jnp.dot(p.astype(BF16), vcat[pl.ds(off, nkeys), hs], preferred_element_type=F32)
            o_ref[pl.ds(qoff, GRID_W), hs] = (o / l).astype(o_ref.dtype)
        return carry

    lax.fori_loop(0, NA_KH, row_body, 0)


def _na_attention(qkv, bias):
    tile = NA_KH * GRID_W
    nt = SEQ // tile
    prev = lambda c: (lambda i: (jnp.maximum(i - 1, 0), c))
    cur = lambda c: (lambda i: (i, c))
    nxt = lambda c: (lambda i: (jnp.minimum(i + 1, nt - 1), c))
    kv_specs = [pl.BlockSpec((tile, WIDTH), f(c)) for c in (1, 2) for f in (prev, cur, nxt)]
    return pl.pallas_call(
        _na_kernel,
        grid=(nt,),
        in_specs=[pl.BlockSpec((tile, WIDTH), cur(0))] + kv_specs
        + [pl.BlockSpec(bias.shape, lambda i: (0, 0, 0, 0), pipeline_mode=pl.Buffered(1))],
        out_specs=pl.BlockSpec((tile, WIDTH), lambda i: (i, 0)),
        out_shape=jax.ShapeDtypeStruct((SEQ, WIDTH), BF16),
        scratch_shapes=[pltpu.VMEM((3 * tile, WIDTH), BF16), pltpu.VMEM((3 * tile, WIDTH), BF16)],
        compiler_params=_params(("arbitrary",)),
        name="na",
    )(qkv, qkv, qkv, qkv, qkv, qkv, qkv, bias)


def _na_bias(rpb):
    v = jnp.arange(NA_KH)
    a = jnp.arange(NA_KH)
    dr_idx = v[:, None] + a[None, :]
    cq = jnp.arange(GRID_W)
    col_start = jnp.clip(cq - NA_COLS // 2, 0, GRID_W - NA_COLS)
    col_mask = (cq[None, :] >= col_start[:, None]) & (cq[None, :] < col_start[:, None] + NA_COLS)
    dc = jnp.clip(cq[None, :] - cq[:, None], -(NA_COLS - 1), NA_COLS - 1) + NA_COLS - 1
    b = rpb.astype(F32)[:, dr_idx[:, :, None, None], dc[None, None]]
    b = jnp.where(col_mask[None, None, None], b, NEG)
    return b.transpose(1, 0, 3, 2, 4).reshape(NA_KH, N_HEADS, GRID_W, NA_KH * GRID_W)


def _merge_kernel(x_ref, o1_ref, o4_ref, o16_ref, l1_ref, l4_ref, l16_ref, ob_ref, g_ref,
                  wbd_ref, wbn_ref, wout_ref, gate_ref, mod_ref,
                  x1_ref, h2_ref, o4n, o16n, l4n, l16n, oa_scr):
    tm = x_ref.shape[0]
    for d, src, dst in ((4, o4_ref, o4n), (16, o16_ref, o16n), (4, l4_ref, l4n), (16, l16_ref, l16n)):
        for c in range(d):
            for t in range(dst.shape[0]):
                dst[t, pl.ds(c, tm // d, stride=d), :] = src[c, :, t * LANES:(t + 1) * LANES]
    l1, l4, l16 = l1_ref[...], l4n[0], l16n[0]
    mx = jnp.maximum(jnp.maximum(l1, l4), l16)
    e1, e4, e16 = jnp.exp(l1 - mx), jnp.exp(l4 - mx), jnp.exp(l16 - mx)
    den = e1 + e4 + e16
    w1, w4, w16 = e1 / den, e4 / den, e16 / den
    for h in range(N_HEADS):
        hs = slice(h * HEAD_DIM, (h + 1) * HEAD_DIM)
        hc = slice(h, h + 1)
        t, ls = divmod(h * HEAD_DIM, LANES)
        ls = slice(ls, ls + HEAD_DIM)
        oa = w1[:, hc] * o1_ref[:, hs] + w4[:, hc] * o4n[t, :, ls] + w16[:, hc] * o16n[t, :, ls]
        oa_scr[:, hs] = oa.astype(BF16)
    ya = jnp.dot(oa_scr[...], wbd_ref[...], preferred_element_type=F32)
    yb = jnp.dot(ob_ref[...], wbn_ref[...], preferred_element_type=F32)
    merged = g_ref[:, 0:D_MODEL].astype(F32) * ya + g_ref[:, D_MODEL:2 * D_MODEL].astype(F32) * yb
    z = jnp.dot(merged.astype(BF16), wout_ref[...], preferred_element_type=F32)
    x1 = x_ref[...] + gate_ref[...] * z
    x1_ref[...] = x1
    h2_ref[...] = _rms_modulate(x1, mod_ref).astype(BF16)


def _merge(x, o1, o4, o16, l1, l4, l16, ob, g, wbd, wbn, wout, gate1, mod2):
    tm = TM_MERGE
    nt = SEQ // tm
    row = lambda w: pl.BlockSpec((tm, w), lambda i: (i, 0))
    cls = lambda d, w: pl.BlockSpec((d, tm // d, w), lambda i: (0, i, 0))
    const = lambda shape: pl.BlockSpec(shape, lambda i: (0, 0))
    return pl.pallas_call(
        _merge_kernel,
        grid=(nt,),
        in_specs=[row(D_MODEL), row(WIDTH), cls(4, WIDTH), cls(16, WIDTH),
                  row(LANES), cls(4, LANES), cls(16, LANES), row(WIDTH), row(2 * D_MODEL),
                  const((WIDTH, D_MODEL)), const((WIDTH, D_MODEL)), const((D_MODEL, D_MODEL)),
                  const((1, D_MODEL)), const((3, D_MODEL))],
        out_specs=[row(D_MODEL), row(D_MODEL)],
        out_shape=[jax.ShapeDtypeStruct((SEQ, D_MODEL), F32), jax.ShapeDtypeStruct((SEQ, D_MODEL), BF16)],
        scratch_shapes=[pltpu.VMEM((WIDTH // LANES, tm, LANES), F32), pltpu.VMEM((WIDTH // LANES, tm, LANES), F32),
                        pltpu.VMEM((1, tm, LANES), F32), pltpu.VMEM((1, tm, LANES), F32),
                        pltpu.VMEM((tm, WIDTH), BF16)],
        compiler_params=_params(("arbitrary",)),
        name="merge",
    )(x, o1, o4.reshape(4, SEQ // 4, WIDTH), o16.reshape(16, SEQ // 16, WIDTH),
      l1, l4.reshape(4, SEQ // 4, LANES), l16.reshape(16, SEQ // 16, LANES), ob, g,
      wbd, wbn, wout, gate1, mod2)


def _ffn_kernel(x1_ref, hm_ref, hp_ref, hn_ref, wv_ref, wg_ref, cwv_ref, cwg_ref, cbv_ref, cbg_ref,
                wd_ref, gate_ref, gfin_ref, y_ref, hext, acc):
    tm = x1_ref.shape[0]
    i, j = pl.program_id(0), pl.program_id(1)

    @pl.when(j == 0)
    def _():
        hext[0:HALO, :] = jnp.where(i == 0, jnp.zeros_like(hp_ref), hp_ref[...])
        hext[HALO:HALO + tm, :] = hm_ref[...]
        hext[HALO + tm:HALO + tm + HALO, :] = jnp.where(i == pl.num_programs(0) - 1,
                                                        jnp.zeros_like(hn_ref), hn_ref[...])
        acc[...] = jnp.zeros_like(acc)

    def conv(w_ref, cw_ref, cb_ref):
        p = jnp.dot(hext[...], w_ref[...], preferred_element_type=F32)
        return (cw_ref[0:1, :] * p[HALO - 1:HALO - 1 + tm] + cw_ref[1:2, :] * p[HALO:HALO + tm]
                + cw_ref[2:3, :] * p[HALO + 1:HALO + 1 + tm] + cb_ref[...])

    u_val = conv(wv_ref, cwv_ref, cbv_ref)
    u_gate = conv(wg_ref, cwg_ref, cbg_ref)
    act = (jax.nn.gelu(u_gate, approximate=True) * u_val).astype(BF16)
    acc[...] += jnp.dot(act, wd_ref[...], preferred_element_type=F32)

    @pl.when(j == pl.num_programs(1) - 1)
    def _():
        x2 = x1_ref[...] + gate_ref[...] * acc[...]
        y = x2 * lax.rsqrt(jnp.mean(x2 * x2, axis=-1, keepdims=True) + RMS_EPS)
        y_ref[...] = y * gfin_ref[...]


def _ffn(x1, h2, w_up, conv_w, conv_b, w_down, gate2, g_final):
    tm, tf = TM_FFN, TF_FFN
    nt, nf = SEQ // tm, D_FF // tf
    per = tm // HALO
    nhalo = SEQ // HALO
    return pl.pallas_call(
        _ffn_kernel,
        grid=(nt, nf),
        in_specs=[
            pl.BlockSpec((tm, D_MODEL), lambda i, j: (i, 0)),
            pl.BlockSpec((tm, D_MODEL), lambda i, j: (i, 0)),
            pl.BlockSpec((HALO, D_MODEL), lambda i, j: (jnp.maximum(i * per - 1, 0), 0)),
            pl.BlockSpec((HALO, D_MODEL), lambda i, j: (jnp.minimum((i + 1) * per, nhalo - 1), 0)),
            pl.BlockSpec((D_MODEL, tf), lambda i, j: (0, j)),
            pl.BlockSpec((D_MODEL, tf), lambda i, j: (0, j + nf)),
            pl.BlockSpec((3, tf), lambda i, j: (0, j)),
            pl.BlockSpec((3, tf), lambda i, j: (0, j + nf)),
            pl.BlockSpec((1, tf), lambda i, j: (0, j)),
            pl.BlockSpec((1, tf), lambda i, j: (0, j + nf)),
            pl.BlockSpec((tf, D_MODEL), lambda i, j: (j, 0)),
            pl.BlockSpec((1, D_MODEL), lambda i, j: (0, 0)),
            pl.BlockSpec((1, D_MODEL), lambda i, j: (0, 0)),
        ],
        out_specs=pl.BlockSpec((tm, D_MODEL), lambda i, j: (i, 0)),
        out_shape=jax.ShapeDtypeStruct((SEQ, D_MODEL), F32),
        scratch_shapes=[pltpu.VMEM((tm + 2 * HALO, D_MODEL), BF16), pltpu.VMEM((tm, D_MODEL), F32)],
        compiler_params=_params(("arbitrary", "arbitrary")),
        name="ffn",
    )(x1, h2, h2, h2, w_up, w_up, conv_w, conv_w, conv_b, conv_b, w_down, gate2, g_final)


def kernel(x, c, w_ada, b_ada, g_mix, w_in, rel_bias, na_rpb, w_branch_dil, w_branch_na, w_out, g_ffn,
           w_up, conv_w, conv_b, w_down, g_final):
    assert x.shape == (1, SEQ, D_MODEL) and w_ada.shape[0] == 1
    xs = x[0]
    ada = _ada(c, w_ada[0], b_ada[0])
    shift1, scale1, gate1, shift2, scale2, gate2 = [ada[:, k * D_MODEL:(k + 1) * D_MODEL] for k in range(6)]
    mod1 = jnp.concatenate([g_mix[0][None], scale1, shift1], axis=0)
    mod2 = jnp.concatenate([g_ffn[0][None], scale2, shift2], axis=0)

    q_scale = np.ones((IN_COLS,), np.float32)
    q_scale[0:WIDTH] = HEAD_DIM ** -0.5
    q_scale[3 * WIDTH:4 * WIDTH] = HEAD_DIM ** -0.5
    w_in_b = (w_in[0] * q_scale).astype(BF16)

    a_nat, a_m4, a_m16, b_qkv, gates = _inproj(xs, mod1, w_in_b)

    outs = []
    for d, arr in zip(DILATIONS, (a_nat, a_m4, a_m16)):
        outs.append(_band_attention(arr.reshape(SEQ, 3 * WIDTH), _band_bias(rel_bias, d), SEQ // d))
    (o1, l1), (o4, l4), (o16, l16) = outs
    ob = _na_attention(b_qkv, _na_bias(na_rpb[0]))

    x1, h2 = _merge(xs, o1, o4, o16, l1, l4, l16, ob, gates,
                    w_branch_dil[0].astype(BF16), w_branch_na[0].astype(BF16), w_out[0].astype(BF16),
                    gate1, mod2)
    y = _ffn(x1, h2, w_up[0].astype(BF16), conv_w[0], conv_b[0].reshape(1, 2 * D_FF),
             w_down[0].astype(BF16), gate2, g_final.reshape(1, D_MODEL))
    return y[None]
```

```python
import functools
import math

import numpy as np
import jax
import jax.numpy as jnp
from jax import lax
from jax.experimental import pallas as pl
from jax.experimental.pallas import tpu as pltpu

F32 = jnp.float32
BF16 = jnp.bfloat16

D_MODEL = 1024
SEQ = 16384
GRID_W = 64
ROWS = SEQ // GRID_W
HEAD_DIM = 64
N_HEADS = 8
WIDTH = N_HEADS * HEAD_DIM
DILATIONS = (1, 4, 16)
BAND = 64
NA_KH = 8
NA_COLS = 16
NA_ROWS_MAX = 8
T5_BUCKETS = 32
T5_MAX_DISTANCE = 1024
D_FF = 2816
RMS_EPS = 1e-6
IN_COLS = 6 * WIDTH + 2 * D_MODEL
NEG = -1e30

VMEM_LIMIT = 56 * 1024 * 1024

TM_IN = 512
TQ_BAND = 256
TM_MERGE = 512
TM_FFN = 512
TF_FFN = 256
HALO = 16
LANES = 128


def _params(sem):
    return pltpu.CompilerParams(dimension_semantics=sem, vmem_limit_bytes=VMEM_LIMIT)


def _ada_kernel(c_ref, w_ref, b_ref, o_ref):
    c = c_ref[...]
    s = c / (1.0 + jnp.exp(-c))
    o_ref[...] = jnp.sum(w_ref[...] * s, axis=0, keepdims=True) + b_ref[...]


def _ada(c, w_ada, b_ada):
    n_out = w_ada.shape[1]
    tn = D_MODEL
    return pl.pallas_call(
        _ada_kernel,
        grid=(n_out // tn,),
        in_specs=[
            pl.BlockSpec((D_MODEL, 1), lambda j: (0, 0)),
            pl.BlockSpec((D_MODEL, tn), lambda j: (0, j)),
            pl.BlockSpec((1, tn), lambda j: (0, j)),
        ],
        out_specs=pl.BlockSpec((1, tn), lambda j: (0, j)),
        out_shape=jax.ShapeDtypeStruct((1, n_out), F32),
        compiler_params=_params(("arbitrary",)),
        name="ada",
    )(c.reshape(D_MODEL, 1), w_ada, b_ada.reshape(1, n_out))


def _rms_modulate(x, mod_ref):
    y = x * lax.rsqrt(jnp.mean(x * x, axis=-1, keepdims=True) + RMS_EPS)
    return (y * mod_ref[0:1, :]) * (1.0 + mod_ref[1:2, :]) + mod_ref[2:3, :]


def _inproj_kernel(x_ref, mod_ref, w_ref, anat_ref, am4_ref, am16_ref, b_ref, g_ref, h_scr, res_scr):
    tm = x_ref.shape[0]
    h_scr[...] = _rms_modulate(x_ref[...], mod_ref).astype(BF16)
    for n in range(IN_COLS // WIDTH):
        cols = slice(n * WIDTH, (n + 1) * WIDTH)
        r = jnp.dot(h_scr[...], w_ref[:, cols], preferred_element_type=F32)
        if n < 3:
            anat_ref[:, cols] = r.astype(BF16)
            for t in range(WIDTH // LANES):
                res_scr[t] = r[:, t * LANES:(t + 1) * LANES]
            for d, ref in ((4, am4_ref), (16, am16_ref)):
                for c in range(d):
                    for t in range(WIDTH // LANES):
                        lanes = slice(n * WIDTH + t * LANES, n * WIDTH + (t + 1) * LANES)
                        ref[c, :, lanes] = res_scr[t, pl.ds(c, tm // d, stride=d), :].astype(BF16)
        elif n < 6:
            b_ref[:, (n - 3) * WIDTH:(n - 2) * WIDTH] = r.astype(BF16)
        else:
            g_ref[:, (n - 6) * WIDTH:(n - 5) * WIDTH] = (1.0 / (1.0 + jnp.exp(-r))).astype(BF16)


def _inproj(x, mod1, w_in):
    tm = TM_IN
    nt = SEQ // tm
    qkv = 3 * WIDTH
    return pl.pallas_call(
        _inproj_kernel,
        grid=(nt,),
        in_specs=[
            pl.BlockSpec((tm, D_MODEL), lambda i: (i, 0)),
            pl.BlockSpec((3, D_MODEL), lambda i: (0, 0)),
            pl.BlockSpec((D_MODEL, IN_COLS), lambda i: (0, 0), pipeline_mode=pl.Buffered(1)),
        ],
        out_specs=[
            pl.BlockSpec((tm, qkv), lambda i: (i, 0)),
            pl.BlockSpec((4, tm // 4, qkv), lambda i: (0, i, 0)),
            pl.BlockSpec((16, tm // 16, qkv), lambda i: (0, i, 0)),
            pl.BlockSpec((tm, qkv), lambda i: (i, 0)),
            pl.BlockSpec((tm, 2 * D_MODEL), lambda i: (i, 0)),
        ],
        out_shape=[
            jax.ShapeDtypeStruct((SEQ, qkv), BF16),
            jax.ShapeDtypeStruct((4, SEQ // 4, qkv), BF16),
            jax.ShapeDtypeStruct((16, SEQ // 16, qkv), BF16),
            jax.ShapeDtypeStruct((SEQ, qkv), BF16),
            jax.ShapeDtypeStruct((SEQ, 2 * D_MODEL), BF16),
        ],
        scratch_shapes=[pltpu.VMEM((tm, D_MODEL), BF16),
                        pltpu.VMEM((WIDTH // LANES, tm, LANES), F32)],
        compiler_params=_params(("arbitrary",)),
        name="inproj",
    )(x, mod1, w_in)


def _band_kernel(q_ref, kl_ref, km_ref, kr_ref, vl_ref, vm_ref, vr_ref, bias_ref,
                 o_ref, lse_ref, kcat, vcat, *, seg_len):
    tq = q_ref.shape[0]
    nk = tq + 2 * BAND
    start = pl.program_id(0) * tq
    for cat, (l_ref, m_ref, r_ref) in ((kcat, (kl_ref, km_ref, kr_ref)), (vcat, (vl_ref, vm_ref, vr_ref))):
        cat[0:BAND, :] = l_ref[...]
        cat[BAND:BAND + tq, :] = m_ref[...]
        cat[BAND + tq:nk, :] = r_ref[...]
    left_out = (start % seg_len) == 0
    right_out = ((start + tq) % seg_len) == 0
    col = lax.broadcasted_iota(jnp.int32, (1, nk), 1)
    edge = jnp.where(((col < BAND) & left_out) | ((col >= BAND + tq) & right_out), NEG, 0.0)
    lse_ref[...] = jnp.zeros_like(lse_ref)
    for h in range(N_HEADS):
        hs = slice(h * HEAD_DIM, (h + 1) * HEAD_DIM)
        s = lax.dot_general(q_ref[:, hs], kcat[:, hs], (((1,), (1,)), ((), ())),
                            preferred_element_type=F32)
        s = s + bias_ref[h] + edge
        m = jnp.max(s, axis=-1, keepdims=True)
        p = jnp.exp(s - m)
        l = jnp.sum(p, axis=-1, keepdims=True)
        o = jnp.dot(p.astype(BF16), vcat[:, hs], preferred_element_type=F32)
        o_ref[:, hs] = o / l
        lse_ref[:, h:h + 1] = m + jnp.log(l)


def _band_attention(qkv, bias, seg_len):
    tq = TQ_BAND
    nt = SEQ // tq
    per = tq // BAND
    nhalo = SEQ // BAND
    left = lambda c: (lambda i: (jnp.maximum(i * per - 1, 0), c))
    right = lambda c: (lambda i: (jnp.minimum((i + 1) * per, nhalo - 1), c))
    main = lambda c: (lambda i: (i, c))
    kv_specs = []
    for c in (1, 2):
        kv_specs += [pl.BlockSpec((BAND, WIDTH), left(c)), pl.BlockSpec((tq, WIDTH), main(c)),
                     pl.BlockSpec((BAND, WIDTH), right(c))]
    nk = tq + 2 * BAND
    return pl.pallas_call(
        functools.partial(_band_kernel, seg_len=seg_len),
        grid=(nt,),
        in_specs=[pl.BlockSpec((tq, WIDTH), main(0))] + kv_specs
        + [pl.BlockSpec((N_HEADS, tq, nk), lambda i: (0, 0, 0))],
        out_specs=[pl.BlockSpec((tq, WIDTH), lambda i: (i, 0)),
                   pl.BlockSpec((tq, LANES), lambda i: (i, 0))],
        out_shape=[jax.ShapeDtypeStruct((SEQ, WIDTH), F32),
                   jax.ShapeDtypeStruct((SEQ, LANES), F32)],
        scratch_shapes=[pltpu.VMEM((nk, WIDTH), BF16), pltpu.VMEM((nk, WIDTH), BF16)],
        compiler_params=_params(("arbitrary",)),
        name=f"band_l{seg_len}",
    )(qkv, qkv, qkv, qkv, qkv, qkv, qkv, bias)


def _t5_bucket(rel):
    n = T5_BUCKETS // 2
    max_exact = n // 2
    sign_part = jnp.where(rel > 0, n, 0)
    a = jnp.abs(rel)
    af = jnp.maximum(a, 1).astype(F32)
    large = max_exact + (jnp.log(af / max_exact) / math.log(T5_MAX_DISTANCE / max_exact)
                         * (n - max_exact)).astype(jnp.int32)
    large = jnp.minimum(large, n - 1)
    return sign_part + jnp.where(a < max_exact, a, large)


def _band_bias(rel_bias, dilation):
    tq = TQ_BAND
    delta = jnp.arange(tq + 2 * BAND)[None, :] - BAND - jnp.arange(tq)[:, None]
    b = _lookup(rel_bias.astype(F32).T, _t5_bucket(delta * dilation), T5_BUCKETS)
    return jnp.where((jnp.abs(delta) <= BAND)[None], b, NEG)


def _lookup(table, idx, n):
    onehot = (idx[..., None] == jnp.arange(n)).astype(F32)
    return jnp.einsum("...b,xyb->...xy", table, onehot, precision=lax.Precision.HIGHEST)


def _na_kernel(q_ref, kp_ref, kc_ref, kn_ref, vp_ref, vc_ref, vn_ref, bias_ref, o_ref, kcat, vcat):
    tile = q_ref.shape[0]
    i = pl.program_id(0)
    for cat, refs in ((kcat, (kp_ref, kc_ref, kn_ref)), (vcat, (vp_ref, vc_ref, vn_ref))):
        for n, ref in enumerate(refs):
            cat[n * tile:(n + 1) * tile, :] = ref[...]
    nkeys = NA_KH * GRID_W

    def row_body(t, carry):
        r = i * NA_KH + t
        rs = jnp.clip(r - NA_KH // 2, 0, ROWS - NA_KH)
        off = pl.multiple_of((rs - (i - 1) * NA_KH) * GRID_W, GRID_W)
        var = rs - r + (NA_KH - 1)
        qoff = pl.multiple_of(t * GRID_W, GRID_W)
        for h in range(N_HEADS):
            hs = slice(h * HEAD_DIM, (h + 1) * HEAD_DIM)
            s = lax.dot_general(q_ref[pl.ds(qoff, GRID_W), hs], kcat[pl.ds(off, nkeys), hs],
                                (((1,), (1,)), ((), ())), preferred_element_type=F32)
            s = s + bias_ref[var, h]
            m = jnp.max(s, axis=-1, keepdims=True)
            p = jnp.exp(s - m)
            l = jnp.sum(p, axis=-1, keepdims=True)
            o = jnp.dot(p.astype(BF16), vcat[pl.ds(off, nkeys), hs], preferred_element_type=F32)
            o_ref[pl.ds(qoff, GRID_W), hs] = (o / l).astype(o_ref.dtype)
        return carry

    lax.fori_loop(0, NA_KH, row_body, 0)


def _na_attention(qkv, bias):
    tile = NA_KH * GRID_W
    nt = SEQ // tile
    prev = lambda c: (lambda i: (jnp.maximum(i - 1, 0), c))
    cur = lambda c: (lambda i: (i, c))
    nxt = lambda c: (lambda i: (jnp.minimum(i + 1, nt - 1), c))
    kv_specs = [pl.BlockSpec((tile, WIDTH), f(c)) for c in (1, 2) for f in (prev, cur, nxt)]
    return pl.pallas_call(
        _na_kernel,
        grid=(nt,),
        in_specs=[pl.BlockSpec((tile, WIDTH), cur(0))] + kv_specs
        + [pl.BlockSpec(bias.shape, lambda i: (0, 0, 0, 0), pipeline_mode=pl.Buffered(1))],
        out_specs=pl.BlockSpec((tile, WIDTH), lambda i: (i, 0)),
        out_shape=jax.ShapeDtypeStruct((SEQ, WIDTH), BF16),
        scratch_shapes=[pltpu.VMEM((3 * tile, WIDTH), BF16), pltpu.VMEM((3 * tile, WIDTH), BF16)],
        compiler_params=_params(("arbitrary",)),
        name="na",
    )(qkv, qkv, qkv, qkv, qkv, qkv, qkv, bias)


def _na_bias(rpb):
    cq = jnp.arange(GRID_W)
    col_start = jnp.clip(cq - NA_COLS // 2, 0, GRID_W - NA_COLS)
    col_mask = (cq[None, :] >= col_start[:, None]) & (cq[None, :] < col_start[:, None] + NA_COLS)
    dc = jnp.clip(cq[None, :] - cq[:, None], -(NA_COLS - 1), NA_COLS - 1) + NA_COLS - 1
    e = _lookup(rpb.astype(F32), dc, 2 * NA_COLS - 1)
    e = jnp.where(col_mask[None, None], e, NEG)
    b = jnp.stack([e[:, v:v + NA_KH] for v in range(NA_KH)])
    return b.transpose(0, 1, 3, 2, 4).reshape(NA_KH, N_HEADS, GRID_W, NA_KH * GRID_W)


def _merge_kernel(x_ref, o1_ref, o4_ref, o16_ref, l1_ref, l4_ref, l16_ref, ob_ref, g_ref,
                  wbd_ref, wbn_ref, wout_ref, gate_ref, mod_ref,
                  x1_ref, h2_ref, o4n, o16n, l4n, l16n, oa_scr):
    tm = x_ref.shape[0]
    for d, src, dst in ((4, o4_ref, o4n), (16, o16_ref, o16n), (4, l4_ref, l4n), (16, l16_ref, l16n)):
        for c in range(d):
            for t in range(dst.shape[0]):
                dst[t, pl.ds(c, tm // d, stride=d), :] = src[c, :, t * LANES:(t + 1) * LANES]
    l1, l4, l16 = l1_ref[...], l4n[0], l16n[0]
    mx = jnp.maximum(jnp.maximum(l1, l4), l16)
    e1, e4, e16 = jnp.exp(l1 - mx), jnp.exp(l4 - mx), jnp.exp(l16 - mx)
    den = e1 + e4 + e16
    w1, w4, w16 = e1 / den, e4 / den, e16 / den
    for h in range(N_HEADS):
        hs = slice(h * HEAD_DIM, (h + 1) * HEAD_DIM)
        hc = slice(h, h + 1)
        t, ls = divmod(h * HEAD_DIM, LANES)
        ls = slice(ls, ls + HEAD_DIM)
        oa = w1[:, hc] * o1_ref[:, hs] + w4[:, hc] * o4n[t, :, ls] + w16[:, hc] * o16n[t, :, ls]
        oa_scr[:, hs] = oa.astype(BF16)
    ya = jnp.dot(oa_scr[...], wbd_ref[...], preferred_element_type=F32)
    yb = jnp.dot(ob_ref[...], wbn_ref[...], preferred_element_type=F32)
    merged = g_ref[:, 0:D_MODEL].astype(F32) * ya + g_ref[:, D_MODEL:2 * D_MODEL].astype(F32) * yb
    z = jnp.dot(merged.astype(BF16), wout_ref[...], preferred_element_type=F32)
    x1 = x_ref[...] + gate_ref[...] * z
    x1_ref[...] = x1
    h2_ref[...] = _rms_modulate(x1, mod_ref).astype(BF16)


def _merge(x, o1, o4, o16, l1, l4, l16, ob, g, wbd, wbn, wout, gate1, mod2):
    tm = TM_MERGE
    nt = SEQ // tm
    row = lambda w: pl.BlockSpec((tm, w), lambda i: (i, 0))
    cls = lambda d, w: pl.BlockSpec((d, tm // d, w), lambda i: (0, i, 0))
    const = lambda shape: pl.BlockSpec(shape, lambda i: (0, 0))
    return pl.pallas_call(
        _merge_kernel,
        grid=(nt,),
        in_specs=[row(D_MODEL), row(WIDTH), cls(4, WIDTH), cls(16, WIDTH),
                  row(LANES), cls(4, LANES), cls(16, LANES), row(WIDTH), row(2 * D_MODEL),
                  const((WIDTH, D_MODEL)), const((WIDTH, D_MODEL)), const((D_MODEL, D_MODEL)),
                  const((1, D_MODEL)), const((3, D_MODEL))],
        out_specs=[row(D_MODEL), row(D_MODEL)],
        out_shape=[jax.ShapeDtypeStruct((SEQ, D_MODEL), F32), jax.ShapeDtypeStruct((SEQ, D_MODEL), BF16)],
        scratch_shapes=[pltpu.VMEM((WIDTH // LANES, tm, LANES), F32), pltpu.VMEM((WIDTH // LANES, tm, LANES), F32),
                        pltpu.VMEM((1, tm, LANES), F32), pltpu.VMEM((1, tm, LANES), F32),
                        pltpu.VMEM((tm, WIDTH), BF16)],
        compiler_params=_params(("arbitrary",)),
        name="merge",
    )(x, o1, o4.reshape(4, SEQ // 4, WIDTH), o16.reshape(16, SEQ // 16, WIDTH),
      l1, l4.reshape(4, SEQ // 4, LANES), l16.reshape(16, SEQ // 16, LANES), ob, g,
      wbd, wbn, wout, gate1, mod2)


def _ffn_kernel(x1_ref, hm_ref, hp_ref, hn_ref, wv_ref, wg_ref, cwv_ref, cwg_ref, cbv_ref, cbg_ref,
                wd_ref, gate_ref, gfin_ref, y_ref, hext, acc):
    tm = x1_ref.shape[0]
    i, j = pl.program_id(0), pl.program_id(1)

    @pl.when(j == 0)
    def _():
        hext[0:HALO, :] = jnp.where(i == 0, jnp.zeros_like(hp_ref), hp_ref[...])
        hext[HALO:HALO + tm, :] = hm_ref[...]
        hext[HALO + tm:HALO + tm + HALO, :] = jnp.where(i == pl.num_programs(0) - 1,
                                                        jnp.zeros_like(hn_ref), hn_ref[...])
        acc[...] = jnp.zeros_like(acc)

    def conv(w_ref, cw_ref, cb_ref):
        p = jnp.dot(hext[...], w_ref[...], preferred_element_type=F32)
        return (cw_ref[0:1, :] * p[HALO - 1:HALO - 1 + tm] + cw_ref[1:2, :] * p[HALO:HALO + tm]
                + cw_ref[2:3, :] * p[HALO + 1:HALO + 1 + tm] + cb_ref[...])

    u_val = conv(wv_ref, cwv_ref, cbv_ref)
    u_gate = conv(wg_ref, cwg_ref, cbg_ref)
    act = (jax.nn.gelu(u_gate, approximate=True) * u_val).astype(BF16)
    acc[...] += jnp.dot(act, wd_ref[...], preferred_element_type=F32)

    @pl.when(j == pl.num_programs(1) - 1)
    def _():
        x2 = x1_ref[...] + gate_ref[...] * acc[...]
        y = x2 * lax.rsqrt(jnp.mean(x2 * x2, axis=-1, keepdims=True) + RMS_EPS)
        y_ref[...] = y * gfin_ref[...]


def _ffn(x1, h2, w_up, conv_w, conv_b, w_down, gate2, g_final):
    tm, tf = TM_FFN, TF_FFN
    nt, nf = SEQ // tm, D_FF // tf
    per = tm // HALO
    nhalo = SEQ // HALO
    return pl.pallas_call(
        _ffn_kernel,
        grid=(nt, nf),
        in_specs=[
            pl.BlockSpec((tm, D_MODEL), lambda i, j: (i, 0)),
            pl.BlockSpec((tm, D_MODEL), lambda i, j: (i, 0)),
            pl.BlockSpec((HALO, D_MODEL), lambda i, j: (jnp.maximum(i * per - 1, 0), 0)),
            pl.BlockSpec((HALO, D_MODEL), lambda i, j: (jnp.minimum((i + 1) * per, nhalo - 1), 0)),
            pl.BlockSpec((D_MODEL, tf), lambda i, j: (0, j)),
            pl.BlockSpec((D_MODEL, tf), lambda i, j: (0, j + nf)),
            pl.BlockSpec((3, tf), lambda i, j: (0, j)),
            pl.BlockSpec((3, tf), lambda i, j: (0, j + nf)),
            pl.BlockSpec((1, tf), lambda i, j: (0, j)),
            pl.BlockSpec((1, tf), lambda i, j: (0, j + nf)),
            pl.BlockSpec((tf, D_MODEL), lambda i, j: (j, 0)),
            pl.BlockSpec((1, D_MODEL), lambda i, j: (0, 0)),
            pl.BlockSpec((1, D_MODEL), lambda i, j: (0, 0)),
        ],
        out_specs=pl.BlockSpec((tm, D_MODEL), lambda i, j: (i, 0)),
        out_shape=jax.ShapeDtypeStruct((SEQ, D_MODEL), F32),
        scratch_shapes=[pltpu.VMEM((tm + 2 * HALO, D_MODEL), BF16), pltpu.VMEM((tm, D_MODEL), F32)],
        compiler_params=_params(("arbitrary", "arbitrary")),
        name="ffn",
    )(x1, h2, h2, h2, w_up, w_up, conv_w, conv_w, conv_b, conv_b, w_down, gate2, g_final)


def kernel(x, c, w_ada, b_ada, g_mix, w_in, rel_bias, na_rpb, w_branch_dil, w_branch_na, w_out, g_ffn,
           w_up, conv_w, conv_b, w_down, g_final):
    assert x.shape == (1, SEQ, D_MODEL) and w_ada.shape[0] == 1
    xs = x[0]
    ada = _ada(c, w_ada[0], b_ada[0])
    shift1, scale1, gate1, shift2, scale2, gate2 = [ada[:, k * D_MODEL:(k + 1) * D_MODEL] for k in range(6)]
    mod1 = jnp.concatenate([g_mix[0][None], scale1, shift1], axis=0)
    mod2 = jnp.concatenate([g_ffn[0][None], scale2, shift2], axis=0)

    q_scale = np.ones((IN_COLS,), np.float32)
    q_scale[0:WIDTH] = HEAD_DIM ** -0.5
    q_scale[3 * WIDTH:4 * WIDTH] = HEAD_DIM ** -0.5
    w_in_b = (w_in[0] * q_scale).astype(BF16)

    a_nat, a_m4, a_m16, b_qkv, gates = _inproj(xs, mod1, w_in_b)

    outs = []
    for d, arr in zip(DILATIONS, (a_nat, a_m4, a_m16)):
        outs.append(_band_attention(arr.reshape(SEQ, 3 * WIDTH), _band_bias(rel_bias, d), SEQ // d))
    (o1, l1), (o4, l4), (o16, l16) = outs
    ob = _na_attention(b_qkv, _na_bias(na_rpb[0]))

    x1, h2 = _merge(xs, o1, o4, o16, l1, l4, l16, ob, gates,
                    w_branch_dil[0].astype(BF16), w_branch_na[0].astype(BF16), w_out[0].astype(BF16),
                    gate1, mod2)
    y = _ffn(x1, h2, w_up[0].astype(BF16), conv_w[0], conv_b[0].reshape(1, 2 * D_FF),
             w_down[0].astype(BF16), gate2, g_final.reshape(1, D_MODEL))
    return y[None]
```

```python
import functools
import math

import numpy as np
import jax
import jax.numpy as jnp
from jax import lax
from jax.experimental import pallas as pl
from jax.experimental.pallas import tpu as pltpu

F32 = jnp.float32
BF16 = jnp.bfloat16

D_MODEL = 1024
SEQ = 16384
GRID_W = 64
ROWS = SEQ // GRID_W
HEAD_DIM = 64
N_HEADS = 8
WIDTH = N_HEADS * HEAD_DIM
DILATIONS = (1, 4, 16)
BAND = 64
NA_KH = 8
NA_COLS = 16
NA_ROWS_MAX = 8
T5_BUCKETS = 32
T5_MAX_DISTANCE = 1024
D_FF = 2816
RMS_EPS = 1e-6
IN_COLS = 6 * WIDTH + 2 * D_MODEL
NEG = -1e30

VMEM_LIMIT = 56 * 1024 * 1024

TM_IN = 512
TQ_BAND = 128
TM_MERGE = 512
TM_FFN = 512
TF_FFN = 256
HALO = 16
LANES = 128


def _params(sem):
    return pltpu.CompilerParams(dimension_semantics=sem, vmem_limit_bytes=VMEM_LIMIT)


def _ada_kernel(c_ref, w_ref, b_ref, o_ref):
    c = c_ref[...]
    s = c / (1.0 + jnp.exp(-c))
    o_ref[...] = jnp.sum(w_ref[...] * s, axis=0, keepdims=True) + b_ref[...]


def _ada(c, w_ada, b_ada):
    n_out = w_ada.shape[1]
    tn = D_MODEL
    return pl.pallas_call(
        _ada_kernel,
        grid=(n_out // tn,),
        in_specs=[
            pl.BlockSpec((D_MODEL, 1), lambda j: (0, 0)),
            pl.BlockSpec((D_MODEL, tn), lambda j: (0, j)),
            pl.BlockSpec((1, tn), lambda j: (0, j)),
        ],
        out_specs=pl.BlockSpec((1, tn), lambda j: (0, j)),
        out_shape=jax.ShapeDtypeStruct((1, n_out), F32),
        compiler_params=_params(("arbitrary",)),
        name="ada",
    )(c.reshape(D_MODEL, 1), w_ada, b_ada.reshape(1, n_out))


def _rms_modulate(x, mod_ref):
    y = x * lax.rsqrt(jnp.mean(x * x, axis=-1, keepdims=True) + RMS_EPS)
    return (y * mod_ref[0:1, :]) * (1.0 + mod_ref[1:2, :]) + mod_ref[2:3, :]


def _inproj_kernel(x_ref, mod_ref, w_ref, anat_ref, am4_ref, am16_ref, b_ref, g_ref, h_scr, res_scr):
    tm = x_ref.shape[0]
    h_scr[...] = _rms_modulate(x_ref[...], mod_ref).astype(BF16)
    for n in range(IN_COLS // WIDTH):
        cols = slice(n * WIDTH, (n + 1) * WIDTH)
        r = jnp.dot(h_scr[...], w_ref[:, cols], preferred_element_type=F32)
        if n < 3:
            anat_ref[:, cols] = r.astype(BF16)
            for t in range(WIDTH // LANES):
                res_scr[t] = r[:, t * LANES:(t + 1) * LANES]
            for d, ref in ((4, am4_ref), (16, am16_ref)):
                for c in range(d):
                    for t in range(WIDTH // LANES):
                        lanes = slice(n * WIDTH + t * LANES, n * WIDTH + (t + 1) * LANES)
                        ref[c, :, lanes] = res_scr[t, pl.ds(c, tm // d, stride=d), :].astype(BF16)
        elif n < 6:
            b_ref[:, (n - 3) * WIDTH:(n - 2) * WIDTH] = r.astype(BF16)
        else:
            g_ref[:, (n - 6) * WIDTH:(n - 5) * WIDTH] = (1.0 / (1.0 + jnp.exp(-r))).astype(BF16)


def _inproj(x, mod1, w_in):
    tm = TM_IN
    nt = SEQ // tm
    qkv = 3 * WIDTH
    return pl.pallas_call(
        _inproj_kernel,
        grid=(nt,),
        in_specs=[
            pl.BlockSpec((tm, D_MODEL), lambda i: (i, 0)),
            pl.BlockSpec((3, D_MODEL), lambda i: (0, 0)),
            pl.BlockSpec((D_MODEL, IN_COLS), lambda i: (0, 0), pipeline_mode=pl.Buffered(1)),
        ],
        out_specs=[
            pl.BlockSpec((tm, qkv), lambda i: (i, 0)),
            pl.BlockSpec((4, tm // 4, qkv), lambda i: (0, i, 0)),
            pl.BlockSpec((16, tm // 16, qkv), lambda i: (0, i, 0)),
            pl.BlockSpec((tm, qkv), lambda i: (i, 0)),
            pl.BlockSpec((tm, 2 * D_MODEL), lambda i: (i, 0)),
        ],
        out_shape=[
            jax.ShapeDtypeStruct((SEQ, qkv), BF16),
            jax.ShapeDtypeStruct((4, SEQ // 4, qkv), BF16),
            jax.ShapeDtypeStruct((16, SEQ // 16, qkv), BF16),
            jax.ShapeDtypeStruct((SEQ, qkv), BF16),
            jax.ShapeDtypeStruct((SEQ, 2 * D_MODEL), BF16),
        ],
        scratch_shapes=[pltpu.VMEM((tm, D_MODEL), BF16),
                        pltpu.VMEM((WIDTH // LANES, tm, LANES), F32)],
        compiler_params=_params(("arbitrary",)),
        name="inproj",
    )(x, mod1, w_in)


def _stack_head_pair(q):
    first = lax.broadcasted_iota(jnp.int32, q.shape, 1) < HEAD_DIM
    qf = q.astype(F32)
    return jnp.concatenate([jnp.where(first, qf, 0.0), jnp.where(first, 0.0, qf)], axis=0).astype(q.dtype)


def _unstack_head_pair(o):
    n = o.shape[0] // 2
    first = lax.broadcasted_iota(jnp.int32, (n, o.shape[1]), 1) < HEAD_DIM
    return jnp.where(first, o[0:n], o[n:2 * n])


def _band_kernel(q_ref, kl_ref, km_ref, kr_ref, vl_ref, vm_ref, vr_ref, bias_ref,
                 o_ref, lse_ref, kcat, vcat, *, seg_len):
    tq = q_ref.shape[0]
    nk = tq + 2 * BAND
    start = pl.program_id(0) * tq
    for cat, (l_ref, m_ref, r_ref) in ((kcat, (kl_ref, km_ref, kr_ref)), (vcat, (vl_ref, vm_ref, vr_ref))):
        cat[0:BAND, :] = l_ref[...]
        cat[BAND:BAND + tq, :] = m_ref[...]
        cat[BAND + tq:nk, :] = r_ref[...]
    left_out = (start % seg_len) == 0
    right_out = ((start + tq) % seg_len) == 0
    col = lax.broadcasted_iota(jnp.int32, (1, nk), 1)
    edge = jnp.where(((col < BAND) & left_out) | ((col >= BAND + tq) & right_out), NEG, 0.0)
    lse_ref[...] = jnp.zeros_like(lse_ref)
    for pr in range(N_HEADS // 2):
        ps = slice(pr * LANES, (pr + 1) * LANES)
        s = lax.dot_general(_stack_head_pair(q_ref[:, ps]), kcat[:, ps], (((1,), (1,)), ((), ())),
                            preferred_element_type=F32)
        s = s + bias_ref[pr] + edge
        m = jnp.max(s, axis=-1, keepdims=True)
        p = jnp.exp(s - m)
        l = jnp.sum(p, axis=-1, keepdims=True)
        o = jnp.dot(p.astype(BF16), vcat[:, ps], preferred_element_type=F32) / l
        o_ref[:, ps] = _unstack_head_pair(o)
        lse = m + jnp.log(l)
        lse_ref[:, 2 * pr:2 * pr + 1] = lse[0:tq]
        lse_ref[:, 2 * pr + 1:2 * pr + 2] = lse[tq:2 * tq]


def _band_attention(qkv, bias, seg_len):
    tq = TQ_BAND
    nt = SEQ // tq
    per = tq // BAND
    nhalo = SEQ // BAND
    left = lambda c: (lambda i: (jnp.maximum(i * per - 1, 0), c))
    right = lambda c: (lambda i: (jnp.minimum((i + 1) * per, nhalo - 1), c))
    main = lambda c: (lambda i: (i, c))
    kv_specs = []
    for c in (1, 2):
        kv_specs += [pl.BlockSpec((BAND, WIDTH), left(c)), pl.BlockSpec((tq, WIDTH), main(c)),
                     pl.BlockSpec((BAND, WIDTH), right(c))]
    nk = tq + 2 * BAND
    return pl.pallas_call(
        functools.partial(_band_kernel, seg_len=seg_len),
        grid=(nt,),
        in_specs=[pl.BlockSpec((tq, WIDTH), main(0))] + kv_specs
        + [pl.BlockSpec((N_HEADS // 2, 2 * tq, nk), lambda i: (0, 0, 0))],
        out_specs=[pl.BlockSpec((tq, WIDTH), lambda i: (i, 0)),
                   pl.BlockSpec((tq, LANES), lambda i: (i, 0))],
        out_shape=[jax.ShapeDtypeStruct((SEQ, WIDTH), F32),
                   jax.ShapeDtypeStruct((SEQ, LANES), F32)],
        scratch_shapes=[pltpu.VMEM((nk, WIDTH), BF16), pltpu.VMEM((nk, WIDTH), BF16)],
        compiler_params=_params(("arbitrary",)),
        name=f"band_l{seg_len}",
    )(qkv, qkv, qkv, qkv, qkv, qkv, qkv, bias.reshape(N_HEADS // 2, 2 * tq, nk))


def _t5_bucket(rel):
    n = T5_BUCKETS // 2
    max_exact = n // 2
    sign_part = jnp.where(rel > 0, n, 0)
    a = jnp.abs(rel)
    af = jnp.maximum(a, 1).astype(F32)
    large = max_exact + (jnp.log(af / max_exact) / math.log(T5_MAX_DISTANCE / max_exact)
                         * (n - max_exact)).astype(jnp.int32)
    large = jnp.minimum(large, n - 1)
    return sign_part + jnp.where(a < max_exact, a, large)


def _band_bias(rel_bias, dilation):
    tq = TQ_BAND
    delta = jnp.arange(tq + 2 * BAND)[None, :] - BAND - jnp.arange(tq)[:, None]
    b = _lookup(rel_bias.astype(F32).T, _t5_bucket(delta * dilation), T5_BUCKETS)
    return jnp.where((jnp.abs(delta) <= BAND)[None], b, NEG)


def _lookup(table, idx, n):
    onehot = (idx[..., None] == jnp.arange(n)).astype(F32)
    return jnp.einsum("...b,xyb->...xy", table, onehot, precision=lax.Precision.HIGHEST)


def _na_kernel(q_ref, kp_ref, kc_ref, kn_ref, vp_ref, vc_ref, vn_ref, bias_ref, o_ref, kcat, vcat):
    tile = q_ref.shape[0]
    i = pl.program_id(0)
    for cat, refs in ((kcat, (kp_ref, kc_ref, kn_ref)), (vcat, (vp_ref, vc_ref, vn_ref))):
        for n, ref in enumerate(refs):
            cat[n * tile:(n + 1) * tile, :] = ref[...]
    nkeys = NA_KH * GRID_W

    def row_body(t, carry):
        r = i * NA_KH + t
        rs = jnp.clip(r - NA_KH // 2, 0, ROWS - NA_KH)
        off = pl.multiple_of((rs - (i - 1) * NA_KH) * GRID_W, GRID_W)
        var = rs - r + (NA_KH - 1)
        qoff = pl.multiple_of(t * GRID_W, GRID_W)
        for pr in range(N_HEADS // 2):
            ps = slice(pr * LANES, (pr + 1) * LANES)
            s = lax.dot_general(_stack_head_pair(q_ref[pl.ds(qoff, GRID_W), ps]), kcat[pl.ds(off, nkeys), ps],
                                (((1,), (1,)), ((), ())), preferred_element_type=F32)
            s = s + bias_ref[var, pr]
            m = jnp.max(s, axis=-1, keepdims=True)
            p = jnp.exp(s - m)
            l = jnp.sum(p, axis=-1, keepdims=True)
            o = jnp.dot(p.astype(BF16), vcat[pl.ds(off, nkeys), ps], preferred_element_type=F32) / l
            o_ref[pl.ds(qoff, GRID_W), ps] = _unstack_head_pair(o).astype(o_ref.dtype)
        return carry

    lax.fori_loop(0, NA_KH, row_body, 0)


def _na_attention(qkv, bias):
    tile = NA_KH * GRID_W
    nt = SEQ // tile
    prev = lambda c: (lambda i: (jnp.maximum(i - 1, 0), c))
    cur = lambda c: (lambda i: (i, c))
    nxt = lambda c: (lambda i: (jnp.minimum(i + 1, nt - 1), c))
    kv_specs = [pl.BlockSpec((tile, WIDTH), f(c)) for c in (1, 2) for f in (prev, cur, nxt)]
    return pl.pallas_call(
        _na_kernel,
        grid=(nt,),
        in_specs=[pl.BlockSpec((tile, WIDTH), cur(0))] + kv_specs
        + [pl.BlockSpec(bias.shape, lambda i: (0, 0, 0, 0), pipeline_mode=pl.Buffered(1))],
        out_specs=pl.BlockSpec((tile, WIDTH), lambda i: (i, 0)),
        out_shape=jax.ShapeDtypeStruct((SEQ, WIDTH), BF16),
        scratch_shapes=[pltpu.VMEM((3 * tile, WIDTH), BF16), pltpu.VMEM((3 * tile, WIDTH), BF16)],
        compiler_params=_params(("arbitrary",)),
        name="na",
    )(qkv, qkv, qkv, qkv, qkv, qkv, qkv, bias)


def _na_bias(rpb):
    cq = jnp.arange(GRID_W)
    col_start = jnp.clip(cq - NA_COLS // 2, 0, GRID_W - NA_COLS)
    col_mask = (cq[None, :] >= col_start[:, None]) & (cq[None, :] < col_start[:, None] + NA_COLS)
    dc = jnp.clip(cq[None, :] - cq[:, None], -(NA_COLS - 1), NA_COLS - 1) + NA_COLS - 1
    e = _lookup(rpb.astype(F32), dc, 2 * NA_COLS - 1)
    e = jnp.where(col_mask[None, None], e, NEG)
    b = jnp.stack([e[:, v:v + NA_KH] for v in range(NA_KH)])
    return b.transpose(0, 1, 3, 2, 4).reshape(NA_KH, N_HEADS // 2, 2 * GRID_W, NA_KH * GRID_W)


def _merge_kernel(x_ref, o1_ref, o4_ref, o16_ref, l1_ref, l4_ref, l16_ref, ob_ref, g_ref,
                  wbd_ref, wbn_ref, wout_ref, gate_ref, mod_ref,
                  x1_ref, h2_ref, o4n, o16n, l4n, l16n, oa_scr):
    tm = x_ref.shape[0]
    for d, src, dst in ((4, o4_ref, o4n), (16, o16_ref, o16n), (4, l4_ref, l4n), (16, l16_ref, l16n)):
        for c in range(d):
            for t in range(dst.shape[0]):
                dst[t, pl.ds(c, tm // d, stride=d), :] = src[c, :, t * LANES:(t + 1) * LANES]
    l1, l4, l16 = l1_ref[...], l4n[0], l16n[0]
    mx = jnp.maximum(jnp.maximum(l1, l4), l16)
    e1, e4, e16 = jnp.exp(l1 - mx), jnp.exp(l4 - mx), jnp.exp(l16 - mx)
    den = e1 + e4 + e16
    w1, w4, w16 = e1 / den, e4 / den, e16 / den
    for h in range(N_HEADS):
        hs = slice(h * HEAD_DIM, (h + 1) * HEAD_DIM)
        hc = slice(h, h + 1)
        t, ls = divmod(h * HEAD_DIM, LANES)
        ls = slice(ls, ls + HEAD_DIM)
        oa = w1[:, hc] * o1_ref[:, hs] + w4[:, hc] * o4n[t, :, ls] + w16[:, hc] * o16n[t, :, ls]
        oa_scr[:, hs] = oa.astype(BF16)
    ya = jnp.dot(oa_scr[...], wbd_ref[...], preferred_element_type=F32)
    yb = jnp.dot(ob_ref[...], wbn_ref[...], preferred_element_type=F32)
    merged = g_ref[:, 0:D_MODEL].astype(F32) * ya + g_ref[:, D_MODEL:2 * D_MODEL].astype(F32) * yb
    z = jnp.dot(merged.astype(BF16), wout_ref[...], preferred_element_type=F32)
    x1 = x_ref[...] + gate_ref[...] * z
    x1_ref[...] = x1
    h2_ref[...] = _rms_modulate(x1, mod_ref).astype(BF16)


def _merge(x, o1, o4, o16, l1, l4, l16, ob, g, wbd, wbn, wout, gate1, mod2):
    tm = TM_MERGE
    nt = SEQ // tm
    row = lambda w: pl.BlockSpec((tm, w), lambda i: (i, 0))
    cls = lambda d, w: pl.BlockSpec((d, tm // d, w), lambda i: (0, i, 0))
    const = lambda shape: pl.BlockSpec(shape, lambda i: (0, 0))
    return pl.pallas_call(
        _merge_kernel,
        grid=(nt,),
        in_specs=[row(D_MODEL), row(WIDTH), cls(4, WIDTH), cls(16, WIDTH),
                  row(LANES), cls(4, LANES), cls(16, LANES), row(WIDTH), row(2 * D_MODEL),
                  const((WIDTH, D_MODEL)), const((WIDTH, D_MODEL)), const((D_MODEL, D_MODEL)),
                  const((1, D_MODEL)), const((3, D_MODEL))],
        out_specs=[row(D_MODEL), row(D_MODEL)],
        out_shape=[jax.ShapeDtypeStruct((SEQ, D_MODEL), F32), jax.ShapeDtypeStruct((SEQ, D_MODEL), BF16)],
        scratch_shapes=[pltpu.VMEM((WIDTH // LANES, tm, LANES), F32), pltpu.VMEM((WIDTH // LANES, tm, LANES), F32),
                        pltpu.VMEM((1, tm, LANES), F32), pltpu.VMEM((1, tm, LANES), F32),
                        pltpu.VMEM((tm, WIDTH), BF16)],
        compiler_params=_params(("arbitrary",)),
        name="merge",
    )(x, o1, o4.reshape(4, SEQ // 4, WIDTH), o16.reshape(16, SEQ // 16, WIDTH),
      l1, l4.reshape(4, SEQ // 4, LANES), l16.reshape(16, SEQ // 16, LANES), ob, g,
      wbd, wbn, wout, gate1, mod2)


def _ffn_kernel(x1_ref, hm_ref, hp_ref, hn_ref, wv_ref, wg_ref, cwv_ref, cwg_ref, cbv_ref, cbg_ref,
                wd_ref, gate_ref, gfin_ref, y_ref, hext, acc):
    tm = x1_ref.shape[0]
    i, j = pl.program_id(0), pl.program_id(1)

    @pl.when(j == 0)
    def _():
        hext[0:HALO, :] = jnp.where(i == 0, jnp.zeros_like(hp_ref), hp_ref[...])
        hext[HALO:HALO + tm, :] = hm_ref[...]
        hext[HALO + tm:HALO + tm + HALO, :] = jnp.where(i == pl.num_programs(0) - 1,
                                                        jnp.zeros_like(hn_ref), hn_ref[...])
        acc[...] = jnp.zeros_like(acc)

    def conv(w_ref, cw_ref, cb_ref):
        p = jnp.dot(hext[...], w_ref[...], preferred_element_type=F32)
        return (cw_ref[0:1, :] * p[HALO - 1:HALO - 1 + tm] + cw_ref[1:2, :] * p[HALO:HALO + tm]
                + cw_ref[2:3, :] * p[HALO + 1:HALO + 1 + tm] + cb_ref[...])

    u_val = conv(wv_ref, cwv_ref, cbv_ref)
    u_gate = conv(wg_ref, cwg_ref, cbg_ref)
    act = (jax.nn.gelu(u_gate, approximate=True) * u_val).astype(BF16)
    acc[...] += jnp.dot(act, wd_ref[...], preferred_element_type=F32)

    @pl.when(j == pl.num_programs(1) - 1)
    def _():
        x2 = x1_ref[...] + gate_ref[...] * acc[...]
        y = x2 * lax.rsqrt(jnp.mean(x2 * x2, axis=-1, keepdims=True) + RMS_EPS)
        y_ref[...] = y * gfin_ref[...]


def _ffn(x1, h2, w_up, conv_w, conv_b, w_down, gate2, g_final):
    tm, tf = TM_FFN, TF_FFN
    nt, nf = SEQ // tm, D_FF // tf
    per = tm // HALO
    nhalo = SEQ // HALO
    return pl.pallas_call(
        _ffn_kernel,
        grid=(nt, nf),
        in_specs=[
            pl.BlockSpec((tm, D_MODEL), lambda i, j: (i, 0)),
            pl.BlockSpec((tm, D_MODEL), lambda i, j: (i, 0)),
            pl.BlockSpec((HALO, D_MODEL), lambda i, j: (jnp.maximum(i * per - 1, 0), 0)),
            pl.BlockSpec((HALO, D_MODEL), lambda i, j: (jnp.minimum((i + 1) * per, nhalo - 1), 0)),
            pl.BlockSpec((D_MODEL, tf), lambda i, j: (0, j)),
            pl.BlockSpec((D_MODEL, tf), lambda i, j: (0, j + nf)),
            pl.BlockSpec((3, tf), lambda i, j: (0, j)),
            pl.BlockSpec((3, tf), lambda i, j: (0, j + nf)),
            pl.BlockSpec((1, tf), lambda i, j: (0, j)),
            pl.BlockSpec((1, tf), lambda i, j: (0, j + nf)),
            pl.BlockSpec((tf, D_MODEL), lambda i, j: (j, 0)),
            pl.BlockSpec((1, D_MODEL), lambda i, j: (0, 0)),
            pl.BlockSpec((1, D_MODEL), lambda i, j: (0, 0)),
        ],
        out_specs=pl.BlockSpec((tm, D_MODEL), lambda i, j: (i, 0)),
        out_shape=jax.ShapeDtypeStruct((SEQ, D_MODEL), F32),
        scratch_shapes=[pltpu.VMEM((tm + 2 * HALO, D_MODEL), BF16), pltpu.VMEM((tm, D_MODEL), F32)],
        compiler_params=_params(("arbitrary", "arbitrary")),
        name="ffn",
    )(x1, h2, h2, h2, w_up, w_up, conv_w, conv_w, conv_b, conv_b, w_down, gate2, g_final)


def kernel(x, c, w_ada, b_ada, g_mix, w_in, rel_bias, na_rpb, w_branch_dil, w_branch_na, w_out, g_ffn,
           w_up, conv_w, conv_b, w_down, g_final):
    assert x.shape == (1, SEQ, D_MODEL) and w_ada.shape[0] == 1
    xs = x[0]
    ada = _ada(c, w_ada[0], b_ada[0])
    shift1, scale1, gate1, shift2, scale2, gate2 = [ada[:, k * D_MODEL:(k + 1) * D_MODEL] for k in range(6)]
    mod1 = jnp.concatenate([g_mix[0][None], scale1, shift1], axis=0)
    mod2 = jnp.concatenate([g_ffn[0][None], scale2, shift2], axis=0)

    q_scale = np.ones((IN_COLS,), np.float32)
    q_scale[0:WIDTH] = HEAD_DIM ** -0.5
    q_scale[3 * WIDTH:4 * WIDTH] = HEAD_DIM ** -0.5
    w_in_b = (w_in[0] * q_scale).astype(BF16)

    a_nat, a_m4, a_m16, b_qkv, gates = _inproj(xs, mod1, w_in_b)

    outs = []
    for d, arr in zip(DILATIONS, (a_nat, a_m4, a_m16)):
        outs.append(_band_attention(arr.reshape(SEQ, 3 * WIDTH), _band_bias(rel_bias, d), SEQ // d))
    (o1, l1), (o4, l4), (o16, l16) = outs
    ob = _na_attention(b_qkv, _na_bias(na_rpb[0]))

    x1, h2 = _merge(xs, o1, o4, o16, l1, l4, l16, ob, gates,
                    w_branch_dil[0].astype(BF16), w_branch_na[0].astype(BF16), w_out[0].astype(BF16),
                    gate1, mod2)
    y = _ffn(x1, h2, w_up[0].astype(BF16), conv_w[0], conv_b[0].reshape(1, 2 * D_FF),
             w_down[0].astype(BF16), gate2, g_final.reshape(1, D_MODEL))
    return y[None]
```

```python
import functools
import math

import numpy as np
import jax
import jax.numpy as jnp
from jax import lax
from jax.experimental import pallas as pl
from jax.experimental.pallas import tpu as pltpu

F32 = jnp.float32
BF16 = jnp.bfloat16

D_MODEL = 1024
SEQ = 16384
GRID_W = 64
ROWS = SEQ // GRID_W
HEAD_DIM = 64
N_HEADS = 8
WIDTH = N_HEADS * HEAD_DIM
DILATIONS = (1, 4, 16)
BAND = 64
NA_KH = 8
NA_COLS = 16
NA_ROWS_MAX = 8
T5_BUCKETS = 32
T5_MAX_DISTANCE = 1024
D_FF = 2816
RMS_EPS = 1e-6
IN_COLS = 6 * WIDTH + 2 * D_MODEL
NEG = -1e30

VMEM_LIMIT = 56 * 1024 * 1024

TM_IN = 512
TQ_BAND = 128
TM_MERGE = 512
TM_FFN = 512
TF_FFN = 256
HALO = 16
LANES = 128


def _params(sem):
    return pltpu.CompilerParams(dimension_semantics=sem, vmem_limit_bytes=VMEM_LIMIT)


def _ada_kernel(c_ref, w_ref, b_ref, o_ref):
    c = c_ref[...]
    s = c / (1.0 + jnp.exp(-c))
    o_ref[...] = jnp.sum(w_ref[...] * s, axis=0, keepdims=True) + b_ref[...]


def _ada(c, w_ada, b_ada):
    n_out = w_ada.shape[1]
    tn = D_MODEL
    return pl.pallas_call(
        _ada_kernel,
        grid=(n_out // tn,),
        in_specs=[
            pl.BlockSpec((D_MODEL, 1), lambda j: (0, 0)),
            pl.BlockSpec((D_MODEL, tn), lambda j: (0, j)),
            pl.BlockSpec((1, tn), lambda j: (0, j)),
        ],
        out_specs=pl.BlockSpec((1, tn), lambda j: (0, j)),
        out_shape=jax.ShapeDtypeStruct((1, n_out), F32),
        compiler_params=_params(("arbitrary",)),
        name="ada",
    )(c.reshape(D_MODEL, 1), w_ada, b_ada.reshape(1, n_out))


def _rms_modulate(x, mod_ref):
    y = x * lax.rsqrt(jnp.mean(x * x, axis=-1, keepdims=True) + RMS_EPS)
    return (y * mod_ref[0:1, :]) * (1.0 + mod_ref[1:2, :]) + mod_ref[2:3, :]


def _inproj_kernel(x_ref, mod_ref, w_ref, anat_ref, am4_ref, am16_ref, b_ref, g_ref, h_scr, res_scr):
    tm = x_ref.shape[0]
    h_scr[...] = _rms_modulate(x_ref[...], mod_ref).astype(BF16)
    for n in range(IN_COLS // WIDTH):
        cols = slice(n * WIDTH, (n + 1) * WIDTH)
        r = jnp.dot(h_scr[...], w_ref[:, cols], preferred_element_type=F32)
        if n < 3:
            anat_ref[:, cols] = r.astype(BF16)
            for t in range(WIDTH // LANES):
                res_scr[t] = r[:, t * LANES:(t + 1) * LANES]
            for d, ref in ((4, am4_ref), (16, am16_ref)):
                for c in range(d):
                    for t in range(WIDTH // LANES):
                        lanes = slice(n * WIDTH + t * LANES, n * WIDTH + (t + 1) * LANES)
                        ref[c, :, lanes] = res_scr[t, pl.ds(c, tm // d, stride=d), :].astype(BF16)
        elif n < 6:
            b_ref[:, (n - 3) * WIDTH:(n - 2) * WIDTH] = r.astype(BF16)
        else:
            g_ref[:, (n - 6) * WIDTH:(n - 5) * WIDTH] = (1.0 / (1.0 + jnp.exp(-r))).astype(BF16)


def _inproj(x, mod1, w_in):
    tm = TM_IN
    nt = SEQ // tm
    qkv = 3 * WIDTH
    return pl.pallas_call(
        _inproj_kernel,
        grid=(nt,),
        in_specs=[
            pl.BlockSpec((tm, D_MODEL), lambda i: (i, 0)),
            pl.BlockSpec((3, D_MODEL), lambda i: (0, 0)),
            pl.BlockSpec((D_MODEL, IN_COLS), lambda i: (0, 0), pipeline_mode=pl.Buffered(1)),
        ],
        out_specs=[
            pl.BlockSpec((tm, qkv), lambda i: (i, 0)),
            pl.BlockSpec((4, tm // 4, qkv), lambda i: (0, i, 0)),
            pl.BlockSpec((16, tm // 16, qkv), lambda i: (0, i, 0)),
            pl.BlockSpec((tm, qkv), lambda i: (i, 0)),
            pl.BlockSpec((tm, 2 * D_MODEL), lambda i: (i, 0)),
        ],
        out_shape=[
            jax.ShapeDtypeStruct((SEQ, qkv), BF16),
            jax.ShapeDtypeStruct((4, SEQ // 4, qkv), BF16),
            jax.ShapeDtypeStruct((16, SEQ // 16, qkv), BF16),
            jax.ShapeDtypeStruct((SEQ, qkv), BF16),
            jax.ShapeDtypeStruct((SEQ, 2 * D_MODEL), BF16),
        ],
        scratch_shapes=[pltpu.VMEM((tm, D_MODEL), BF16),
                        pltpu.VMEM((WIDTH // LANES, tm, LANES), F32)],
        compiler_params=_params(("arbitrary",)),
        name="inproj",
    )(x, mod1, w_in)


def _stack_head_pair(q):
    first = lax.broadcasted_iota(jnp.int32, q.shape, 1) < HEAD_DIM
    qf = q.astype(F32)
    return jnp.concatenate([jnp.where(first, qf, 0.0), jnp.where(first, 0.0, qf)], axis=0).astype(q.dtype)


def _unstack_head_pair(o):
    n = o.shape[0] // 2
    first = lax.broadcasted_iota(jnp.int32, (n, o.shape[1]), 1) < HEAD_DIM
    return jnp.where(first, o[0:n], o[n:2 * n])


def _band_kernel(q_ref, kl_ref, km_ref, kr_ref, vl_ref, vm_ref, vr_ref, bias_ref,
                 o_ref, lse_ref, kcat, vcat, *, seg_len):
    tq = q_ref.shape[0]
    nk = tq + 2 * BAND
    start = pl.program_id(0) * tq
    for cat, (l_ref, m_ref, r_ref) in ((kcat, (kl_ref, km_ref, kr_ref)), (vcat, (vl_ref, vm_ref, vr_ref))):
        cat[0:BAND, :] = l_ref[...]
        cat[BAND:BAND + tq, :] = m_ref[...]
        cat[BAND + tq:nk, :] = r_ref[...]
    left_out = (start % seg_len) == 0
    right_out = ((start + tq) % seg_len) == 0
    col = lax.broadcasted_iota(jnp.int32, (1, nk), 1)
    edge = jnp.where(((col < BAND) & left_out) | ((col >= BAND + tq) & right_out), NEG, 0.0)
    lse_ref[...] = jnp.zeros_like(lse_ref)
    for pr in range(N_HEADS // 2):
        ps = slice(pr * LANES, (pr + 1) * LANES)
        s = lax.dot_general(_stack_head_pair(q_ref[:, ps]), kcat[:, ps], (((1,), (1,)), ((), ())),
                            preferred_element_type=F32)
        s = s + bias_ref[pr] + edge
        m = jnp.max(s, axis=-1, keepdims=True)
        p = jnp.exp(s - m)
        l = jnp.sum(p, axis=-1, keepdims=True)
        o = jnp.dot(p.astype(BF16), vcat[:, ps], preferred_element_type=F32) / l
        o_ref[:, ps] = _unstack_head_pair(o)
        lse = m + jnp.log(l)
        lse_ref[:, 2 * pr:2 * pr + 1] = lse[0:tq]
        lse_ref[:, 2 * pr + 1:2 * pr + 2] = lse[tq:2 * tq]


def _band_attention(qkv, bias, seg_len):
    tq = TQ_BAND
    nt = SEQ // tq
    per = tq // BAND
    nhalo = SEQ // BAND
    left = lambda c: (lambda i: (jnp.maximum(i * per - 1, 0), c))
    right = lambda c: (lambda i: (jnp.minimum((i + 1) * per, nhalo - 1), c))
    main = lambda c: (lambda i: (i, c))
    kv_specs = []
    for c in (1, 2):
        kv_specs += [pl.BlockSpec((BAND, WIDTH), left(c)), pl.BlockSpec((tq, WIDTH), main(c)),
                     pl.BlockSpec((BAND, WIDTH), right(c))]
    nk = tq + 2 * BAND
    return pl.pallas_call(
        functools.partial(_band_kernel, seg_len=seg_len),
        grid=(nt,),
        in_specs=[pl.BlockSpec((tq, WIDTH), main(0))] + kv_specs
        + [pl.BlockSpec((N_HEADS // 2, 2 * tq, nk), lambda i: (0, 0, 0))],
        out_specs=[pl.BlockSpec((tq, WIDTH), lambda i: (i, 0)),
                   pl.BlockSpec((tq, LANES), lambda i: (i, 0))],
        out_shape=[jax.ShapeDtypeStruct((SEQ, WIDTH), F32),
                   jax.ShapeDtypeStruct((SEQ, LANES), F32)],
        scratch_shapes=[pltpu.VMEM((nk, WIDTH), BF16), pltpu.VMEM((nk, WIDTH), BF16)],
        compiler_params=_params(("arbitrary",)),
        name=f"band_l{seg_len}",
    )(qkv, qkv, qkv, qkv, qkv, qkv, qkv, bias.reshape(N_HEADS // 2, 2 * tq, nk))


def _t5_bucket(rel):
    n = T5_BUCKETS // 2
    max_exact = n // 2
    sign_part = jnp.where(rel > 0, n, 0)
    a = jnp.abs(rel)
    af = jnp.maximum(a, 1).astype(F32)
    large = max_exact + (jnp.log(af / max_exact) / math.log(T5_MAX_DISTANCE / max_exact)
                         * (n - max_exact)).astype(jnp.int32)
    large = jnp.minimum(large, n - 1)
    return sign_part + jnp.where(a < max_exact, a, large)


def _band_bias(rel_bias, dilation):
    tq = TQ_BAND
    delta = jnp.arange(tq + 2 * BAND)[None, :] - BAND - jnp.arange(tq)[:, None]
    b = _lookup(rel_bias.astype(F32).T, _t5_bucket(delta * dilation), T5_BUCKETS)
    return jnp.where((jnp.abs(delta) <= BAND)[None], b, NEG)


def _lookup(table, idx, n):
    onehot = (idx[..., None] == jnp.arange(n)).astype(F32)
    return jnp.einsum("...b,xyb->...xy", table, onehot, precision=lax.Precision.HIGHEST)


def _na_kernel(q_ref, kp_ref, kc_ref, kn_ref, vp_ref, vc_ref, vn_ref, bias_ref, o_ref, kcat, vcat):
    tile = q_ref.shape[0]
    i = pl.program_id(0)
    for cat, refs in ((kcat, (kp_ref, kc_ref, kn_ref)), (vcat, (vp_ref, vc_ref, vn_ref))):
        for n, ref in enumerate(refs):
            cat[n * tile:(n + 1) * tile, :] = ref[...]
    nkeys = NA_KH * GRID_W

    def row_body(t, carry):
        r = i * NA_KH + t
        rs = jnp.clip(r - NA_KH // 2, 0, ROWS - NA_KH)
        off = pl.multiple_of((rs - (i - 1) * NA_KH) * GRID_W, GRID_W)
        var = rs - r + (NA_KH - 1)
        qoff = pl.multiple_of(t * GRID_W, GRID_W)
        for pr in range(N_HEADS // 2):
            ps = slice(pr * LANES, (pr + 1) * LANES)
            s = lax.dot_general(_stack_head_pair(q_ref[pl.ds(qoff, GRID_W), ps]), kcat[pl.ds(off, nkeys), ps],
                                (((1,), (1,)), ((), ())), preferred_element_type=F32)
            s = s + bias_ref[var, pr]
            m = jnp.max(s, axis=-1, keepdims=True)
            p = jnp.exp(s - m)
            l = jnp.sum(p, axis=-1, keepdims=True)
            o = jnp.dot(p.astype(BF16), vcat[pl.ds(off, nkeys), ps], preferred_element_type=F32) / l
            o_ref[pl.ds(qoff, GRID_W), ps] = _unstack_head_pair(o).astype(o_ref.dtype)
        return carry

    lax.fori_loop(0, NA_KH, row_body, 0)


def _na_attention(qkv, bias):
    tile = NA_KH * GRID_W
    nt = SEQ // tile
    prev = lambda c: (lambda i: (jnp.maximum(i - 1, 0), c))
    cur = lambda c: (lambda i: (i, c))
    nxt = lambda c: (lambda i: (jnp.minimum(i + 1, nt - 1), c))
    kv_specs = [pl.BlockSpec((tile, WIDTH), f(c)) for c in (1, 2) for f in (prev, cur, nxt)]
    return pl.pallas_call(
        _na_kernel,
        grid=(nt,),
        in_specs=[pl.BlockSpec((tile, WIDTH), cur(0))] + kv_specs
        + [pl.BlockSpec(bias.shape, lambda i: (0, 0, 0, 0), pipeline_mode=pl.Buffered(1))],
        out_specs=pl.BlockSpec((tile, WIDTH), lambda i: (i, 0)),
        out_shape=jax.ShapeDtypeStruct((SEQ, WIDTH), BF16),
        scratch_shapes=[pltpu.VMEM((3 * tile, WIDTH), BF16), pltpu.VMEM((3 * tile, WIDTH), BF16)],
        compiler_params=_params(("arbitrary",)),
        name="na",
    )(qkv, qkv, qkv, qkv, qkv, qkv, qkv, bias)


def _na_bias(rpb):
    cq = jnp.arange(GRID_W)
    col_start = jnp.clip(cq - NA_COLS // 2, 0, GRID_W - NA_COLS)
    col_mask = (cq[None, :] >= col_start[:, None]) & (cq[None, :] < col_start[:, None] + NA_COLS)
    dc = jnp.clip(cq[None, :] - cq[:, None], -(NA_COLS - 1), NA_COLS - 1) + NA_COLS - 1
    e = _lookup(rpb.astype(F32), dc, 2 * NA_COLS - 1)
    e = jnp.where(col_mask[None, None], e, NEG)
    b = jnp.stack([e[:, v:v + NA_KH] for v in range(NA_KH)])
    return b.transpose(0, 1, 3, 2, 4).reshape(NA_KH, N_HEADS // 2, 2 * GRID_W, NA_KH * GRID_W)


def _merge_kernel(x_ref, o1_ref, o4_ref, o16_ref, l1_ref, l4_ref, l16_ref, ob_ref, g_ref,
                  wbd_ref, wbn_ref, wout_ref, gate_ref, mod_ref,
                  x1_ref, h2_ref, o4n, o16n, l4n, l16n, oa_scr):
    tm = x_ref.shape[0]
    for d, src, dst in ((4, o4_ref, o4n), (16, o16_ref, o16n), (4, l4_ref, l4n), (16, l16_ref, l16n)):
        for c in range(d):
            for t in range(dst.shape[0]):
                dst[t, pl.ds(c, tm // d, stride=d), :] = src[c, :, t * LANES:(t + 1) * LANES]
    l1, l4, l16 = l1_ref[...], l4n[0], l16n[0]
    mx = jnp.maximum(jnp.maximum(l1, l4), l16)
    e1, e4, e16 = jnp.exp(l1 - mx), jnp.exp(l4 - mx), jnp.exp(l16 - mx)
    den = e1 + e4 + e16
    w1, w4, w16 = e1 / den, e4 / den, e16 / den
    for h in range(N_HEADS):
        hs = slice(h * HEAD_DIM, (h + 1) * HEAD_DIM)
        hc = slice(h, h + 1)
        t, ls = divmod(h * HEAD_DIM, LANES)
        ls = slice(ls, ls + HEAD_DIM)
        oa = w1[:, hc] * o1_ref[:, hs] + w4[:, hc] * o4n[t, :, ls] + w16[:, hc] * o16n[t, :, ls]
        oa_scr[:, hs] = oa.astype(BF16)
    ya = jnp.dot(oa_scr[...], wbd_ref[...], preferred_element_type=F32)
    yb = jnp.dot(ob_ref[...], wbn_ref[...], preferred_element_type=F32)
    merged = g_ref[:, 0:D_MODEL].astype(F32) * ya + g_ref[:, D_MODEL:2 * D_MODEL].astype(F32) * yb
    z = jnp.dot(merged.astype(BF16), wout_ref[...], preferred_element_type=F32)
    x1 = x_ref[...] + gate_ref[...] * z
    x1_ref[...] = x1
    h2_ref[...] = _rms_modulate(x1, mod_ref).astype(BF16)


def _merge(x, o1, o4, o16, l1, l4, l16, ob, g, wbd, wbn, wout, gate1, mod2):
    tm = TM_MERGE
    nt = SEQ // tm
    row = lambda w: pl.BlockSpec((tm, w), lambda i: (i, 0))
    cls = lambda d, w: pl.BlockSpec((d, tm // d, w), lambda i: (0, i, 0))
    const = lambda shape: pl.BlockSpec(shape, lambda i: (0, 0))
    return pl.pallas_call(
        _merge_kernel,
        grid=(nt,),
        in_specs=[row(D_MODEL), row(WIDTH), cls(4, WIDTH), cls(16, WIDTH),
                  row(LANES), cls(4, LANES), cls(16, LANES), row(WIDTH), row(2 * D_MODEL),
                  const((WIDTH, D_MODEL)), const((WIDTH, D_MODEL)), const((D_MODEL, D_MODEL)),
                  const((1, D_MODEL)), const((3, D_MODEL))],
        out_specs=[row(D_MODEL), row(D_MODEL)],
        out_shape=[jax.ShapeDtypeStruct((SEQ, D_MODEL), F32), jax.ShapeDtypeStruct((SEQ, D_MODEL), BF16)],
        scratch_shapes=[pltpu.VMEM((WIDTH // LANES, tm, LANES), F32), pltpu.VMEM((WIDTH // LANES, tm, LANES), F32),
                        pltpu.VMEM((1, tm, LANES), F32), pltpu.VMEM((1, tm, LANES), F32),
                        pltpu.VMEM((tm, WIDTH), BF16)],
        compiler_params=_params(("arbitrary",)),
        name="merge",
    )(x, o1, o4.reshape(4, SEQ // 4, WIDTH), o16.reshape(16, SEQ // 16, WIDTH),
      l1, l4.reshape(4, SEQ // 4, LANES), l16.reshape(16, SEQ // 16, LANES), ob, g,
      wbd, wbn, wout, gate1, mod2)


def _ffn_kernel(x1_ref, hm_ref, hp_ref, hn_ref, wup_ref, cw_ref, cb_ref, wd_ref, gate_ref, gfin_ref,
                y_ref, hext, act):
    tm = x1_ref.shape[0]
    n = tm + 2 * HALO
    i = pl.program_id(0)
    hext[0:HALO, :] = jnp.where(i == 0, jnp.zeros_like(hp_ref), hp_ref[...])
    hext[HALO:HALO + tm, :] = hm_ref[...]
    hext[HALO + tm:n, :] = jnp.where(i == pl.num_programs(0) - 1, jnp.zeros_like(hn_ref), hn_ref[...])

    def conv(cols):
        p = jnp.dot(hext[...], wup_ref[:, cols], preferred_element_type=F32)
        prev = pltpu.roll(p, 1, 0)[HALO:HALO + tm]
        nxt = pltpu.roll(p, n - 1, 0)[HALO:HALO + tm]
        return (cw_ref[0:1, cols] * prev + cw_ref[1:2, cols] * p[HALO:HALO + tm] + cw_ref[2:3, cols] * nxt
                + cb_ref[:, cols])

    for c in range(D_FF // TF_FFN):
        val = slice(c * TF_FFN, (c + 1) * TF_FFN)
        gate = slice(D_FF + c * TF_FFN, D_FF + (c + 1) * TF_FFN)
        act[:, val] = (jax.nn.gelu(conv(gate), approximate=True) * conv(val)).astype(BF16)

    x2 = x1_ref[...] + gate_ref[...] * jnp.dot(act[...], wd_ref[...], preferred_element_type=F32)
    y = x2 * lax.rsqrt(jnp.mean(x2 * x2, axis=-1, keepdims=True) + RMS_EPS)
    y_ref[...] = y * gfin_ref[...]


def _ffn(x1, h2, w_up, conv_w, conv_b, w_down, gate2, g_final):
    tm = TM_FFN
    nt = SEQ // tm
    per = tm // HALO
    nhalo = SEQ // HALO
    row = lambda: pl.BlockSpec((tm, D_MODEL), lambda i: (i, 0))
    const = lambda a: pl.BlockSpec(a.shape, lambda i: (0, 0), pipeline_mode=pl.Buffered(1))
    return pl.pallas_call(
        _ffn_kernel,
        grid=(nt,),
        in_specs=[
            row(), row(),
            pl.BlockSpec((HALO, D_MODEL), lambda i: (jnp.maximum(i * per - 1, 0), 0)),
            pl.BlockSpec((HALO, D_MODEL), lambda i: (jnp.minimum((i + 1) * per, nhalo - 1), 0)),
            const(w_up), const(conv_w), const(conv_b), const(w_down), const(gate2), const(g_final),
        ],
        out_specs=row(),
        out_shape=jax.ShapeDtypeStruct((SEQ, D_MODEL), F32),
        scratch_shapes=[pltpu.VMEM((tm + 2 * HALO, D_MODEL), BF16), pltpu.VMEM((tm, D_FF), BF16)],
        compiler_params=_params(("arbitrary",)),
        name="ffn",
    )(x1, h2, h2, h2, w_up, conv_w, conv_b, w_down, gate2, g_final)


def kernel(x, c, w_ada, b_ada, g_mix, w_in, rel_bias, na_rpb, w_branch_dil, w_branch_na, w_out, g_ffn,
           w_up, conv_w, conv_b, w_down, g_final):
    assert x.shape == (1, SEQ, D_MODEL) and w_ada.shape[0] == 1
    xs = x[0]
    ada = _ada(c, w_ada[0], b_ada[0])
    shift1, scale1, gate1, shift2, scale2, gate2 = [ada[:, k * D_MODEL:(k + 1) * D_MODEL] for k in range(6)]
    mod1 = jnp.concatenate([g_mix[0][None], scale1, shift1], axis=0)
    mod2 = jnp.concatenate([g_ffn[0][None], scale2, shift2], axis=0)

    q_scale = np.ones((IN_COLS,), np.float32)
    q_scale[0:WIDTH] = HEAD_DIM ** -0.5
    q_scale[3 * WIDTH:4 * WIDTH] = HEAD_DIM ** -0.5
    w_in_b = (w_in[0] * q_scale).astype(BF16)

    a_nat, a_m4, a_m16, b_qkv, gates = _inproj(xs, mod1, w_in_b)

    outs = []
    for d, arr in zip(DILATIONS, (a_nat, a_m4, a_m16)):
        outs.append(_band_attention(arr.reshape(SEQ, 3 * WIDTH), _band_bias(rel_bias, d), SEQ // d))
    (o1, l1), (o4, l4), (o16, l16) = outs
    ob = _na_attention(b_qkv, _na_bias(na_rpb[0]))

    x1, h2 = _merge(xs, o1, o4, o16, l1, l4, l16, ob, gates,
                    w_branch_dil[0].astype(BF16), w_branch_na[0].astype(BF16), w_out[0].astype(BF16),
                    gate1, mod2)
    y = _ffn(x1, h2, w_up[0].astype(BF16), conv_w[0], conv_b[0].reshape(1, 2 * D_FF),
             w_down[0].astype(BF16), gate2, g_final.reshape(1, D_MODEL))
    return y[None]
```

```python
import functools
import math

import numpy as np
import jax
import jax.numpy as jnp
from jax import lax
from jax.experimental import pallas as pl
from jax.experimental.pallas import tpu as pltpu

F32 = jnp.float32
BF16 = jnp.bfloat16

D_MODEL = 1024
SEQ = 16384
GRID_W = 64
ROWS = SEQ // GRID_W
HEAD_DIM = 64
N_HEADS = 8
WIDTH = N_HEADS * HEAD_DIM
DILATIONS = (1, 4, 16)
BAND = 64
NA_KH = 8
NA_COLS = 16
NA_ROWS_MAX = 8
T5_BUCKETS = 32
T5_MAX_DISTANCE = 1024
D_FF = 2816
RMS_EPS = 1e-6
IN_COLS = 6 * WIDTH + 2 * D_MODEL
NEG = -1e30

VMEM_LIMIT = 56 * 1024 * 1024

TM_IN = 512
TQ_BAND = 512
SUB_BAND = 128
TM_MERGE = 512
TM_FFN = 512
TF_FFN = 256
HALO = 16
LANES = 128
NA_AHEAD = 3
BAND_AHEAD = 3


def _params(sem):
    return pltpu.CompilerParams(dimension_semantics=sem, vmem_limit_bytes=VMEM_LIMIT)


def _ada_kernel(c_ref, w_ref, b_ref, o_ref):
    c = c_ref[...]
    s = c / (1.0 + jnp.exp(-c))
    o_ref[...] = jnp.sum(w_ref[...] * s, axis=0, keepdims=True) + b_ref[...]


def _ada(c, w_ada, b_ada):
    n_out = w_ada.shape[1]
    tn = D_MODEL
    return pl.pallas_call(
        _ada_kernel,
        grid=(n_out // tn,),
        in_specs=[
            pl.BlockSpec((D_MODEL, 1), lambda j: (0, 0)),
            pl.BlockSpec((D_MODEL, tn), lambda j: (0, j)),
            pl.BlockSpec((1, tn), lambda j: (0, j)),
        ],
        out_specs=pl.BlockSpec((1, tn), lambda j: (0, j)),
        out_shape=jax.ShapeDtypeStruct((1, n_out), F32),
        compiler_params=_params(("arbitrary",)),
        name="ada",
    )(c.reshape(D_MODEL, 1), w_ada, b_ada.reshape(1, n_out))


def _rms_modulate(x, mod_ref):
    y = x * lax.rsqrt(jnp.mean(x * x, axis=-1, keepdims=True) + RMS_EPS)
    return (y * mod_ref[0:1, :]) * (1.0 + mod_ref[1:2, :]) + mod_ref[2:3, :]


def _inproj_kernel(x_ref, mod_ref, w_ref, anat_ref, am4_ref, am16_ref, b_ref, g_ref, h_scr, res_scr):
    tm = x_ref.shape[0]
    h_scr[...] = _rms_modulate(x_ref[...], mod_ref).astype(BF16)
    for n in range(IN_COLS // WIDTH):
        cols = slice(n * WIDTH, (n + 1) * WIDTH)
        r = jnp.dot(h_scr[...], w_ref[:, cols], preferred_element_type=F32)
        if n < 3:
            anat_ref[:, cols] = r.astype(BF16)
            for t in range(WIDTH // LANES):
                res_scr[t] = r[:, t * LANES:(t + 1) * LANES]
            for d, ref in ((4, am4_ref), (16, am16_ref)):
                for c in range(d):
                    for t in range(WIDTH // LANES):
                        lanes = slice(n * WIDTH + t * LANES, n * WIDTH + (t + 1) * LANES)
                        ref[c, :, lanes] = res_scr[t, pl.ds(c, tm // d, stride=d), :].astype(BF16)
        elif n < 6:
            b_ref[:, (n - 3) * WIDTH:(n - 2) * WIDTH] = r.astype(BF16)
        else:
            g_ref[:, (n - 6) * WIDTH:(n - 5) * WIDTH] = (1.0 / (1.0 + jnp.exp(-r))).astype(BF16)


def _inproj(x, mod1, w_in):
    tm = TM_IN
    nt = SEQ // tm
    qkv = 3 * WIDTH
    return pl.pallas_call(
        _inproj_kernel,
        grid=(nt,),
        in_specs=[
            pl.BlockSpec((tm, D_MODEL), lambda i: (i, 0)),
            pl.BlockSpec((3, D_MODEL), lambda i: (0, 0)),
            pl.BlockSpec((D_MODEL, IN_COLS), lambda i: (0, 0), pipeline_mode=pl.Buffered(1)),
        ],
        out_specs=[
            pl.BlockSpec((tm, qkv), lambda i: (i, 0)),
            pl.BlockSpec((4, tm // 4, qkv), lambda i: (0, i, 0)),
            pl.BlockSpec((16, tm // 16, qkv), lambda i: (0, i, 0)),
            pl.BlockSpec((tm, qkv), lambda i: (i, 0)),
            pl.BlockSpec((tm, 2 * D_MODEL), lambda i: (i, 0)),
        ],
        out_shape=[
            jax.ShapeDtypeStruct((SEQ, qkv), BF16),
            jax.ShapeDtypeStruct((4, SEQ // 4, qkv), BF16),
            jax.ShapeDtypeStruct((16, SEQ // 16, qkv), BF16),
            jax.ShapeDtypeStruct((SEQ, qkv), BF16),
            jax.ShapeDtypeStruct((SEQ, 2 * D_MODEL), BF16),
        ],
        scratch_shapes=[pltpu.VMEM((tm, D_MODEL), BF16),
                        pltpu.VMEM((WIDTH // LANES, tm, LANES), F32)],
        compiler_params=_params(("arbitrary",)),
        name="inproj",
    )(x, mod1, w_in)


def _stack_head_pair(q):
    first = lax.broadcasted_iota(jnp.int32, q.shape, 1) < HEAD_DIM
    qf = q.astype(F32)
    return jnp.concatenate([jnp.where(first, qf, 0.0), jnp.where(first, 0.0, qf)], axis=0).astype(q.dtype)


def _unstack_head_pair(o):
    n = o.shape[0] // 2
    first = lax.broadcasted_iota(jnp.int32, (n, o.shape[1]), 1) < HEAD_DIM
    return jnp.where(first, o[0:n], o[n:2 * n])


def _band_kernel(q_ref, kl_ref, km_ref, kr_ref, vl_ref, vm_ref, vr_ref, bias_ref,
                 o_ref, lse_ref, kcat, vcat, *, seg_len):
    tq = q_ref.shape[0]
    sub = SUB_BAND
    nsub = tq // sub
    nk = sub + 2 * BAND
    start = pl.program_id(0) * tq
    for cat, (l_ref, m_ref, r_ref) in ((kcat, (kl_ref, km_ref, kr_ref)), (vcat, (vl_ref, vm_ref, vr_ref))):
        cat[0:BAND, :] = l_ref[...]
        cat[BAND:BAND + tq, :] = m_ref[...]
        cat[BAND + tq:tq + 2 * BAND, :] = r_ref[...]
    col = lax.broadcasted_iota(jnp.int32, (1, nk), 1)
    edge = {0: jnp.where((col < BAND) & ((start % seg_len) == 0), NEG, 0.0),
            nsub - 1: jnp.where((col >= BAND + sub) & (((start + tq) % seg_len) == 0), NEG, 0.0)}
    lse_ref[...] = jnp.zeros_like(lse_ref)

    def scores(j, pr):
        ps = slice(pr * LANES, (pr + 1) * LANES)
        s = lax.dot_general(_stack_head_pair(q_ref[j * sub:(j + 1) * sub, ps]), kcat[j * sub:j * sub + nk, ps],
                            (((1,), (1,)), ((), ())), preferred_element_type=F32)
        s = s + bias_ref[pr]
        return s + edge[j] if j in edge else s

    def finish(j, pr, s):
        ps = slice(pr * LANES, (pr + 1) * LANES)
        rows = slice(j * sub, (j + 1) * sub)
        m = jnp.max(s, axis=-1, keepdims=True)
        p = jnp.exp(s - m)
        l = jnp.sum(p, axis=-1, keepdims=True)
        o = jnp.dot(p.astype(BF16), vcat[j * sub:j * sub + nk, ps], preferred_element_type=F32) / l
        o_ref[rows, ps] = _unstack_head_pair(o)
        lse = m + jnp.log(l)
        lse_ref[rows, 2 * pr:2 * pr + 1] = lse[0:sub]
        lse_ref[rows, 2 * pr + 1:2 * pr + 2] = lse[sub:2 * sub]

    pending = []
    for j in range(nsub):
        for pr in range(N_HEADS // 2):
            pending.append((j, pr, scores(j, pr)))
            if len(pending) > BAND_AHEAD:
                finish(*pending.pop(0))
    for unit in pending:
        finish(*unit)


def _band_attention(qkv, bias, seg_len):
    tq = TQ_BAND
    nt = SEQ // tq
    per = tq // BAND
    nhalo = SEQ // BAND
    left = lambda c: (lambda i: (jnp.maximum(i * per - 1, 0), c))
    right = lambda c: (lambda i: (jnp.minimum((i + 1) * per, nhalo - 1), c))
    main = lambda c: (lambda i: (i, c))
    kv_specs = []
    for c in (1, 2):
        kv_specs += [pl.BlockSpec((BAND, WIDTH), left(c)), pl.BlockSpec((tq, WIDTH), main(c)),
                     pl.BlockSpec((BAND, WIDTH), right(c))]
    nk = tq + 2 * BAND
    return pl.pallas_call(
        functools.partial(_band_kernel, seg_len=seg_len),
        grid=(nt,),
        in_specs=[pl.BlockSpec((tq, WIDTH), main(0))] + kv_specs
        + [pl.BlockSpec((N_HEADS // 2, 2 * SUB_BAND, SUB_BAND + 2 * BAND), lambda i: (0, 0, 0))],
        out_specs=[pl.BlockSpec((tq, WIDTH), lambda i: (i, 0)),
                   pl.BlockSpec((tq, LANES), lambda i: (i, 0))],
        out_shape=[jax.ShapeDtypeStruct((SEQ, WIDTH), F32),
                   jax.ShapeDtypeStruct((SEQ, LANES), F32)],
        scratch_shapes=[pltpu.VMEM((nk, WIDTH), BF16), pltpu.VMEM((nk, WIDTH), BF16)],
        compiler_params=_params(("arbitrary",)),
        name=f"band_l{seg_len}",
    )(qkv, qkv, qkv, qkv, qkv, qkv, qkv, bias.reshape(N_HEADS // 2, 2 * SUB_BAND, SUB_BAND + 2 * BAND))


def _t5_bucket(rel):
    n = T5_BUCKETS // 2
    max_exact = n // 2
    sign_part = jnp.where(rel > 0, n, 0)
    a = jnp.abs(rel)
    af = jnp.maximum(a, 1).astype(F32)
    large = max_exact + (jnp.log(af / max_exact) / math.log(T5_MAX_DISTANCE / max_exact)
                         * (n - max_exact)).astype(jnp.int32)
    large = jnp.minimum(large, n - 1)
    return sign_part + jnp.where(a < max_exact, a, large)


def _band_bias(rel_bias, dilation):
    tq = SUB_BAND
    delta = jnp.arange(tq + 2 * BAND)[None, :] - BAND - jnp.arange(tq)[:, None]
    b = _lookup(rel_bias.astype(F32).T, _t5_bucket(delta * dilation), T5_BUCKETS)
    return jnp.where((jnp.abs(delta) <= BAND)[None], b, NEG)


def _lookup(table, idx, n):
    onehot = (idx[..., None] == jnp.arange(n)).astype(F32)
    return jnp.einsum("...b,xyb->...xy", table, onehot, precision=lax.Precision.HIGHEST)


def _na_kernel(q_ref, kp_ref, kc_ref, kn_ref, vp_ref, vc_ref, vn_ref, bias_ref, o_ref, kcat, vcat):
    tile = q_ref.shape[0]
    i = pl.program_id(0)
    for cat, refs in ((kcat, (kp_ref, kc_ref, kn_ref)), (vcat, (vp_ref, vc_ref, vn_ref))):
        for n, ref in enumerate(refs):
            cat[n * tile:(n + 1) * tile, :] = ref[...]
    nkeys = NA_KH * GRID_W

    def window(t):
        r = i * NA_KH + t
        rs = jnp.clip(r - NA_KH // 2, 0, ROWS - NA_KH)
        return pl.multiple_of((rs - (i - 1) * NA_KH) * GRID_W, GRID_W), rs - r + (NA_KH - 1)

    def scores(t, pr, off, var):
        ps = slice(pr * LANES, (pr + 1) * LANES)
        s = lax.dot_general(_stack_head_pair(q_ref[t * GRID_W:(t + 1) * GRID_W, ps]), kcat[pl.ds(off, nkeys), ps],
                            (((1,), (1,)), ((), ())), preferred_element_type=F32)
        return s + bias_ref[var, pr]

    def finish(t, pr, off, s):
        ps = slice(pr * LANES, (pr + 1) * LANES)
        m = jnp.max(s, axis=-1, keepdims=True)
        p = jnp.exp(s - m)
        l = jnp.sum(p, axis=-1, keepdims=True)
        o = jnp.dot(p.astype(BF16), vcat[pl.ds(off, nkeys), ps], preferred_element_type=F32) / l
        o_ref[t * GRID_W:(t + 1) * GRID_W, ps] = _unstack_head_pair(o).astype(o_ref.dtype)

    units = [(t, pr) for t in range(NA_KH) for pr in range(N_HEADS // 2)]
    wins = [window(t) for t in range(NA_KH)]
    pending = []
    for t, pr in units:
        off, var = wins[t]
        pending.append((t, pr, off, scores(t, pr, off, var)))
        if len(pending) > NA_AHEAD:
            finish(*pending.pop(0))
    for unit in pending:
        finish(*unit)


def _na_attention(qkv, bias):
    tile = NA_KH * GRID_W
    nt = SEQ // tile
    prev = lambda c: (lambda i: (jnp.maximum(i - 1, 0), c))
    cur = lambda c: (lambda i: (i, c))
    nxt = lambda c: (lambda i: (jnp.minimum(i + 1, nt - 1), c))
    kv_specs = [pl.BlockSpec((tile, WIDTH), f(c)) for c in (1, 2) for f in (prev, cur, nxt)]
    return pl.pallas_call(
        _na_kernel,
        grid=(nt,),
        in_specs=[pl.BlockSpec((tile, WIDTH), cur(0))] + kv_specs
        + [pl.BlockSpec(bias.shape, lambda i: (0, 0, 0, 0), pipeline_mode=pl.Buffered(1))],
        out_specs=pl.BlockSpec((tile, WIDTH), lambda i: (i, 0)),
        out_shape=jax.ShapeDtypeStruct((SEQ, WIDTH), BF16),
        scratch_shapes=[pltpu.VMEM((3 * tile, WIDTH), BF16), pltpu.VMEM((3 * tile, WIDTH), BF16)],
        compiler_params=_params(("arbitrary",)),
        name="na",
    )(qkv, qkv, qkv, qkv, qkv, qkv, qkv, bias)


def _na_bias(rpb):
    cq = jnp.arange(GRID_W)
    col_start = jnp.clip(cq - NA_COLS // 2, 0, GRID_W - NA_COLS)
    col_mask = (cq[None, :] >= col_start[:, None]) & (cq[None, :] < col_start[:, None] + NA_COLS)
    dc = jnp.clip(cq[None, :] - cq[:, None], -(NA_COLS - 1), NA_COLS - 1) + NA_COLS - 1
    e = _lookup(rpb.astype(F32), dc, 2 * NA_COLS - 1)
    e = jnp.where(col_mask[None, None], e, NEG)
    b = jnp.stack([e[:, v:v + NA_KH] for v in range(NA_KH)])
    return b.transpose(0, 1, 3, 2, 4).reshape(NA_KH, N_HEADS // 2, 2 * GRID_W, NA_KH * GRID_W)


def _merge_kernel(x_ref, o1_ref, o4_ref, o16_ref, l1_ref, l4_ref, l16_ref, ob_ref, g_ref,
                  wbd_ref, wbn_ref, wout_ref, gate_ref, mod_ref,
                  x1_ref, h2_ref, o4n, o16n, l4n, l16n, oa_scr):
    tm = x_ref.shape[0]
    for d, src, dst in ((4, o4_ref, o4n), (16, o16_ref, o16n), (4, l4_ref, l4n), (16, l16_ref, l16n)):
        for c in range(d):
            for t in range(dst.shape[0]):
                dst[t, pl.ds(c, tm // d, stride=d), :] = src[c, :, t * LANES:(t + 1) * LANES]
    l1, l4, l16 = l1_ref[...], l4n[0], l16n[0]
    mx = jnp.maximum(jnp.maximum(l1, l4), l16)
    e1, e4, e16 = jnp.exp(l1 - mx), jnp.exp(l4 - mx), jnp.exp(l16 - mx)
    den = e1 + e4 + e16
    w1, w4, w16 = e1 / den, e4 / den, e16 / den
    for h in range(N_HEADS):
        hs = slice(h * HEAD_DIM, (h + 1) * HEAD_DIM)
        hc = slice(h, h + 1)
        t, ls = divmod(h * HEAD_DIM, LANES)
        ls = slice(ls, ls + HEAD_DIM)
        oa = w1[:, hc] * o1_ref[:, hs] + w4[:, hc] * o4n[t, :, ls] + w16[:, hc] * o16n[t, :, ls]
        oa_scr[:, hs] = oa.astype(BF16)
    ya = jnp.dot(oa_scr[...], wbd_ref[...], preferred_element_type=F32)
    yb = jnp.dot(ob_ref[...], wbn_ref[...], preferred_element_type=F32)
    merged = g_ref[:, 0:D_MODEL].astype(F32) * ya + g_ref[:, D_MODEL:2 * D_MODEL].astype(F32) * yb
    z = jnp.dot(merged.astype(BF16), wout_ref[...], preferred_element_type=F32)
    x1 = x_ref[...] + gate_ref[...] * z
    x1_ref[...] = x1
    h2_ref[...] = _rms_modulate(x1, mod_ref).astype(BF16)


def _merge(x, o1, o4, o16, l1, l4, l16, ob, g, wbd, wbn, wout, gate1, mod2):
    tm = TM_MERGE
    nt = SEQ // tm
    row = lambda w: pl.BlockSpec((tm, w), lambda i: (i, 0))
    cls = lambda d, w: pl.BlockSpec((d, tm // d, w), lambda i: (0, i, 0))
    const = lambda shape: pl.BlockSpec(shape, lambda i: (0, 0))
    return pl.pallas_call(
        _merge_kernel,
        grid=(nt,),
        in_specs=[row(D_MODEL), row(WIDTH), cls(4, WIDTH), cls(16, WIDTH),
                  row(LANES), cls(4, LANES), cls(16, LANES), row(WIDTH), row(2 * D_MODEL),
                  const((WIDTH, D_MODEL)), const((WIDTH, D_MODEL)), const((D_MODEL, D_MODEL)),
                  const((1, D_MODEL)), const((3, D_MODEL))],
        out_specs=[row(D_MODEL), row(D_MODEL)],
        out_shape=[jax.ShapeDtypeStruct((SEQ, D_MODEL), F32), jax.ShapeDtypeStruct((SEQ, D_MODEL), BF16)],
        scratch_shapes=[pltpu.VMEM((WIDTH // LANES, tm, LANES), F32), pltpu.VMEM((WIDTH // LANES, tm, LANES), F32),
                        pltpu.VMEM((1, tm, LANES), F32), pltpu.VMEM((1, tm, LANES), F32),
                        pltpu.VMEM((tm, WIDTH), BF16)],
        compiler_params=_params(("arbitrary",)),
        name="merge",
    )(x, o1, o4.reshape(4, SEQ // 4, WIDTH), o16.reshape(16, SEQ // 16, WIDTH),
      l1, l4.reshape(4, SEQ // 4, LANES), l16.reshape(16, SEQ // 16, LANES), ob, g,
      wbd, wbn, wout, gate1, mod2)


def _ffn_kernel(x1_ref, hm_ref, hp_ref, hn_ref, wup_ref, cw_ref, cb_ref, wd_ref, gate_ref, gfin_ref,
                y_ref, hext, act):
    tm = x1_ref.shape[0]
    n = tm + 2 * HALO
    i = pl.program_id(0)
    hext[0:HALO, :] = jnp.where(i == 0, jnp.zeros_like(hp_ref), hp_ref[...])
    hext[HALO:HALO + tm, :] = hm_ref[...]
    hext[HALO + tm:n, :] = jnp.where(i == pl.num_programs(0) - 1, jnp.zeros_like(hn_ref), hn_ref[...])

    def conv(cols):
        p = jnp.dot(hext[...], wup_ref[:, cols], preferred_element_type=F32)
        prev = pltpu.roll(p, 1, 0)[HALO:HALO + tm]
        nxt = pltpu.roll(p, n - 1, 0)[HALO:HALO + tm]
        return (cw_ref[0:1, cols] * prev + cw_ref[1:2, cols] * p[HALO:HALO + tm] + cw_ref[2:3, cols] * nxt
                + cb_ref[:, cols])

    for c in range(D_FF // TF_FFN):
        val = slice(c * TF_FFN, (c + 1) * TF_FFN)
        gate = slice(D_FF + c * TF_FFN, D_FF + (c + 1) * TF_FFN)
        act[:, val] = (jax.nn.gelu(conv(gate), approximate=True) * conv(val)).astype(BF16)

    x2 = x1_ref[...] + gate_ref[...] * jnp.dot(act[...], wd_ref[...], preferred_element_type=F32)
    y = x2 * lax.rsqrt(jnp.mean(x2 * x2, axis=-1, keepdims=True) + RMS_EPS)
    y_ref[...] = y * gfin_ref[...]


def _ffn(x1, h2, w_up, conv_w, conv_b, w_down, gate2, g_final):
    tm = TM_FFN
    nt = SEQ // tm
    per = tm // HALO
    nhalo = SEQ // HALO
    row = lambda: pl.BlockSpec((tm, D_MODEL), lambda i: (i, 0))
    const = lambda a: pl.BlockSpec(a.shape, lambda i: (0, 0), pipeline_mode=pl.Buffered(1))
    return pl.pallas_call(
        _ffn_kernel,
        grid=(nt,),
        in_specs=[
            row(), row(),
            pl.BlockSpec((HALO, D_MODEL), lambda i: (jnp.maximum(i * per - 1, 0), 0)),
            pl.BlockSpec((HALO, D_MODEL), lambda i: (jnp.minimum((i + 1) * per, nhalo - 1), 0)),
            const(w_up), const(conv_w), const(conv_b), const(w_down), const(gate2), const(g_final),
        ],
        out_specs=row(),
        out_shape=jax.ShapeDtypeStruct((SEQ, D_MODEL), F32),
        scratch_shapes=[pltpu.VMEM((tm + 2 * HALO, D_MODEL), BF16), pltpu.VMEM((tm, D_FF), BF16)],
        compiler_params=_params(("arbitrary",)),
        name="ffn",
    )(x1, h2, h2, h2, w_up, conv_w, conv_b, w_down, gate2, g_final)


def kernel(x, c, w_ada, b_ada, g_mix, w_in, rel_bias, na_rpb, w_branch_dil, w_branch_na, w_out, g_ffn,
           w_up, conv_w, conv_b, w_down, g_final):
    assert x.shape == (1, SEQ, D_MODEL) and w_ada.shape[0] == 1
    xs = x[0]
    ada = _ada(c, w_ada[0], b_ada[0])
    shift1, scale1, gate1, shift2, scale2, gate2 = [ada[:, k * D_MODEL:(k + 1) * D_MODEL] for k in range(6)]
    mod1 = jnp.concatenate([g_mix[0][None], scale1, shift1], axis=0)
    mod2 = jnp.concatenate([g_ffn[0][None], scale2, shift2], axis=0)

    q_scale = np.ones((IN_COLS,), np.float32)
    q_scale[0:WIDTH] = HEAD_DIM ** -0.5
    q_scale[3 * WIDTH:4 * WIDTH] = HEAD_DIM ** -0.5
    w_in_b = (w_in[0] * q_scale).astype(BF16)

    a_nat, a_m4, a_m16, b_qkv, gates = _inproj(xs, mod1, w_in_b)

    outs = []
    for d, arr in zip(DILATIONS, (a_nat, a_m4, a_m16)):
        outs.append(_band_attention(arr.reshape(SEQ, 3 * WIDTH), _band_bias(rel_bias, d), SEQ // d))
    (o1, l1), (o4, l4), (o16, l16) = outs
    ob = _na_attention(b_qkv, _na_bias(na_rpb[0]))

    x1, h2 = _merge(xs, o1, o4, o16, l1, l4, l16, ob, gates,
                    w_branch_dil[0].astype(BF16), w_branch_na[0].astype(BF16), w_out[0].astype(BF16),
                    gate1, mod2)
    y = _ffn(x1, h2, w_up[0].astype(BF16), conv_w[0], conv_b[0].reshape(1, 2 * D_FF),
             w_down[0].astype(BF16), gate2, g_final.reshape(1, D_MODEL))
    return y[None]
```

```python
import functools
import math

import numpy as np
import jax
import jax.numpy as jnp
from jax import lax
from jax.experimental import pallas as pl
from jax.experimental.pallas import tpu as pltpu

F32 = jnp.float32
BF16 = jnp.bfloat16

D_MODEL = 1024
SEQ = 16384
GRID_W = 64
ROWS = SEQ // GRID_W
HEAD_DIM = 64
N_HEADS = 8
WIDTH = N_HEADS * HEAD_DIM
DILATIONS = (1, 4, 16)
BAND = 64
NA_KH = 8
NA_COLS = 16
NA_ROWS_MAX = 8
T5_BUCKETS = 32
T5_MAX_DISTANCE = 1024
D_FF = 2816
RMS_EPS = 1e-6
IN_COLS = 6 * WIDTH + 2 * D_MODEL
NEG = -1e30

VMEM_LIMIT = 56 * 1024 * 1024

TM_IN = 512
TQ_BAND = 512
SUB_BAND = 128
TM_MERGE = 512
SUB_MERGE = 256
TM_FFN = 512
TF_FFN = 256
HALO = 16
LANES = 128
NA_AHEAD = 3
BAND_AHEAD = 3


def _params(sem):
    return pltpu.CompilerParams(dimension_semantics=sem, vmem_limit_bytes=VMEM_LIMIT)


def _ada_kernel(c_ref, w_ref, b_ref, o_ref):
    c = c_ref[...]
    s = c / (1.0 + jnp.exp(-c))
    o_ref[...] = jnp.sum(w_ref[...] * s, axis=0, keepdims=True) + b_ref[...]


def _ada(c, w_ada, b_ada):
    n_out = w_ada.shape[1]
    tn = D_MODEL
    return pl.pallas_call(
        _ada_kernel,
        grid=(n_out // tn,),
        in_specs=[
            pl.BlockSpec((D_MODEL, 1), lambda j: (0, 0)),
            pl.BlockSpec((D_MODEL, tn), lambda j: (0, j)),
            pl.BlockSpec((1, tn), lambda j: (0, j)),
        ],
        out_specs=pl.BlockSpec((1, tn), lambda j: (0, j)),
        out_shape=jax.ShapeDtypeStruct((1, n_out), F32),
        compiler_params=_params(("arbitrary",)),
        name="ada",
    )(c.reshape(D_MODEL, 1), w_ada, b_ada.reshape(1, n_out))


def _rms_modulate(x, mod_ref):
    y = x * lax.rsqrt(jnp.mean(x * x, axis=-1, keepdims=True) + RMS_EPS)
    return (y * mod_ref[0:1, :]) * (1.0 + mod_ref[1:2, :]) + mod_ref[2:3, :]


def _inproj_kernel(x0_ref, xn_ref, mod_ref, w_ref, anat_ref, am4_ref, am16_ref, b_ref, g_ref,
                   h_even, h_odd, res_scr, res4_scr):
    tm = xn_ref.shape[0]
    i = pl.program_id(0)

    @pl.when(i == 0)
    def _():
        h_even[...] = _rms_modulate(x0_ref[...], mod_ref).astype(BF16)

    def project(h_cur, h_next):
        for k, n in enumerate((6, 7, 8, 9, 0, 1, 2, 3, 4, 5)):
            cols = slice(n * WIDTH, (n + 1) * WIDTH)
            r = jnp.dot(h_cur[...], w_ref[:, cols], preferred_element_type=F32)
            if k == 2:
                h_next[...] = _rms_modulate(xn_ref[...], mod_ref).astype(BF16)
            if n < 3:
                anat_ref[:, cols] = r.astype(BF16)
                for t in range(WIDTH // LANES):
                    lanes = slice(n * WIDTH + t * LANES, n * WIDTH + (t + 1) * LANES)
                    res_scr[t] = r[:, t * LANES:(t + 1) * LANES]
                    for c in range(4):
                        cls4 = res_scr[t, pl.ds(c, tm // 4, stride=4), :]
                        am4_ref[c, :, lanes] = cls4.astype(BF16)
                        res4_scr[t, c] = cls4
                        for c2 in range(4):
                            am16_ref[c + 4 * c2, :, lanes] = (
                                res4_scr[t, c, pl.ds(c2, tm // 16, stride=4), :].astype(BF16))
            elif n < 6:
                b_ref[:, (n - 3) * WIDTH:(n - 2) * WIDTH] = r.astype(BF16)
            else:
                g_ref[:, (n - 6) * WIDTH:(n - 5) * WIDTH] = (1.0 / (1.0 + jnp.exp(-r))).astype(BF16)

    @pl.when(i % 2 == 0)
    def _():
        project(h_even, h_odd)

    @pl.when(i % 2 == 1)
    def _():
        project(h_odd, h_even)


def _inproj(x, mod1, w_in):
    tm = TM_IN
    nt = SEQ // tm
    qkv = 3 * WIDTH
    return pl.pallas_call(
        _inproj_kernel,
        grid=(nt,),
        in_specs=[
            pl.BlockSpec((tm, D_MODEL), lambda i: (0, 0)),
            pl.BlockSpec((tm, D_MODEL), lambda i: (jnp.minimum(i + 1, nt - 1), 0)),
            pl.BlockSpec((3, D_MODEL), lambda i: (0, 0)),
            pl.BlockSpec((D_MODEL, IN_COLS), lambda i: (0, 0), pipeline_mode=pl.Buffered(1)),
        ],
        out_specs=[
            pl.BlockSpec((tm, qkv), lambda i: (i, 0)),
            pl.BlockSpec((4, tm // 4, qkv), lambda i: (0, i, 0)),
            pl.BlockSpec((16, tm // 16, qkv), lambda i: (0, i, 0)),
            pl.BlockSpec((tm, qkv), lambda i: (i, 0)),
            pl.BlockSpec((tm, 2 * D_MODEL), lambda i: (i, 0)),
        ],
        out_shape=[
            jax.ShapeDtypeStruct((SEQ, qkv), BF16),
            jax.ShapeDtypeStruct((4, SEQ // 4, qkv), BF16),
            jax.ShapeDtypeStruct((16, SEQ // 16, qkv), BF16),
            jax.ShapeDtypeStruct((SEQ, qkv), BF16),
            jax.ShapeDtypeStruct((SEQ, 2 * D_MODEL), BF16),
        ],
        scratch_shapes=[pltpu.VMEM((tm, D_MODEL), BF16), pltpu.VMEM((tm, D_MODEL), BF16),
                        pltpu.VMEM((WIDTH // LANES, tm, LANES), F32),
                        pltpu.VMEM((WIDTH // LANES, 4, tm // 4, LANES), F32)],
        compiler_params=_params(("arbitrary",)),
        name="inproj",
    )(x, x, mod1, w_in)


def _stack_head_pair(q):
    first = lax.broadcasted_iota(jnp.int32, q.shape, 1) < HEAD_DIM
    qf = q.astype(F32)
    return jnp.concatenate([jnp.where(first, qf, 0.0), jnp.where(first, 0.0, qf)], axis=0).astype(q.dtype)


def _unstack_head_pair(o):
    n = o.shape[0] // 2
    first = lax.broadcasted_iota(jnp.int32, (n, o.shape[1]), 1) < HEAD_DIM
    return jnp.where(first, o[0:n], o[n:2 * n])


def _band_kernel(q_ref, kl_ref, km_ref, kr_ref, vl_ref, vm_ref, vr_ref, bias_ref,
                 o_ref, lse_ref, kcat, vcat, *, seg_len):
    tq = q_ref.shape[0]
    sub = SUB_BAND
    nsub = tq // sub
    nk = sub + 2 * BAND
    start = pl.program_id(0) * tq
    for cat, (l_ref, m_ref, r_ref) in ((kcat, (kl_ref, km_ref, kr_ref)), (vcat, (vl_ref, vm_ref, vr_ref))):
        cat[0:BAND, :] = l_ref[...]
        cat[BAND:BAND + tq, :] = m_ref[...]
        cat[BAND + tq:tq + 2 * BAND, :] = r_ref[...]
    col = lax.broadcasted_iota(jnp.int32, (1, nk), 1)
    edge = {0: jnp.where((col < BAND) & ((start % seg_len) == 0), NEG, 0.0),
            nsub - 1: jnp.where((col >= BAND + sub) & (((start + tq) % seg_len) == 0), NEG, 0.0)}
    lse_ref[...] = jnp.zeros_like(lse_ref)

    def scores(j, pr):
        ps = slice(pr * LANES, (pr + 1) * LANES)
        s = lax.dot_general(_stack_head_pair(q_ref[j * sub:(j + 1) * sub, ps]), kcat[j * sub:j * sub + nk, ps],
                            (((1,), (1,)), ((), ())), preferred_element_type=F32)
        s = s + bias_ref[pr]
        return s + edge[j] if j in edge else s

    def finish(j, pr, s):
        ps = slice(pr * LANES, (pr + 1) * LANES)
        rows = slice(j * sub, (j + 1) * sub)
        m = jnp.max(s, axis=-1, keepdims=True)
        p = jnp.exp(s - m)
        l = jnp.sum(p, axis=-1, keepdims=True)
        o = jnp.dot(p.astype(BF16), vcat[j * sub:j * sub + nk, ps], preferred_element_type=F32) / l
        o_ref[rows, ps] = _unstack_head_pair(o)
        lse = m + jnp.log(l)
        lse_ref[rows, 2 * pr:2 * pr + 1] = lse[0:sub]
        lse_ref[rows, 2 * pr + 1:2 * pr + 2] = lse[sub:2 * sub]

    pending = []
    for j in range(nsub):
        for pr in range(N_HEADS // 2):
            pending.append((j, pr, scores(j, pr)))
            if len(pending) > BAND_AHEAD:
                finish(*pending.pop(0))
    for unit in pending:
        finish(*unit)


def _band_attention(qkv, bias, seg_len):
    tq = TQ_BAND
    nt = SEQ // tq
    per = tq // BAND
    nhalo = SEQ // BAND
    left = lambda c: (lambda i: (jnp.maximum(i * per - 1, 0), c))
    right = lambda c: (lambda i: (jnp.minimum((i + 1) * per, nhalo - 1), c))
    main = lambda c: (lambda i: (i, c))
    kv_specs = []
    for c in (1, 2):
        kv_specs += [pl.BlockSpec((BAND, WIDTH), left(c)), pl.BlockSpec((tq, WIDTH), main(c)),
                     pl.BlockSpec((BAND, WIDTH), right(c))]
    nk = tq + 2 * BAND
    return pl.pallas_call(
        functools.partial(_band_kernel, seg_len=seg_len),
        grid=(nt,),
        in_specs=[pl.BlockSpec((tq, WIDTH), main(0))] + kv_specs
        + [pl.BlockSpec((N_HEADS // 2, 2 * SUB_BAND, SUB_BAND + 2 * BAND), lambda i: (0, 0, 0))],
        out_specs=[pl.BlockSpec((tq, WIDTH), lambda i: (i, 0)),
                   pl.BlockSpec((tq, LANES), lambda i: (i, 0))],
        out_shape=[jax.ShapeDtypeStruct((SEQ, WIDTH), F32),
                   jax.ShapeDtypeStruct((SEQ, LANES), F32)],
        scratch_shapes=[pltpu.VMEM((nk, WIDTH), BF16), pltpu.VMEM((nk, WIDTH), BF16)],
        compiler_params=_params(("arbitrary",)),
        name=f"band_l{seg_len}",
    )(qkv, qkv, qkv, qkv, qkv, qkv, qkv, bias.reshape(N_HEADS // 2, 2 * SUB_BAND, SUB_BAND + 2 * BAND))


def _t5_bucket(rel):
    n = T5_BUCKETS // 2
    max_exact = n // 2
    sign_part = jnp.where(rel > 0, n, 0)
    a = jnp.abs(rel)
    af = jnp.maximum(a, 1).astype(F32)
    large = max_exact + (jnp.log(af / max_exact) / math.log(T5_MAX_DISTANCE / max_exact)
                         * (n - max_exact)).astype(jnp.int32)
    large = jnp.minimum(large, n - 1)
    return sign_part + jnp.where(a < max_exact, a, large)


def _band_bias(rel_bias, dilation):
    tq = SUB_BAND
    delta = jnp.arange(tq + 2 * BAND)[None, :] - BAND - jnp.arange(tq)[:, None]
    b = _lookup(rel_bias.astype(F32).T, _t5_bucket(delta * dilation), T5_BUCKETS)
    return jnp.where((jnp.abs(delta) <= BAND)[None], b, NEG)


def _lookup(table, idx, n):
    onehot = (idx[..., None] == jnp.arange(n)).astype(F32)
    return jnp.einsum("...b,xyb->...xy", table, onehot, precision=lax.Precision.HIGHEST)


def _na_kernel(q_ref, kp_ref, kc_ref, kn_ref, vp_ref, vc_ref, vn_ref, bias_ref, o_ref, kcat, vcat):
    tile = q_ref.shape[0]
    i = pl.program_id(0)
    for cat, refs in ((kcat, (kp_ref, kc_ref, kn_ref)), (vcat, (vp_ref, vc_ref, vn_ref))):
        for n, ref in enumerate(refs):
            cat[n * tile:(n + 1) * tile, :] = ref[...]
    nkeys = NA_KH * GRID_W

    def window(t):
        r = i * NA_KH + t
        rs = jnp.clip(r - NA_KH // 2, 0, ROWS - NA_KH)
        return pl.multiple_of((rs - (i - 1) * NA_KH) * GRID_W, GRID_W), rs - r + (NA_KH - 1)

    def scores(t, pr, off, var):
        ps = slice(pr * LANES, (pr + 1) * LANES)
        s = lax.dot_general(_stack_head_pair(q_ref[t * GRID_W:(t + 1) * GRID_W, ps]), kcat[pl.ds(off, nkeys), ps],
                            (((1,), (1,)), ((), ())), preferred_element_type=F32)
        return s + bias_ref[var, pr]

    def finish(t, pr, off, s):
        ps = slice(pr * LANES, (pr + 1) * LANES)
        m = jnp.max(s, axis=-1, keepdims=True)
        p = jnp.exp(s - m)
        l = jnp.sum(p, axis=-1, keepdims=True)
        o = jnp.dot(p.astype(BF16), vcat[pl.ds(off, nkeys), ps], preferred_element_type=F32) / l
        o_ref[t * GRID_W:(t + 1) * GRID_W, ps] = _unstack_head_pair(o).astype(o_ref.dtype)

    units = [(t, pr) for t in range(NA_KH) for pr in range(N_HEADS // 2)]
    wins = [window(t) for t in range(NA_KH)]
    pending = []
    for t, pr in units:
        off, var = wins[t]
        pending.append((t, pr, off, scores(t, pr, off, var)))
        if len(pending) > NA_AHEAD:
            finish(*pending.pop(0))
    for unit in pending:
        finish(*unit)


def _na_attention(qkv, bias):
    tile = NA_KH * GRID_W
    nt = SEQ // tile
    prev = lambda c: (lambda i: (jnp.maximum(i - 1, 0), c))
    cur = lambda c: (lambda i: (i, c))
    nxt = lambda c: (lambda i: (jnp.minimum(i + 1, nt - 1), c))
    kv_specs = [pl.BlockSpec((tile, WIDTH), f(c)) for c in (1, 2) for f in (prev, cur, nxt)]
    return pl.pallas_call(
        _na_kernel,
        grid=(nt,),
        in_specs=[pl.BlockSpec((tile, WIDTH), cur(0))] + kv_specs
        + [pl.BlockSpec(bias.shape, lambda i: (0, 0, 0, 0), pipeline_mode=pl.Buffered(1))],
        out_specs=pl.BlockSpec((tile, WIDTH), lambda i: (i, 0)),
        out_shape=jax.ShapeDtypeStruct((SEQ, WIDTH), BF16),
        scratch_shapes=[pltpu.VMEM((3 * tile, WIDTH), BF16), pltpu.VMEM((3 * tile, WIDTH), BF16)],
        compiler_params=_params(("arbitrary",)),
        name="na",
    )(qkv, qkv, qkv, qkv, qkv, qkv, qkv, bias)


def _na_bias(rpb):
    cq = jnp.arange(GRID_W)
    col_start = jnp.clip(cq - NA_COLS // 2, 0, GRID_W - NA_COLS)
    col_mask = (cq[None, :] >= col_start[:, None]) & (cq[None, :] < col_start[:, None] + NA_COLS)
    dc = jnp.clip(cq[None, :] - cq[:, None], -(NA_COLS - 1), NA_COLS - 1) + NA_COLS - 1
    e = _lookup(rpb.astype(F32), dc, 2 * NA_COLS - 1)
    e = jnp.where(col_mask[None, None], e, NEG)
    b = jnp.stack([e[:, v:v + NA_KH] for v in range(NA_KH)])
    return b.transpose(0, 1, 3, 2, 4).reshape(NA_KH, N_HEADS // 2, 2 * GRID_W, NA_KH * GRID_W)


def _expand_heads(w, exp_ref):
    hi = w.astype(BF16)
    lo = (w - hi.astype(F32)).astype(BF16)
    return (jnp.dot(hi, exp_ref[...], preferred_element_type=F32)
            + jnp.dot(lo, exp_ref[...], preferred_element_type=F32))


def _merge_kernel(x_ref, o1_ref, o4_ref, o16_ref, l1_ref, l4_ref, l16_ref, ob_ref, g_ref,
                  wbd_ref, wbn_ref, wout_ref, exp_ref, gate_ref, mod_ref,
                  x1_ref, h2_ref, o4n, o16n, o16j, l4n, l16n, l16j, oa_scr):
    tm = x_ref.shape[0]
    for src4, src16, n4, j16, n16 in ((o4_ref, o16_ref, o4n, o16j, o16n), (l4_ref, l16_ref, l4n, l16j, l16n)):
        for t in range(n4.shape[0]):
            ls = slice(t * LANES, (t + 1) * LANES)
            for c in range(4):
                n4[t, pl.ds(c, tm // 4, stride=4), :] = src4[c, :, ls]
                for c2 in range(4):
                    j16[t, c, pl.ds(c2, tm // 16, stride=4), :] = src16[c + 4 * c2, :, ls]
                n16[t, pl.ds(c, tm // 4, stride=4), :] = j16[t, c]

    for sb in range(tm // SUB_MERGE):
        rows = slice(sb * SUB_MERGE, (sb + 1) * SUB_MERGE)
        l1, l4, l16 = l1_ref[rows, :], l4n[0, rows, :], l16n[0, rows, :]
        mx = jnp.maximum(jnp.maximum(l1, l4), l16)
        e1, e4, e16 = jnp.exp(l1 - mx), jnp.exp(l4 - mx), jnp.exp(l16 - mx)
        den = e1 + e4 + e16
        w1, w4, w16 = (_expand_heads(e / den, exp_ref) for e in (e1, e4, e16))
        for t in range(WIDTH // LANES):
            ls = slice(t * LANES, (t + 1) * LANES)
            oa = w1[:, ls] * o1_ref[rows, ls] + w4[:, ls] * o4n[t, rows, :] + w16[:, ls] * o16n[t, rows, :]
            oa_scr[sb, :, ls] = oa.astype(BF16)
        ya = jnp.dot(oa_scr[sb], wbd_ref[...], preferred_element_type=F32)
        yb = jnp.dot(ob_ref[rows, :], wbn_ref[...], preferred_element_type=F32)
        merged = (g_ref[rows, 0:D_MODEL].astype(F32) * ya + g_ref[rows, D_MODEL:2 * D_MODEL].astype(F32) * yb)
        z = jnp.dot(merged.astype(BF16), wout_ref[...], preferred_element_type=F32)
        x1 = x_ref[rows, :] + gate_ref[...] * z
        x1_ref[rows, :] = x1
        h2_ref[rows, :] = _rms_modulate(x1, mod_ref).astype(BF16)


def _merge(x, o1, o4, o16, l1, l4, l16, ob, g, wbd, wbn, wout, gate1, mod2):
    tm = TM_MERGE
    nt = SEQ // tm
    row = lambda w: pl.BlockSpec((tm, w), lambda i: (i, 0))
    cls = lambda d, w: pl.BlockSpec((d, tm // d, w), lambda i: (0, i, 0))
    const = lambda shape: pl.BlockSpec(shape, lambda i: (0, 0))
    spread = (jnp.arange(LANES)[:, None] == jnp.arange(WIDTH)[None, :] // HEAD_DIM).astype(BF16)
    return pl.pallas_call(
        _merge_kernel,
        grid=(nt,),
        in_specs=[row(D_MODEL), row(WIDTH), cls(4, WIDTH), cls(16, WIDTH),
                  row(LANES), cls(4, LANES), cls(16, LANES), row(WIDTH), row(2 * D_MODEL),
                  const((WIDTH, D_MODEL)), const((WIDTH, D_MODEL)), const((D_MODEL, D_MODEL)),
                  const((LANES, WIDTH)), const((1, D_MODEL)), const((3, D_MODEL))],
        out_specs=[row(D_MODEL), row(D_MODEL)],
        out_shape=[jax.ShapeDtypeStruct((SEQ, D_MODEL), F32), jax.ShapeDtypeStruct((SEQ, D_MODEL), BF16)],
        scratch_shapes=[pltpu.VMEM((WIDTH // LANES, tm, LANES), F32), pltpu.VMEM((WIDTH // LANES, tm, LANES), F32),
                        pltpu.VMEM((WIDTH // LANES, 4, tm // 4, LANES), F32),
                        pltpu.VMEM((1, tm, LANES), F32), pltpu.VMEM((1, tm, LANES), F32),
                        pltpu.VMEM((1, 4, tm // 4, LANES), F32),
                        pltpu.VMEM((tm // SUB_MERGE, SUB_MERGE, WIDTH), BF16)],
        compiler_params=_params(("arbitrary",)),
        name="merge",
    )(x, o1, o4.reshape(4, SEQ // 4, WIDTH), o16.reshape(16, SEQ // 16, WIDTH),
      l1, l4.reshape(4, SEQ // 4, LANES), l16.reshape(16, SEQ // 16, LANES), ob, g,
      wbd, wbn, wout, spread, gate1, mod2)


def _ffn_kernel(x1_ref, hm_ref, hp_ref, hn_ref, wup_ref, cw_ref, cb_ref, wd_ref, gate_ref, gfin_ref,
                y_ref, hext, act):
    tm = x1_ref.shape[0]
    n = tm + 2 * HALO
    i = pl.program_id(0)
    hext[0:HALO, :] = jnp.where(i == 0, jnp.zeros_like(hp_ref), hp_ref[...])
    hext[HALO:HALO + tm, :] = hm_ref[...]
    hext[HALO + tm:n, :] = jnp.where(i == pl.num_programs(0) - 1, jnp.zeros_like(hn_ref), hn_ref[...])

    def conv(cols):
        p = jnp.dot(hext[...], wup_ref[:, cols], preferred_element_type=F32)
        prev = pltpu.roll(p, 1, 0)[HALO:HALO + tm]
        nxt = pltpu.roll(p, n - 1, 0)[HALO:HALO + tm]
        return (cw_ref[0:1, cols] * prev + cw_ref[1:2, cols] * p[HALO:HALO + tm] + cw_ref[2:3, cols] * nxt
                + cb_ref[:, cols])

    for c in range(D_FF // TF_FFN):
        val = slice(c * TF_FFN, (c + 1) * TF_FFN)
        gate = slice(D_FF + c * TF_FFN, D_FF + (c + 1) * TF_FFN)
        act[:, val] = (jax.nn.gelu(conv(gate), approximate=True) * conv(val)).astype(BF16)

    x2 = x1_ref[...] + gate_ref[...] * jnp.dot(act[...], wd_ref[...], preferred_element_type=F32)
    y = x2 * lax.rsqrt(jnp.mean(x2 * x2, axis=-1, keepdims=True) + RMS_EPS)
    y_ref[...] = y * gfin_ref[...]


def _ffn(x1, h2, w_up, conv_w, conv_b, w_down, gate2, g_final):
    tm = TM_FFN
    nt = SEQ // tm
    per = tm // HALO
    nhalo = SEQ // HALO
    row = lambda: pl.BlockSpec((tm, D_MODEL), lambda i: (i, 0))
    const = lambda a: pl.BlockSpec(a.shape, lambda i: (0, 0), pipeline_mode=pl.Buffered(1))
    return pl.pallas_call(
        _ffn_kernel,
        grid=(nt,),
        in_specs=[
            row(), row(),
            pl.BlockSpec((HALO, D_MODEL), lambda i: (jnp.maximum(i * per - 1, 0), 0)),
            pl.BlockSpec((HALO, D_MODEL), lambda i: (jnp.minimum((i + 1) * per, nhalo - 1), 0)),
            const(w_up), const(conv_w), const(conv_b), const(w_down), const(gate2), const(g_final),
        ],
        out_specs=row(),
        out_shape=jax.ShapeDtypeStruct((SEQ, D_MODEL), F32),
        scratch_shapes=[pltpu.VMEM((tm + 2 * HALO, D_MODEL), BF16), pltpu.VMEM((tm, D_FF), BF16)],
        compiler_params=_params(("arbitrary",)),
        name="ffn",
    )(x1, h2, h2, h2, w_up, conv_w, conv_b, w_down, gate2, g_final)


def kernel(x, c, w_ada, b_ada, g_mix, w_in, rel_bias, na_rpb, w_branch_dil, w_branch_na, w_out, g_ffn,
           w_up, conv_w, conv_b, w_down, g_final):
    assert x.shape == (1, SEQ, D_MODEL) and w_ada.shape[0] == 1
    xs = x[0]
    ada = _ada(c, w_ada[0], b_ada[0])
    shift1, scale1, gate1, shift2, scale2, gate2 = [ada[:, k * D_MODEL:(k + 1) * D_MODEL] for k in range(6)]
    mod1 = jnp.concatenate([g_mix[0][None], scale1, shift1], axis=0)
    mod2 = jnp.concatenate([g_ffn[0][None], scale2, shift2], axis=0)

    q_scale = np.ones((IN_COLS,), np.float32)
    q_scale[0:WIDTH] = HEAD_DIM ** -0.5
    q_scale[3 * WIDTH:4 * WIDTH] = HEAD_DIM ** -0.5
    w_in_b = (w_in[0] * q_scale).astype(BF16)

    a_nat, a_m4, a_m16, b_qkv, gates = _inproj(xs, mod1, w_in_b)

    outs = []
    for d, arr in zip(DILATIONS, (a_nat, a_m4, a_m16)):
        outs.append(_band_attention(arr.reshape(SEQ, 3 * WIDTH), _band_bias(rel_bias, d), SEQ // d))
    (o1, l1), (o4, l4), (o16, l16) = outs
    ob = _na_attention(b_qkv, _na_bias(na_rpb[0]))

    x1, h2 = _merge(xs, o1, o4, o16, l1, l4, l16, ob, gates,
                    w_branch_dil[0].astype(BF16), w_branch_na[0].astype(BF16), w_out[0].astype(BF16),
                    gate1, mod2)
    y = _ffn(x1, h2, w_up[0].astype(BF16), conv_w[0], conv_b[0].reshape(1, 2 * D_FF),
             w_down[0].astype(BF16), gate2, g_final.reshape(1, D_MODEL))
    return y[None]
```

```python
import functools
import math

import numpy as np
import jax
import jax.numpy as jnp
from jax import lax
from jax.experimental import pallas as pl
from jax.experimental.pallas import tpu as pltpu

F32 = jnp.float32
BF16 = jnp.bfloat16

D_MODEL = 1024
SEQ = 16384
GRID_W = 64
ROWS = SEQ // GRID_W
HEAD_DIM = 64
N_HEADS = 8
WIDTH = N_HEADS * HEAD_DIM
DILATIONS = (1, 4, 16)
BAND = 64
NA_KH = 8
NA_COLS = 16
NA_ROWS_MAX = 8
T5_BUCKETS = 32
T5_MAX_DISTANCE = 1024
D_FF = 2816
RMS_EPS = 1e-6
IN_COLS = 6 * WIDTH + 2 * D_MODEL
NEG = -1e30

VMEM_LIMIT = 56 * 1024 * 1024

TM_IN = 512
TQ_BAND = 512
SUB_BAND = 128
TM_MERGE = 512
SUB_MERGE = 256
TM_FFN = 512
TF_FFN = 256
HALO = 16
LANES = 128
NA_AHEAD = 3
BAND_AHEAD = 3


def _params(sem):
    return pltpu.CompilerParams(dimension_semantics=sem, vmem_limit_bytes=VMEM_LIMIT)


def _ada_kernel(c_ref, w_ref, b_ref, o_ref):
    c = c_ref[...]
    s = c / (1.0 + jnp.exp(-c))
    o_ref[...] = jnp.sum(w_ref[...] * s, axis=0, keepdims=True) + b_ref[...]


def _ada(c, w_ada, b_ada):
    n_out = w_ada.shape[1]
    tn = D_MODEL
    return pl.pallas_call(
        _ada_kernel,
        grid=(n_out // tn,),
        in_specs=[
            pl.BlockSpec((D_MODEL, 1), lambda j: (0, 0)),
            pl.BlockSpec((D_MODEL, tn), lambda j: (0, j)),
            pl.BlockSpec((1, tn), lambda j: (0, j)),
        ],
        out_specs=pl.BlockSpec((1, tn), lambda j: (0, j)),
        out_shape=jax.ShapeDtypeStruct((1, n_out), F32),
        compiler_params=_params(("arbitrary",)),
        name="ada",
    )(c.reshape(D_MODEL, 1), w_ada, b_ada.reshape(1, n_out))


def _rms_modulate(x, mod_ref):
    y = x * lax.rsqrt(jnp.mean(x * x, axis=-1, keepdims=True) + RMS_EPS)
    return (y * mod_ref[0:1, :]) * (1.0 + mod_ref[1:2, :]) + mod_ref[2:3, :]


def _inproj_kernel(x0_ref, xn_ref, mod_ref, w_ref, anat_ref, am4_ref, am16_ref, b_ref, g_ref,
                   h_even, h_odd, res_scr, res4_scr):
    tm = xn_ref.shape[0]
    i = pl.program_id(0)

    @pl.when(i == 0)
    def _():
        h_even[...] = _rms_modulate(x0_ref[...], mod_ref).astype(BF16)

    def project(h_cur, h_next):
        for k, n in enumerate((6, 7, 8, 9, 0, 1, 2, 3, 4, 5)):
            cols = slice(n * WIDTH, (n + 1) * WIDTH)
            r = jnp.dot(h_cur[...], w_ref[:, cols], preferred_element_type=F32)
            if k == 2:
                h_next[...] = _rms_modulate(xn_ref[...], mod_ref).astype(BF16)
            if n < 3:
                anat_ref[:, cols] = r.astype(BF16)
                for t in range(WIDTH // LANES):
                    lanes = slice(n * WIDTH + t * LANES, n * WIDTH + (t + 1) * LANES)
                    res_scr[t] = r[:, t * LANES:(t + 1) * LANES]
                    for c in range(4):
                        cls4 = res_scr[t, pl.ds(c, tm // 4, stride=4), :]
                        am4_ref[c, :, lanes] = cls4.astype(BF16)
                        res4_scr[t, c] = cls4
                        for c2 in range(4):
                            am16_ref[c + 4 * c2, :, lanes] = (
                                res4_scr[t, c, pl.ds(c2, tm // 16, stride=4), :].astype(BF16))
            elif n < 6:
                b_ref[:, (n - 3) * WIDTH:(n - 2) * WIDTH] = r.astype(BF16)
            else:
                g_ref[:, (n - 6) * WIDTH:(n - 5) * WIDTH] = (1.0 / (1.0 + jnp.exp(-r))).astype(BF16)

    @pl.when(i % 2 == 0)
    def _():
        project(h_even, h_odd)

    @pl.when(i % 2 == 1)
    def _():
        project(h_odd, h_even)


def _inproj(x, mod1, w_in):
    tm = TM_IN
    nt = SEQ // tm
    qkv = 3 * WIDTH
    return pl.pallas_call(
        _inproj_kernel,
        grid=(nt,),
        in_specs=[
            pl.BlockSpec((tm, D_MODEL), lambda i: (0, 0)),
            pl.BlockSpec((tm, D_MODEL), lambda i: (jnp.minimum(i + 1, nt - 1), 0)),
            pl.BlockSpec((3, D_MODEL), lambda i: (0, 0)),
            pl.BlockSpec((D_MODEL, IN_COLS), lambda i: (0, 0), pipeline_mode=pl.Buffered(1)),
        ],
        out_specs=[
            pl.BlockSpec((tm, qkv), lambda i: (i, 0)),
            pl.BlockSpec((4, tm // 4, qkv), lambda i: (0, i, 0)),
            pl.BlockSpec((16, tm // 16, qkv), lambda i: (0, i, 0)),
            pl.BlockSpec((tm, qkv), lambda i: (i, 0)),
            pl.BlockSpec((tm, 2 * D_MODEL), lambda i: (i, 0)),
        ],
        out_shape=[
            jax.ShapeDtypeStruct((SEQ, qkv), BF16),
            jax.ShapeDtypeStruct((4, SEQ // 4, qkv), BF16),
            jax.ShapeDtypeStruct((16, SEQ // 16, qkv), BF16),
            jax.ShapeDtypeStruct((SEQ, qkv), BF16),
            jax.ShapeDtypeStruct((SEQ, 2 * D_MODEL), BF16),
        ],
        scratch_shapes=[pltpu.VMEM((tm, D_MODEL), BF16), pltpu.VMEM((tm, D_MODEL), BF16),
                        pltpu.VMEM((WIDTH // LANES, tm, LANES), F32),
                        pltpu.VMEM((WIDTH // LANES, 4, tm // 4, LANES), F32)],
        compiler_params=_params(("arbitrary",)),
        name="inproj",
    )(x, x, mod1, w_in)


def _stack_head_pair(q):
    first = lax.broadcasted_iota(jnp.int32, q.shape, 1) < HEAD_DIM
    qf = q.astype(F32)
    return jnp.concatenate([jnp.where(first, qf, 0.0), jnp.where(first, 0.0, qf)], axis=0).astype(q.dtype)


def _unstack_head_pair(o):
    n = o.shape[0] // 2
    first = lax.broadcasted_iota(jnp.int32, (n, o.shape[1]), 1) < HEAD_DIM
    return jnp.where(first, o[0:n], o[n:2 * n])


def _band_kernel(q_ref, kl_ref, km_ref, kr_ref, vl_ref, vm_ref, vr_ref, bias_ref,
                 o_ref, lse_ref, kcat, vcat, *, seg_len):
    tq = q_ref.shape[0]
    sub = SUB_BAND
    nsub = tq // sub
    nk = sub + 2 * BAND
    start = pl.program_id(0) * tq
    for cat, (l_ref, m_ref, r_ref) in ((kcat, (kl_ref, km_ref, kr_ref)), (vcat, (vl_ref, vm_ref, vr_ref))):
        cat[0:BAND, :] = l_ref[...]
        cat[BAND:BAND + tq, :] = m_ref[...]
        cat[BAND + tq:tq + 2 * BAND, :] = r_ref[...]
    col = lax.broadcasted_iota(jnp.int32, (1, nk), 1)
    edge = {0: jnp.where((col < BAND) & ((start % seg_len) == 0), NEG, 0.0),
            nsub - 1: jnp.where((col >= BAND + sub) & (((start + tq) % seg_len) == 0), NEG, 0.0)}
    lse_ref[...] = jnp.zeros_like(lse_ref)

    def scores(j, pr):
        ps = slice(pr * LANES, (pr + 1) * LANES)
        s = lax.dot_general(_stack_head_pair(q_ref[j * sub:(j + 1) * sub, ps]), kcat[j * sub:j * sub + nk, ps],
                            (((1,), (1,)), ((), ())), preferred_element_type=F32)
        s = s + bias_ref[pr]
        return s + edge[j] if j in edge else s

    def finish(j, pr, s):
        ps = slice(pr * LANES, (pr + 1) * LANES)
        rows = slice(j * sub, (j + 1) * sub)
        m = jnp.max(s, axis=-1, keepdims=True)
        p = jnp.exp(s - m)
        l = jnp.sum(p, axis=-1, keepdims=True)
        o = jnp.dot(p.astype(BF16), vcat[j * sub:j * sub + nk, ps], preferred_element_type=F32) / l
        o_ref[rows, ps] = _unstack_head_pair(o).astype(o_ref.dtype)
        lse = m + jnp.log(l)
        lse_ref[rows, 2 * pr:2 * pr + 1] = lse[0:sub]
        lse_ref[rows, 2 * pr + 1:2 * pr + 2] = lse[sub:2 * sub]

    pending = []
    for j in range(nsub):
        for pr in range(N_HEADS // 2):
            pending.append((j, pr, scores(j, pr)))
            if len(pending) > BAND_AHEAD:
                finish(*pending.pop(0))
    for unit in pending:
        finish(*unit)


def _band_attention(qkv, bias, seg_len):
    tq = TQ_BAND
    nt = SEQ // tq
    per = tq // BAND
    nhalo = SEQ // BAND
    left = lambda c: (lambda i: (jnp.maximum(i * per - 1, 0), c))
    right = lambda c: (lambda i: (jnp.minimum((i + 1) * per, nhalo - 1), c))
    main = lambda c: (lambda i: (i, c))
    kv_specs = []
    for c in (1, 2):
        kv_specs += [pl.BlockSpec((BAND, WIDTH), left(c)), pl.BlockSpec((tq, WIDTH), main(c)),
                     pl.BlockSpec((BAND, WIDTH), right(c))]
    nk = tq + 2 * BAND
    return pl.pallas_call(
        functools.partial(_band_kernel, seg_len=seg_len),
        grid=(nt,),
        in_specs=[pl.BlockSpec((tq, WIDTH), main(0))] + kv_specs
        + [pl.BlockSpec((N_HEADS // 2, 2 * SUB_BAND, SUB_BAND + 2 * BAND), lambda i: (0, 0, 0))],
        out_specs=[pl.BlockSpec((tq, WIDTH), lambda i: (i, 0)),
                   pl.BlockSpec((tq, LANES), lambda i: (i, 0))],
        out_shape=[jax.ShapeDtypeStruct((SEQ, WIDTH), BF16),
                   jax.ShapeDtypeStruct((SEQ, LANES), F32)],
        scratch_shapes=[pltpu.VMEM((nk, WIDTH), BF16), pltpu.VMEM((nk, WIDTH), BF16)],
        compiler_params=_params(("arbitrary",)),
        name=f"band_l{seg_len}",
    )(qkv, qkv, qkv, qkv, qkv, qkv, qkv, bias.reshape(N_HEADS // 2, 2 * SUB_BAND, SUB_BAND + 2 * BAND))


def _t5_bucket(rel):
    n = T5_BUCKETS // 2
    max_exact = n // 2
    sign_part = jnp.where(rel > 0, n, 0)
    a = jnp.abs(rel)
    af = jnp.maximum(a, 1).astype(F32)
    large = max_exact + (jnp.log(af / max_exact) / math.log(T5_MAX_DISTANCE / max_exact)
                         * (n - max_exact)).astype(jnp.int32)
    large = jnp.minimum(large, n - 1)
    return sign_part + jnp.where(a < max_exact, a, large)


def _band_bias(rel_bias, dilation):
    tq = SUB_BAND
    delta = jnp.arange(tq + 2 * BAND)[None, :] - BAND - jnp.arange(tq)[:, None]
    b = _lookup(rel_bias.astype(F32).T, _t5_bucket(delta * dilation), T5_BUCKETS)
    return jnp.where((jnp.abs(delta) <= BAND)[None], b, NEG)


def _lookup(table, idx, n):
    onehot = (idx[..., None] == jnp.arange(n)).astype(F32)
    return jnp.einsum("...b,xyb->...xy", table, onehot, precision=lax.Precision.HIGHEST)


def _na_kernel(q_ref, kp_ref, kc_ref, kn_ref, vp_ref, vc_ref, vn_ref, btab_ref, o_ref, kcat, vcat, bias_ref):
    tile = q_ref.shape[0]
    i = pl.program_id(0)

    @pl.when(i == 0)
    def _():
        for v in range(NA_KH):
            for h in range(N_HEADS):
                for k in range(NA_KH // 2):
                    bias_ref[v, h // 2, (h % 2) * GRID_W:(h % 2 + 1) * GRID_W, k * LANES:(k + 1) * LANES] = (
                        btab_ref[h, v + 2 * k])
    for cat, refs in ((kcat, (kp_ref, kc_ref, kn_ref)), (vcat, (vp_ref, vc_ref, vn_ref))):
        for n, ref in enumerate(refs):
            cat[n * tile:(n + 1) * tile, :] = ref[...]
    nkeys = NA_KH * GRID_W

    def window(t):
        r = i * NA_KH + t
        rs = jnp.clip(r - NA_KH // 2, 0, ROWS - NA_KH)
        return pl.multiple_of((rs - (i - 1) * NA_KH) * GRID_W, GRID_W), rs - r + (NA_KH - 1)

    def scores(t, pr, off, var):
        ps = slice(pr * LANES, (pr + 1) * LANES)
        s = lax.dot_general(_stack_head_pair(q_ref[t * GRID_W:(t + 1) * GRID_W, ps]), kcat[pl.ds(off, nkeys), ps],
                            (((1,), (1,)), ((), ())), preferred_element_type=F32)
        return s + bias_ref[var, pr]

    def finish(t, pr, off, s):
        ps = slice(pr * LANES, (pr + 1) * LANES)
        m = jnp.max(s, axis=-1, keepdims=True)
        p = jnp.exp(s - m)
        l = jnp.sum(p, axis=-1, keepdims=True)
        o = jnp.dot(p.astype(BF16), vcat[pl.ds(off, nkeys), ps], preferred_element_type=F32) / l
        o_ref[t * GRID_W:(t + 1) * GRID_W, ps] = _unstack_head_pair(o).astype(o_ref.dtype)

    units = [(t, pr) for t in range(NA_KH) for pr in range(N_HEADS // 2)]
    wins = [window(t) for t in range(NA_KH)]
    pending = []
    for t, pr in units:
        off, var = wins[t]
        pending.append((t, pr, off, scores(t, pr, off, var)))
        if len(pending) > NA_AHEAD:
            finish(*pending.pop(0))
    for unit in pending:
        finish(*unit)


def _na_attention(qkv, btab):
    tile = NA_KH * GRID_W
    nt = SEQ // tile
    prev = lambda c: (lambda i: (jnp.maximum(i - 1, 0), c))
    cur = lambda c: (lambda i: (i, c))
    nxt = lambda c: (lambda i: (jnp.minimum(i + 1, nt - 1), c))
    kv_specs = [pl.BlockSpec((tile, WIDTH), f(c)) for c in (1, 2) for f in (prev, cur, nxt)]
    return pl.pallas_call(
        _na_kernel,
        grid=(nt,),
        in_specs=[pl.BlockSpec((tile, WIDTH), cur(0))] + kv_specs
        + [pl.BlockSpec(btab.shape, lambda i: (0, 0, 0, 0), pipeline_mode=pl.Buffered(1))],
        out_specs=pl.BlockSpec((tile, WIDTH), lambda i: (i, 0)),
        out_shape=jax.ShapeDtypeStruct((SEQ, WIDTH), BF16),
        scratch_shapes=[pltpu.VMEM((3 * tile, WIDTH), BF16), pltpu.VMEM((3 * tile, WIDTH), BF16),
                        pltpu.VMEM((NA_KH, N_HEADS // 2, 2 * GRID_W, NA_KH * GRID_W), F32)],
        compiler_params=_params(("arbitrary",)),
        name="na",
    )(qkv, qkv, qkv, qkv, qkv, qkv, qkv, btab)


def _na_bias(rpb):
    cq = jnp.arange(GRID_W)
    col_start = jnp.clip(cq - NA_COLS // 2, 0, GRID_W - NA_COLS)
    col_mask = (cq[None, :] >= col_start[:, None]) & (cq[None, :] < col_start[:, None] + NA_COLS)
    dc = jnp.clip(cq[None, :] - cq[:, None], -(NA_COLS - 1), NA_COLS - 1) + NA_COLS - 1
    e = _lookup(rpb.astype(F32), dc, 2 * NA_COLS - 1)
    e = jnp.where(col_mask[None, None], e, NEG)
    return jnp.concatenate([e[:, :-1], e[:, 1:]], axis=-1)


def _expand_heads(w, exp_ref):
    hi = w.astype(BF16)
    lo = (w - hi.astype(F32)).astype(BF16)
    return (jnp.dot(hi, exp_ref[...], preferred_element_type=F32)
            + jnp.dot(lo, exp_ref[...], preferred_element_type=F32))


def _merge_kernel(x_ref, o1_ref, o4_ref, o16_ref, l1_ref, l4_ref, l16_ref, ob_ref, g_ref,
                  wbd_ref, wbn_ref, wout_ref, exp_ref, gate_ref, mod_ref,
                  x1_ref, h2_ref, o4n, o16n, o16j, l4n, l16n, l16j, oa_scr):
    tm = x_ref.shape[0]
    for src4, src16, n4, j16, n16 in ((o4_ref, o16_ref, o4n, o16j, o16n), (l4_ref, l16_ref, l4n, l16j, l16n)):
        for t in range(n4.shape[0]):
            ls = slice(t * LANES, (t + 1) * LANES)
            for c in range(4):
                n4[t, pl.ds(c, tm // 4, stride=4), :] = src4[c, :, ls].astype(F32)
                for c2 in range(4):
                    j16[t, c, pl.ds(c2, tm // 16, stride=4), :] = src16[c + 4 * c2, :, ls].astype(F32)
                n16[t, pl.ds(c, tm // 4, stride=4), :] = j16[t, c]

    for sb in range(tm // SUB_MERGE):
        rows = slice(sb * SUB_MERGE, (sb + 1) * SUB_MERGE)
        l1, l4, l16 = l1_ref[rows, :], l4n[0, rows, :], l16n[0, rows, :]
        mx = jnp.maximum(jnp.maximum(l1, l4), l16)
        e1, e4, e16 = jnp.exp(l1 - mx), jnp.exp(l4 - mx), jnp.exp(l16 - mx)
        den = e1 + e4 + e16
        w1, w4, w16 = (_expand_heads(e / den, exp_ref) for e in (e1, e4, e16))
        for t in range(WIDTH // LANES):
            ls = slice(t * LANES, (t + 1) * LANES)
            oa = w1[:, ls] * o1_ref[rows, ls] + w4[:, ls] * o4n[t, rows, :] + w16[:, ls] * o16n[t, rows, :]
            oa_scr[sb, :, ls] = oa.astype(BF16)
        ya = jnp.dot(oa_scr[sb], wbd_ref[...], preferred_element_type=F32)
        yb = jnp.dot(ob_ref[rows, :], wbn_ref[...], preferred_element_type=F32)
        merged = (g_ref[rows, 0:D_MODEL].astype(F32) * ya + g_ref[rows, D_MODEL:2 * D_MODEL].astype(F32) * yb)
        z = jnp.dot(merged.astype(BF16), wout_ref[...], preferred_element_type=F32)
        x1 = x_ref[rows, :] + gate_ref[...] * z
        x1_ref[rows, :] = x1
        h2_ref[rows, :] = _rms_modulate(x1, mod_ref).astype(BF16)


def _merge(x, o1, o4, o16, l1, l4, l16, ob, g, wbd, wbn, wout, gate1, mod2):
    tm = TM_MERGE
    nt = SEQ // tm
    row = lambda w: pl.BlockSpec((tm, w), lambda i: (i, 0))
    cls = lambda d, w: pl.BlockSpec((d, tm // d, w), lambda i: (0, i, 0))
    const = lambda shape: pl.BlockSpec(shape, lambda i: (0, 0))
    spread = (jnp.arange(LANES)[:, None] == jnp.arange(WIDTH)[None, :] // HEAD_DIM).astype(BF16)
    return pl.pallas_call(
        _merge_kernel,
        grid=(nt,),
        in_specs=[row(D_MODEL), row(WIDTH), cls(4, WIDTH), cls(16, WIDTH),
                  row(LANES), cls(4, LANES), cls(16, LANES), row(WIDTH), row(2 * D_MODEL),
                  const((WIDTH, D_MODEL)), const((WIDTH, D_MODEL)), const((D_MODEL, D_MODEL)),
                  const((LANES, WIDTH)), const((1, D_MODEL)), const((3, D_MODEL))],
        out_specs=[row(D_MODEL), row(D_MODEL)],
        out_shape=[jax.ShapeDtypeStruct((SEQ, D_MODEL), F32), jax.ShapeDtypeStruct((SEQ, D_MODEL), BF16)],
        scratch_shapes=[pltpu.VMEM((WIDTH // LANES, tm, LANES), F32), pltpu.VMEM((WIDTH // LANES, tm, LANES), F32),
                        pltpu.VMEM((WIDTH // LANES, 4, tm // 4, LANES), F32),
                        pltpu.VMEM((1, tm, LANES), F32), pltpu.VMEM((1, tm, LANES), F32),
                        pltpu.VMEM((1, 4, tm // 4, LANES), F32),
                        pltpu.VMEM((tm // SUB_MERGE, SUB_MERGE, WIDTH), BF16)],
        compiler_params=_params(("arbitrary",)),
        name="merge",
    )(x, o1, o4.reshape(4, SEQ // 4, WIDTH), o16.reshape(16, SEQ // 16, WIDTH),
      l1, l4.reshape(4, SEQ // 4, LANES), l16.reshape(16, SEQ // 16, LANES), ob, g,
      wbd, wbn, wout, spread, gate1, mod2)


def _ffn_kernel(x1_ref, hm_ref, hp_ref, hn_ref, wup_ref, cw_ref, cb_ref, wd_ref, gate_ref, gfin_ref,
                y_ref, hext, act):
    tm = x1_ref.shape[0]
    n = tm + 2 * HALO
    i = pl.program_id(0)
    hext[0:HALO, :] = jnp.where(i == 0, jnp.zeros_like(hp_ref), hp_ref[...])
    hext[HALO:HALO + tm, :] = hm_ref[...]
    hext[HALO + tm:n, :] = jnp.where(i == pl.num_programs(0) - 1, jnp.zeros_like(hn_ref), hn_ref[...])

    def conv(cols):
        p = jnp.dot(hext[...], wup_ref[:, cols], preferred_element_type=F32)
        prev = pltpu.roll(p, 1, 0)[HALO:HALO + tm]
        nxt = pltpu.roll(p, n - 1, 0)[HALO:HALO + tm]
        return (cw_ref[0:1, cols] * prev + cw_ref[1:2, cols] * p[HALO:HALO + tm] + cw_ref[2:3, cols] * nxt
                + cb_ref[:, cols])

    for c in range(D_FF // TF_FFN):
        val = slice(c * TF_FFN, (c + 1) * TF_FFN)
        gate = slice(D_FF + c * TF_FFN, D_FF + (c + 1) * TF_FFN)
        act[:, val] = (jax.nn.gelu(conv(gate), approximate=True) * conv(val)).astype(BF16)

    x2 = x1_ref[...] + gate_ref[...] * jnp.dot(act[...], wd_ref[...], preferred_element_type=F32)
    y = x2 * lax.rsqrt(jnp.mean(x2 * x2, axis=-1, keepdims=True) + RMS_EPS)
    y_ref[...] = y * gfin_ref[...]


def _ffn(x1, h2, w_up, conv_w, conv_b, w_down, gate2, g_final):
    tm = TM_FFN
    nt = SEQ // tm
    per = tm // HALO
    nhalo = SEQ // HALO
    row = lambda: pl.BlockSpec((tm, D_MODEL), lambda i: (i, 0))
    const = lambda a: pl.BlockSpec(a.shape, lambda i: (0, 0), pipeline_mode=pl.Buffered(1))
    return pl.pallas_call(
        _ffn_kernel,
        grid=(nt,),
        in_specs=[
            row(), row(),
            pl.BlockSpec((HALO, D_MODEL), lambda i: (jnp.maximum(i * per - 1, 0), 0)),
            pl.BlockSpec((HALO, D_MODEL), lambda i: (jnp.minimum((i + 1) * per, nhalo - 1), 0)),
            const(w_up), const(conv_w), const(conv_b), const(w_down), const(gate2), const(g_final),
        ],
        out_specs=row(),
        out_shape=jax.ShapeDtypeStruct((SEQ, D_MODEL), F32),
        scratch_shapes=[pltpu.VMEM((tm + 2 * HALO, D_MODEL), BF16), pltpu.VMEM((tm, D_FF), BF16)],
        compiler_params=_params(("arbitrary",)),
        name="ffn",
    )(x1, h2, h2, h2, w_up, conv_w, conv_b, w_down, gate2, g_final)


def kernel(x, c, w_ada, b_ada, g_mix, w_in, rel_bias, na_rpb, w_branch_dil, w_branch_na, w_out, g_ffn,
           w_up, conv_w, conv_b, w_down, g_final):
    assert x.shape == (1, SEQ, D_MODEL) and w_ada.shape[0] == 1
    xs = x[0]
    ada = _ada(c, w_ada[0], b_ada[0])
    shift1, scale1, gate1, shift2, scale2, gate2 = [ada[:, k * D_MODEL:(k + 1) * D_MODEL] for k in range(6)]
    mod1 = jnp.concatenate([g_mix[0][None], scale1, shift1], axis=0)
    mod2 = jnp.concatenate([g_ffn[0][None], scale2, shift2], axis=0)

    q_scale = np.ones((IN_COLS,), np.float32)
    q_scale[0:WIDTH] = HEAD_DIM ** -0.5
    q_scale[3 * WIDTH:4 * WIDTH] = HEAD_DIM ** -0.5
    w_in_b = (w_in[0] * q_scale).astype(BF16)

    a_nat, a_m4, a_m16, b_qkv, gates = _inproj(xs, mod1, w_in_b)

    outs = []
    for d, arr in zip(DILATIONS, (a_nat, a_m4, a_m16)):
        outs.append(_band_attention(arr.reshape(SEQ, 3 * WIDTH), _band_bias(rel_bias, d), SEQ // d))
    (o1, l1), (o4, l4), (o16, l16) = outs
    ob = _na_attention(b_qkv, _na_bias(na_rpb[0]))

    x1, h2 = _merge(xs, o1, o4, o16, l1, l4, l16, ob, gates,
                    w_branch_dil[0].astype(BF16), w_branch_na[0].astype(BF16), w_out[0].astype(BF16),
                    gate1, mod2)
    y = _ffn(x1, h2, w_up[0].astype(BF16), conv_w[0], conv_b[0].reshape(1, 2 * D_FF),
             w_down[0].astype(BF16), gate2, g_final.reshape(1, D_MODEL))
    return y[None]
```

```python
import functools
import math

import numpy as np
import jax
import jax.numpy as jnp
from jax import lax
from jax.experimental import pallas as pl
from jax.experimental.pallas import tpu as pltpu

F32 = jnp.float32
BF16 = jnp.bfloat16

D_MODEL = 1024
SEQ = 16384
GRID_W = 64
ROWS = SEQ // GRID_W
HEAD_DIM = 64
N_HEADS = 8
WIDTH = N_HEADS * HEAD_DIM
DILATIONS = (1, 4, 16)
BAND = 64
NA_KH = 8
NA_COLS = 16
NA_ROWS_MAX = 8
T5_BUCKETS = 32
T5_MAX_DISTANCE = 1024
D_FF = 2816
RMS_EPS = 1e-6
IN_COLS = 6 * WIDTH + 2 * D_MODEL
NEG = -1e30

VMEM_LIMIT = 56 * 1024 * 1024

TM_IN = 512
TQ_BAND = 512
SUB_BAND = 128
TM_MERGE = 512
SUB_MERGE = 256
TM_FFN = 1024
TF_FFN = 256
HALO = 16
LANES = 128
NA_AHEAD = 3
BAND_AHEAD = 3


def _params(sem):
    return pltpu.CompilerParams(dimension_semantics=sem, vmem_limit_bytes=VMEM_LIMIT)


def _ada_kernel(c_ref, w_ref, b_ref, o_ref):
    c = c_ref[...]
    s = c / (1.0 + jnp.exp(-c))
    o_ref[...] = jnp.sum(w_ref[...] * s, axis=0, keepdims=True) + b_ref[...]


def _ada(c, w_ada, b_ada):
    n_out = w_ada.shape[1]
    tn = D_MODEL
    return pl.pallas_call(
        _ada_kernel,
        grid=(n_out // tn,),
        in_specs=[
            pl.BlockSpec((D_MODEL, 1), lambda j: (0, 0)),
            pl.BlockSpec((D_MODEL, tn), lambda j: (0, j)),
            pl.BlockSpec((1, tn), lambda j: (0, j)),
        ],
        out_specs=pl.BlockSpec((1, tn), lambda j: (0, j)),
        out_shape=jax.ShapeDtypeStruct((1, n_out), F32),
        compiler_params=_params(("arbitrary",)),
        name="ada",
    )(c.reshape(D_MODEL, 1), w_ada, b_ada.reshape(1, n_out))


def _rms_modulate(x, mod_ref):
    y = x * lax.rsqrt(jnp.mean(x * x, axis=-1, keepdims=True) + RMS_EPS)
    return (y * mod_ref[0:1, :]) * (1.0 + mod_ref[1:2, :]) + mod_ref[2:3, :]


def _inproj_kernel(x0_ref, xn_ref, mod_ref, w_ref, anat_ref, am4_ref, am16_ref, b_ref, g_ref,
                   h_even, h_odd, res_scr, res4_scr):
    tm = xn_ref.shape[0]
    i = pl.program_id(0)

    @pl.when(i == 0)
    def _():
        h_even[...] = _rms_modulate(x0_ref[...], mod_ref).astype(BF16)

    def project(h_cur, h_next):
        for k, n in enumerate((6, 7, 8, 9, 0, 1, 2, 3, 4, 5)):
            cols = slice(n * WIDTH, (n + 1) * WIDTH)
            r = jnp.dot(h_cur[...], w_ref[:, cols], preferred_element_type=F32)
            if k == 2:
                h_next[...] = _rms_modulate(xn_ref[...], mod_ref).astype(BF16)
            if n < 3:
                anat_ref[:, cols] = r.astype(BF16)
                for t in range(WIDTH // LANES):
                    lanes = slice(n * WIDTH + t * LANES, n * WIDTH + (t + 1) * LANES)
                    res_scr[t] = r[:, t * LANES:(t + 1) * LANES]
                    for c in range(4):
                        cls4 = res_scr[t, pl.ds(c, tm // 4, stride=4), :]
                        am4_ref[c, :, lanes] = cls4.astype(BF16)
                        res4_scr[t, c] = cls4
                        for c2 in range(4):
                            am16_ref[c + 4 * c2, :, lanes] = (
                                res4_scr[t, c, pl.ds(c2, tm // 16, stride=4), :].astype(BF16))
            elif n < 6:
                b_ref[:, (n - 3) * WIDTH:(n - 2) * WIDTH] = r.astype(BF16)
            else:
                g_ref[:, (n - 6) * WIDTH:(n - 5) * WIDTH] = (1.0 / (1.0 + jnp.exp(-r))).astype(BF16)

    @pl.when(i % 2 == 0)
    def _():
        project(h_even, h_odd)

    @pl.when(i % 2 == 1)
    def _():
        project(h_odd, h_even)


def _inproj(x, mod1, w_in):
    tm = TM_IN
    nt = SEQ // tm
    qkv = 3 * WIDTH
    return pl.pallas_call(
        _inproj_kernel,
        grid=(nt,),
        in_specs=[
            pl.BlockSpec((tm, D_MODEL), lambda i: (0, 0)),
            pl.BlockSpec((tm, D_MODEL), lambda i: (jnp.minimum(i + 1, nt - 1), 0)),
            pl.BlockSpec((3, D_MODEL), lambda i: (0, 0)),
            pl.BlockSpec((D_MODEL, IN_COLS), lambda i: (0, 0), pipeline_mode=pl.Buffered(1)),
        ],
        out_specs=[
            pl.BlockSpec((tm, qkv), lambda i: (i, 0)),
            pl.BlockSpec((4, tm // 4, qkv), lambda i: (0, i, 0)),
            pl.BlockSpec((16, tm // 16, qkv), lambda i: (0, i, 0)),
            pl.BlockSpec((tm, qkv), lambda i: (i, 0)),
            pl.BlockSpec((tm, 2 * D_MODEL), lambda i: (i, 0)),
        ],
        out_shape=[
            jax.ShapeDtypeStruct((SEQ, qkv), BF16),
            jax.ShapeDtypeStruct((4, SEQ // 4, qkv), BF16),
            jax.ShapeDtypeStruct((16, SEQ // 16, qkv), BF16),
            jax.ShapeDtypeStruct((SEQ, qkv), BF16),
            jax.ShapeDtypeStruct((SEQ, 2 * D_MODEL), BF16),
        ],
        scratch_shapes=[pltpu.VMEM((tm, D_MODEL), BF16), pltpu.VMEM((tm, D_MODEL), BF16),
                        pltpu.VMEM((WIDTH // LANES, tm, LANES), F32),
                        pltpu.VMEM((WIDTH // LANES, 4, tm // 4, LANES), F32)],
        compiler_params=_params(("arbitrary",)),
        name="inproj",
    )(x, x, mod1, w_in)


def _stack_head_pair(q):
    first = lax.broadcasted_iota(jnp.int32, q.shape, 1) < HEAD_DIM
    qf = q.astype(F32)
    return jnp.concatenate([jnp.where(first, qf, 0.0), jnp.where(first, 0.0, qf)], axis=0).astype(q.dtype)


def _unstack_head_pair(o):
    n = o.shape[0] // 2
    first = lax.broadcasted_iota(jnp.int32, (n, o.shape[1]), 1) < HEAD_DIM
    return jnp.where(first, o[0:n], o[n:2 * n])


def _band_kernel(q_ref, kl_ref, km_ref, kr_ref, vl_ref, vm_ref, vr_ref, bias_ref,
                 o_ref, lse_ref, kcat, vcat, *, seg_len):
    tq = q_ref.shape[0]
    sub = SUB_BAND
    nsub = tq // sub
    nk = sub + 2 * BAND
    start = pl.program_id(0) * tq
    for cat, (l_ref, m_ref, r_ref) in ((kcat, (kl_ref, km_ref, kr_ref)), (vcat, (vl_ref, vm_ref, vr_ref))):
        cat[0:BAND, :] = l_ref[...]
        cat[BAND:BAND + tq, :] = m_ref[...]
        cat[BAND + tq:tq + 2 * BAND, :] = r_ref[...]
    col = lax.broadcasted_iota(jnp.int32, (1, nk), 1)
    edge = {0: jnp.where((col < BAND) & ((start % seg_len) == 0), NEG, 0.0),
            nsub - 1: jnp.where((col >= BAND + sub) & (((start + tq) % seg_len) == 0), NEG, 0.0)}
    lse_ref[...] = jnp.zeros_like(lse_ref)

    def scores(j, pr):
        ps = slice(pr * LANES, (pr + 1) * LANES)
        s = lax.dot_general(_stack_head_pair(q_ref[j * sub:(j + 1) * sub, ps]), kcat[j * sub:j * sub + nk, ps],
                            (((1,), (1,)), ((), ())), preferred_element_type=F32)
        s = s + bias_ref[pr]
        return s + edge[j] if j in edge else s

    def finish(j, pr, s):
        ps = slice(pr * LANES, (pr + 1) * LANES)
        rows = slice(j * sub, (j + 1) * sub)
        m = jnp.max(s, axis=-1, keepdims=True)
        p = jnp.exp(s - m)
        l = jnp.sum(p, axis=-1, keepdims=True)
        o = jnp.dot(p.astype(BF16), vcat[j * sub:j * sub + nk, ps], preferred_element_type=F32) / l
        o_ref[rows, ps] = _unstack_head_pair(o).astype(o_ref.dtype)
        lse = m + jnp.log(l)
        lse_ref[rows, 2 * pr:2 * pr + 1] = lse[0:sub]
        lse_ref[rows, 2 * pr + 1:2 * pr + 2] = lse[sub:2 * sub]

    pending = []
    for j in range(nsub):
        for pr in range(N_HEADS // 2):
            pending.append((j, pr, scores(j, pr)))
            if len(pending) > BAND_AHEAD:
                finish(*pending.pop(0))
    for unit in pending:
        finish(*unit)


def _band_attention(qkv, bias, seg_len):
    tq = TQ_BAND
    nt = SEQ // tq
    per = tq // BAND
    nhalo = SEQ // BAND
    left = lambda c: (lambda i: (jnp.maximum(i * per - 1, 0), c))
    right = lambda c: (lambda i: (jnp.minimum((i + 1) * per, nhalo - 1), c))
    main = lambda c: (lambda i: (i, c))
    kv_specs = []
    for c in (1, 2):
        kv_specs += [pl.BlockSpec((BAND, WIDTH), left(c)), pl.BlockSpec((tq, WIDTH), main(c)),
                     pl.BlockSpec((BAND, WIDTH), right(c))]
    nk = tq + 2 * BAND
    return pl.pallas_call(
        functools.partial(_band_kernel, seg_len=seg_len),
        grid=(nt,),
        in_specs=[pl.BlockSpec((tq, WIDTH), main(0))] + kv_specs
        + [pl.BlockSpec((N_HEADS // 2, 2 * SUB_BAND, SUB_BAND + 2 * BAND), lambda i: (0, 0, 0))],
        out_specs=[pl.BlockSpec((tq, WIDTH), lambda i: (i, 0)),
                   pl.BlockSpec((tq, LANES), lambda i: (i, 0))],
        out_shape=[jax.ShapeDtypeStruct((SEQ, WIDTH), BF16),
                   jax.ShapeDtypeStruct((SEQ, LANES), F32)],
        scratch_shapes=[pltpu.VMEM((nk, WIDTH), BF16), pltpu.VMEM((nk, WIDTH), BF16)],
        compiler_params=_params(("arbitrary",)),
        name=f"band_l{seg_len}",
    )(qkv, qkv, qkv, qkv, qkv, qkv, qkv, bias.reshape(N_HEADS // 2, 2 * SUB_BAND, SUB_BAND + 2 * BAND))


def _t5_bucket(rel):
    n = T5_BUCKETS // 2
    max_exact = n // 2
    sign_part = jnp.where(rel > 0, n, 0)
    a = jnp.abs(rel)
    af = jnp.maximum(a, 1).astype(F32)
    large = max_exact + (jnp.log(af / max_exact) / math.log(T5_MAX_DISTANCE / max_exact)
                         * (n - max_exact)).astype(jnp.int32)
    large = jnp.minimum(large, n - 1)
    return sign_part + jnp.where(a < max_exact, a, large)


def _band_bias(rel_bias, dilation):
    tq = SUB_BAND
    delta = jnp.arange(tq + 2 * BAND)[None, :] - BAND - jnp.arange(tq)[:, None]
    b = _lookup(rel_bias.astype(F32).T, _t5_bucket(delta * dilation), T5_BUCKETS)
    return jnp.where((jnp.abs(delta) <= BAND)[None], b, NEG)


def _lookup(table, idx, n):
    onehot = (idx[..., None] == jnp.arange(n)).astype(F32)
    return jnp.einsum("...b,xyb->...xy", table, onehot, precision=lax.Precision.HIGHEST)


def _na_kernel(q_ref, kp_ref, kc_ref, kn_ref, vp_ref, vc_ref, vn_ref, btab_ref, *rest):
    n_w = (len(rest) - 4) // 2
    w_in_refs, o_ref, w_out_refs = rest[:n_w], rest[n_w], rest[n_w + 1:2 * n_w + 1]
    kcat, vcat, bias_ref = rest[2 * n_w + 1:]
    for src, dst in zip(w_in_refs, w_out_refs):
        dst[...] = src[...].astype(dst.dtype)
    tile = q_ref.shape[0]
    i = pl.program_id(0)

    @pl.when(i == 0)
    def _():
        for v in range(NA_KH):
            for h in range(N_HEADS):
                for k in range(NA_KH // 2):
                    bias_ref[v, h // 2, (h % 2) * GRID_W:(h % 2 + 1) * GRID_W, k * LANES:(k + 1) * LANES] = (
                        btab_ref[h, v + 2 * k])
    for cat, refs in ((kcat, (kp_ref, kc_ref, kn_ref)), (vcat, (vp_ref, vc_ref, vn_ref))):
        for n, ref in enumerate(refs):
            cat[n * tile:(n + 1) * tile, :] = ref[...]
    nkeys = NA_KH * GRID_W

    def window(t):
        r = i * NA_KH + t
        rs = jnp.clip(r - NA_KH // 2, 0, ROWS - NA_KH)
        return pl.multiple_of((rs - (i - 1) * NA_KH) * GRID_W, GRID_W), rs - r + (NA_KH - 1)

    def scores(t, pr, off, var):
        ps = slice(pr * LANES, (pr + 1) * LANES)
        s = lax.dot_general(_stack_head_pair(q_ref[t * GRID_W:(t + 1) * GRID_W, ps]), kcat[pl.ds(off, nkeys), ps],
                            (((1,), (1,)), ((), ())), preferred_element_type=F32)
        return s + bias_ref[var, pr]

    def finish(t, pr, off, s):
        ps = slice(pr * LANES, (pr + 1) * LANES)
        m = jnp.max(s, axis=-1, keepdims=True)
        p = jnp.exp(s - m)
        l = jnp.sum(p, axis=-1, keepdims=True)
        o = jnp.dot(p.astype(BF16), vcat[pl.ds(off, nkeys), ps], preferred_element_type=F32) / l
        o_ref[t * GRID_W:(t + 1) * GRID_W, ps] = _unstack_head_pair(o).astype(o_ref.dtype)

    units = [(t, pr) for t in range(NA_KH) for pr in range(N_HEADS // 2)]
    wins = [window(t) for t in range(NA_KH)]
    pending = []
    for t, pr in units:
        off, var = wins[t]
        pending.append((t, pr, off, scores(t, pr, off, var)))
        if len(pending) > NA_AHEAD:
            finish(*pending.pop(0))
    for unit in pending:
        finish(*unit)


def _na_attention(qkv, btab, weights):
    tile = NA_KH * GRID_W
    nt = SEQ // tile
    w_specs = []
    for w in weights:
        rows = next(r for r in (w.shape[0] // nt, LANES) if r % HALO == 0 and w.shape[0] % r == 0)
        nblk = w.shape[0] // rows
        assert nblk <= nt
        w_specs.append(pl.BlockSpec((rows, w.shape[1]), lambda i, nblk=nblk: (jnp.minimum(i, nblk - 1), 0)))
    prev = lambda c: (lambda i: (jnp.maximum(i - 1, 0), c))
    cur = lambda c: (lambda i: (i, c))
    nxt = lambda c: (lambda i: (jnp.minimum(i + 1, nt - 1), c))
    kv_specs = [pl.BlockSpec((tile, WIDTH), f(c)) for c in (1, 2) for f in (prev, cur, nxt)]
    return pl.pallas_call(
        _na_kernel,
        grid=(nt,),
        in_specs=[pl.BlockSpec((tile, WIDTH), cur(0))] + kv_specs
        + [pl.BlockSpec(btab.shape, lambda i: (0, 0, 0, 0), pipeline_mode=pl.Buffered(1))] + w_specs,
        out_specs=[pl.BlockSpec((tile, WIDTH), lambda i: (i, 0))] + w_specs,
        out_shape=[jax.ShapeDtypeStruct((SEQ, WIDTH), BF16)]
        + [jax.ShapeDtypeStruct(w.shape, BF16) for w in weights],
        scratch_shapes=[pltpu.VMEM((3 * tile, WIDTH), BF16), pltpu.VMEM((3 * tile, WIDTH), BF16),
                        pltpu.VMEM((NA_KH, N_HEADS // 2, 2 * GRID_W, NA_KH * GRID_W), F32)],
        compiler_params=_params(("arbitrary",)),
        name="na",
    )(qkv, qkv, qkv, qkv, qkv, qkv, qkv, btab, *weights)


def _na_bias(rpb):
    cq = jnp.arange(GRID_W)
    col_start = jnp.clip(cq - NA_COLS // 2, 0, GRID_W - NA_COLS)
    col_mask = (cq[None, :] >= col_start[:, None]) & (cq[None, :] < col_start[:, None] + NA_COLS)
    dc = jnp.clip(cq[None, :] - cq[:, None], -(NA_COLS - 1), NA_COLS - 1) + NA_COLS - 1
    e = _lookup(rpb.astype(F32), dc, 2 * NA_COLS - 1)
    e = jnp.where(col_mask[None, None], e, NEG)
    return jnp.concatenate([e[:, :-1], e[:, 1:]], axis=-1)


def _expand_heads(w, exp_ref):
    hi = w.astype(BF16)
    lo = (w - hi.astype(F32)).astype(BF16)
    return (jnp.dot(hi, exp_ref[...], preferred_element_type=F32)
            + jnp.dot(lo, exp_ref[...], preferred_element_type=F32))


def _merge_kernel(x_ref, o1_ref, o4_ref, o16_ref, l1_ref, l4_ref, l16_ref, ob_ref, g_ref,
                  wbd_ref, wbn_ref, wout_ref, exp_ref, gate_ref, mod_ref,
                  x1_ref, h2_ref, o4n, o16n, o16j, l4n, l16n, l16j, oa_scr):
    tm = x_ref.shape[0]
    for src4, src16, n4, j16, n16 in ((o4_ref, o16_ref, o4n, o16j, o16n), (l4_ref, l16_ref, l4n, l16j, l16n)):
        for t in range(n4.shape[0]):
            ls = slice(t * LANES, (t + 1) * LANES)
            for c in range(4):
                n4[t, pl.ds(c, tm // 4, stride=4), :] = src4[c, :, ls].astype(F32)
                for c2 in range(4):
                    j16[t, c, pl.ds(c2, tm // 16, stride=4), :] = src16[c + 4 * c2, :, ls].astype(F32)
                n16[t, pl.ds(c, tm // 4, stride=4), :] = j16[t, c]

    for sb in range(tm // SUB_MERGE):
        rows = slice(sb * SUB_MERGE, (sb + 1) * SUB_MERGE)
        l1, l4, l16 = l1_ref[rows, :], l4n[0, rows, :], l16n[0, rows, :]
        mx = jnp.maximum(jnp.maximum(l1, l4), l16)
        e1, e4, e16 = jnp.exp(l1 - mx), jnp.exp(l4 - mx), jnp.exp(l16 - mx)
        den = e1 + e4 + e16
        w1, w4, w16 = (_expand_heads(e / den, exp_ref) for e in (e1, e4, e16))
        for t in range(WIDTH // LANES):
            ls = slice(t * LANES, (t + 1) * LANES)
            oa = w1[:, ls] * o1_ref[rows, ls] + w4[:, ls] * o4n[t, rows, :] + w16[:, ls] * o16n[t, rows, :]
            oa_scr[sb, :, ls] = oa.astype(BF16)
        ya = jnp.dot(oa_scr[sb], wbd_ref[...], preferred_element_type=F32)
        yb = jnp.dot(ob_ref[rows, :], wbn_ref[...], preferred_element_type=F32)
        merged = (g_ref[rows, 0:D_MODEL].astype(F32) * ya + g_ref[rows, D_MODEL:2 * D_MODEL].astype(F32) * yb)
        z = jnp.dot(merged.astype(BF16), wout_ref[...], preferred_element_type=F32)
        x1 = x_ref[rows, :] + gate_ref[...] * z
        x1_ref[rows, :] = x1
        h2_ref[rows, :] = _rms_modulate(x1, mod_ref).astype(BF16)


def _merge(x, o1, o4, o16, l1, l4, l16, ob, g, wbd, wbn, wout, gate1, mod2):
    tm = TM_MERGE
    nt = SEQ // tm
    row = lambda w: pl.BlockSpec((tm, w), lambda i: (i, 0))
    cls = lambda d, w: pl.BlockSpec((d, tm // d, w), lambda i: (0, i, 0))
    const = lambda shape: pl.BlockSpec(shape, lambda i: (0, 0))
    spread = (jnp.arange(LANES)[:, None] == jnp.arange(WIDTH)[None, :] // HEAD_DIM).astype(BF16)
    return pl.pallas_call(
        _merge_kernel,
        grid=(nt,),
        in_specs=[row(D_MODEL), row(WIDTH), cls(4, WIDTH), cls(16, WIDTH),
                  row(LANES), cls(4, LANES), cls(16, LANES), row(WIDTH), row(2 * D_MODEL),
                  const((WIDTH, D_MODEL)), const((WIDTH, D_MODEL)), const((D_MODEL, D_MODEL)),
                  const((LANES, WIDTH)), const((1, D_MODEL)), const((3, D_MODEL))],
        out_specs=[row(D_MODEL), row(D_MODEL)],
        out_shape=[jax.ShapeDtypeStruct((SEQ, D_MODEL), F32), jax.ShapeDtypeStruct((SEQ, D_MODEL), BF16)],
        scratch_shapes=[pltpu.VMEM((WIDTH // LANES, tm, LANES), F32), pltpu.VMEM((WIDTH // LANES, tm, LANES), F32),
                        pltpu.VMEM((WIDTH // LANES, 4, tm // 4, LANES), F32),
                        pltpu.VMEM((1, tm, LANES), F32), pltpu.VMEM((1, tm, LANES), F32),
                        pltpu.VMEM((1, 4, tm // 4, LANES), F32),
                        pltpu.VMEM((tm // SUB_MERGE, SUB_MERGE, WIDTH), BF16)],
        compiler_params=_params(("arbitrary",)),
        name="merge",
    )(x, o1, o4.reshape(4, SEQ // 4, WIDTH), o16.reshape(16, SEQ // 16, WIDTH),
      l1, l4.reshape(4, SEQ // 4, LANES), l16.reshape(16, SEQ // 16, LANES), ob, g,
      wbd, wbn, wout, spread, gate1, mod2)


def _ffn_kernel(x1_ref, hm_ref, hp_ref, hn_ref, wup_ref, cw_ref, cb_ref, wd_ref, gate_ref, gfin_ref,
                y_ref, hext, act):
    tm = x1_ref.shape[0]
    n = tm + 2 * HALO
    i = pl.program_id(0)
    hext[0:HALO, :] = jnp.where(i == 0, jnp.zeros_like(hp_ref), hp_ref[...])
    hext[HALO:HALO + tm, :] = hm_ref[...]
    hext[HALO + tm:n, :] = jnp.where(i == pl.num_programs(0) - 1, jnp.zeros_like(hn_ref), hn_ref[...])

    def conv(cols):
        p = jnp.dot(hext[...], wup_ref[:, cols], preferred_element_type=F32)
        prev = pltpu.roll(p, 1, 0)[HALO:HALO + tm]
        nxt = pltpu.roll(p, n - 1, 0)[HALO:HALO + tm]
        return (cw_ref[0:1, cols] * prev + cw_ref[1:2, cols] * p[HALO:HALO + tm] + cw_ref[2:3, cols] * nxt
                + cb_ref[:, cols])

    for c in range(D_FF // TF_FFN):
        val = slice(c * TF_FFN, (c + 1) * TF_FFN)
        gate = slice(D_FF + c * TF_FFN, D_FF + (c + 1) * TF_FFN)
        act[:, val] = (jax.nn.gelu(conv(gate), approximate=True) * conv(val)).astype(BF16)

    x2 = x1_ref[...] + gate_ref[...] * jnp.dot(act[...], wd_ref[...], preferred_element_type=F32)
    y = x2 * lax.rsqrt(jnp.mean(x2 * x2, axis=-1, keepdims=True) + RMS_EPS)
    y_ref[...] = y * gfin_ref[...]


def _ffn(x1, h2, w_up, conv_w, conv_b, w_down, gate2, g_final):
    tm = TM_FFN
    nt = SEQ // tm
    per = tm // HALO
    nhalo = SEQ // HALO
    row = lambda: pl.BlockSpec((tm, D_MODEL), lambda i: (i, 0))
    const = lambda a: pl.BlockSpec(a.shape, lambda i: (0, 0), pipeline_mode=pl.Buffered(1))
    return pl.pallas_call(
        _ffn_kernel,
        grid=(nt,),
        in_specs=[
            row(), row(),
            pl.BlockSpec((HALO, D_MODEL), lambda i: (jnp.maximum(i * per - 1, 0), 0)),
            pl.BlockSpec((HALO, D_MODEL), lambda i: (jnp.minimum((i + 1) * per, nhalo - 1), 0)),
            const(w_up), const(conv_w), const(conv_b), const(w_down), const(gate2), const(g_final),
        ],
        out_specs=row(),
        out_shape=jax.ShapeDtypeStruct((SEQ, D_MODEL), F32),
        scratch_shapes=[pltpu.VMEM((tm + 2 * HALO, D_MODEL), BF16), pltpu.VMEM((tm, D_FF), BF16)],
        compiler_params=_params(("arbitrary",)),
        name="ffn",
    )(x1, h2, h2, h2, w_up, conv_w, conv_b, w_down, gate2, g_final)


def kernel(x, c, w_ada, b_ada, g_mix, w_in, rel_bias, na_rpb, w_branch_dil, w_branch_na, w_out, g_ffn,
           w_up, conv_w, conv_b, w_down, g_final):
    assert x.shape == (1, SEQ, D_MODEL) and w_ada.shape[0] == 1
    xs = x[0]
    ada = _ada(c, w_ada[0], b_ada[0])
    shift1, scale1, gate1, shift2, scale2, gate2 = [ada[:, k * D_MODEL:(k + 1) * D_MODEL] for k in range(6)]
    mod1 = jnp.concatenate([g_mix[0][None], scale1, shift1], axis=0)
    mod2 = jnp.concatenate([g_ffn[0][None], scale2, shift2], axis=0)

    q_scale = np.ones((IN_COLS,), np.float32)
    q_scale[0:WIDTH] = HEAD_DIM ** -0.5
    q_scale[3 * WIDTH:4 * WIDTH] = HEAD_DIM ** -0.5
    w_in_b = (w_in[0] * q_scale).astype(BF16)

    a_nat, a_m4, a_m16, b_qkv, gates = _inproj(xs, mod1, w_in_b)

    outs = []
    for d, arr in zip(DILATIONS, (a_nat, a_m4, a_m16)):
        outs.append(_band_attention(arr.reshape(SEQ, 3 * WIDTH), _band_bias(rel_bias, d), SEQ // d))
    (o1, l1), (o4, l4), (o16, l16) = outs
    ob, wbd, wbn, wout, wup, wdown = _na_attention(
        b_qkv, _na_bias(na_rpb[0]), (w_branch_dil[0], w_branch_na[0], w_out[0], w_up[0], w_down[0]))

    x1, h2 = _merge(xs, o1, o4, o16, l1, l4, l16, ob, gates, wbd, wbn, wout, gate1, mod2)
    y = _ffn(x1, h2, wup, conv_w[0], conv_b[0].reshape(1, 2 * D_FF), wdown, gate2, g_final.reshape(1, D_MODEL))
    return y[None]
```

```python
import functools
import math

import numpy as np
import jax
import jax.numpy as jnp
from jax import lax
from jax.experimental import pallas as pl
from jax.experimental.pallas import tpu as pltpu

F32 = jnp.float32
BF16 = jnp.bfloat16

D_MODEL = 1024
SEQ = 16384
GRID_W = 64
ROWS = SEQ // GRID_W
HEAD_DIM = 64
N_HEADS = 8
WIDTH = N_HEADS * HEAD_DIM
DILATIONS = (1, 4, 16)
BAND = 64
NA_KH = 8
NA_COLS = 16
NA_ROWS_MAX = 8
T5_BUCKETS = 32
T5_MAX_DISTANCE = 1024
D_FF = 2816
RMS_EPS = 1e-6
IN_COLS = 6 * WIDTH + 2 * D_MODEL
NEG = -1e30
LOG2E = math.log2(math.e)

VMEM_LIMIT = 56 * 1024 * 1024

TM_IN = 512
TQ_BAND = 512
SUB_BAND = 128
TM_MERGE = 512
SUB_MERGE = 256
TM_FFN = 1024
TF_FFN = 256
HALO = 16
LANES = 128
NA_AHEAD = 3
BAND_AHEAD = 3


def _params(sem):
    return pltpu.CompilerParams(dimension_semantics=sem, vmem_limit_bytes=VMEM_LIMIT)


def _ada_kernel(c_ref, w_ref, b_ref, o_ref):
    c = c_ref[...]
    s = c / (1.0 + jnp.exp(-c))
    o_ref[...] = jnp.sum(w_ref[...] * s, axis=0, keepdims=True) + b_ref[...]


def _ada(c, w_ada, b_ada):
    n_out = w_ada.shape[1]
    tn = D_MODEL
    return pl.pallas_call(
        _ada_kernel,
        grid=(n_out // tn,),
        in_specs=[
            pl.BlockSpec((D_MODEL, 1), lambda j: (0, 0)),
            pl.BlockSpec((D_MODEL, tn), lambda j: (0, j)),
            pl.BlockSpec((1, tn), lambda j: (0, j)),
        ],
        out_specs=pl.BlockSpec((1, tn), lambda j: (0, j)),
        out_shape=jax.ShapeDtypeStruct((1, n_out), F32),
        compiler_params=_params(("arbitrary",)),
        name="ada",
    )(c.reshape(D_MODEL, 1), w_ada, b_ada.reshape(1, n_out))


def _rms_modulate(x, mod_ref):
    y = x * lax.rsqrt(jnp.mean(x * x, axis=-1, keepdims=True) + RMS_EPS)
    return (y * mod_ref[0:1, :]) * (1.0 + mod_ref[1:2, :]) + mod_ref[2:3, :]


def _inproj_kernel(x0_ref, xn_ref, mod_ref, w_ref, anat_ref, am4_ref, am16_ref, b_ref, g_ref,
                   h_even, h_odd, res_scr, res4_scr):
    tm = xn_ref.shape[0]
    i = pl.program_id(0)

    @pl.when(i == 0)
    def _():
        h_even[...] = _rms_modulate(x0_ref[...], mod_ref).astype(BF16)

    def project(h_cur, h_next):
        for k, n in enumerate((6, 7, 8, 9, 0, 1, 2, 3, 4, 5)):
            cols = slice(n * WIDTH, (n + 1) * WIDTH)
            r = jnp.dot(h_cur[...], w_ref[:, cols], preferred_element_type=F32)
            if k == 2:
                h_next[...] = _rms_modulate(xn_ref[...], mod_ref).astype(BF16)
            if n < 3:
                anat_ref[:, cols] = r.astype(BF16)
                for t in range(WIDTH // LANES):
                    lanes = slice(n * WIDTH + t * LANES, n * WIDTH + (t + 1) * LANES)
                    res_scr[t] = r[:, t * LANES:(t + 1) * LANES]
                    for c in range(4):
                        cls4 = res_scr[t, pl.ds(c, tm // 4, stride=4), :]
                        am4_ref[c, :, lanes] = cls4.astype(BF16)
                        res4_scr[t, c] = cls4
                        for c2 in range(4):
                            am16_ref[c + 4 * c2, :, lanes] = (
                                res4_scr[t, c, pl.ds(c2, tm // 16, stride=4), :].astype(BF16))
            elif n < 6:
                b_ref[:, (n - 3) * WIDTH:(n - 2) * WIDTH] = r.astype(BF16)
            else:
                g_ref[:, (n - 6) * WIDTH:(n - 5) * WIDTH] = (1.0 / (1.0 + jnp.exp(-r))).astype(BF16)

    @pl.when(i % 2 == 0)
    def _():
        project(h_even, h_odd)

    @pl.when(i % 2 == 1)
    def _():
        project(h_odd, h_even)


def _inproj(x, mod1, w_in):
    tm = TM_IN
    nt = SEQ // tm
    qkv = 3 * WIDTH
    return pl.pallas_call(
        _inproj_kernel,
        grid=(nt,),
        in_specs=[
            pl.BlockSpec((tm, D_MODEL), lambda i: (0, 0)),
            pl.BlockSpec((tm, D_MODEL), lambda i: (jnp.minimum(i + 1, nt - 1), 0)),
            pl.BlockSpec((3, D_MODEL), lambda i: (0, 0)),
            pl.BlockSpec((D_MODEL, IN_COLS), lambda i: (0, 0), pipeline_mode=pl.Buffered(1)),
        ],
        out_specs=[
            pl.BlockSpec((tm, qkv), lambda i: (i, 0)),
            pl.BlockSpec((4, tm // 4, qkv), lambda i: (0, i, 0)),
            pl.BlockSpec((16, tm // 16, qkv), lambda i: (0, i, 0)),
            pl.BlockSpec((tm, qkv), lambda i: (i, 0)),
            pl.BlockSpec((tm, 2 * D_MODEL), lambda i: (i, 0)),
        ],
        out_shape=[
            jax.ShapeDtypeStruct((SEQ, qkv), BF16),
            jax.ShapeDtypeStruct((4, SEQ // 4, qkv), BF16),
            jax.ShapeDtypeStruct((16, SEQ // 16, qkv), BF16),
            jax.ShapeDtypeStruct((SEQ, qkv), BF16),
            jax.ShapeDtypeStruct((SEQ, 2 * D_MODEL), BF16),
        ],
        scratch_shapes=[pltpu.VMEM((tm, D_MODEL), BF16), pltpu.VMEM((tm, D_MODEL), BF16),
                        pltpu.VMEM((WIDTH // LANES, tm, LANES), F32),
                        pltpu.VMEM((WIDTH // LANES, 4, tm // 4, LANES), F32)],
        compiler_params=_params(("arbitrary",)),
        name="inproj",
    )(x, x, mod1, w_in)


def _stack_head_pair(q):
    first = lax.broadcasted_iota(jnp.int32, q.shape, 1) < HEAD_DIM
    qf = q.astype(F32)
    return jnp.concatenate([jnp.where(first, qf, 0.0), jnp.where(first, 0.0, qf)], axis=0).astype(q.dtype)


def _unstack_head_pair(o):
    n = o.shape[0] // 2
    first = lax.broadcasted_iota(jnp.int32, (n, o.shape[1]), 1) < HEAD_DIM
    return jnp.where(first, o[0:n], o[n:2 * n])


def _fill_value_ones(vext, first_step):
    @pl.when(first_step)
    def _():
        vext[:, :, LANES:2 * LANES] = jnp.ones(vext.shape[:2] + (LANES,), vext.dtype)


def _band_kernel(q_ref, kl_ref, km_ref, kr_ref, vl_ref, vm_ref, vr_ref, bias_ref,
                 o_ref, lse_ref, kcat, vext, *, seg_len):
    tq = q_ref.shape[0]
    sub = SUB_BAND
    nsub = tq // sub
    nk = sub + 2 * BAND
    start = pl.program_id(0) * tq
    kcat[0:BAND, :] = kl_ref[...]
    kcat[BAND:BAND + tq, :] = km_ref[...]
    kcat[BAND + tq:tq + 2 * BAND, :] = kr_ref[...]
    _fill_value_ones(vext, pl.program_id(0) == 0)
    for pr in range(N_HEADS // 2):
        ps = slice(pr * LANES, (pr + 1) * LANES)
        vext[pr, 0:BAND, 0:LANES] = vl_ref[:, ps]
        vext[pr, BAND:BAND + tq, 0:LANES] = vm_ref[:, ps]
        vext[pr, BAND + tq:tq + 2 * BAND, 0:LANES] = vr_ref[:, ps]
    col = lax.broadcasted_iota(jnp.int32, (1, nk), 1)
    edge = {0: jnp.where((col < BAND) & ((start % seg_len) == 0), NEG, 0.0),
            nsub - 1: jnp.where((col >= BAND + sub) & (((start + tq) % seg_len) == 0), NEG, 0.0)}
    lse_ref[...] = jnp.zeros_like(lse_ref)

    def scores(j, pr):
        ps = slice(pr * LANES, (pr + 1) * LANES)
        s = lax.dot_general(_stack_head_pair(q_ref[j * sub:(j + 1) * sub, ps]), kcat[j * sub:j * sub + nk, ps],
                            (((1,), (1,)), ((), ())), preferred_element_type=F32)
        s = s + bias_ref[pr]
        return s + edge[j] if j in edge else s

    def finish(j, pr, s):
        ps = slice(pr * LANES, (pr + 1) * LANES)
        rows = slice(j * sub, (j + 1) * sub)
        m = jnp.max(s, axis=-1, keepdims=True)
        p = jnp.exp2(s - m)
        ol = jnp.dot(p.astype(BF16), vext[pr, j * sub:j * sub + nk, :], preferred_element_type=F32)
        o = ol[:, 0:LANES] / ol[:, LANES:2 * LANES]
        o_ref[rows, ps] = _unstack_head_pair(o).astype(o_ref.dtype)
        lse = m + jnp.log2(ol[:, LANES:2 * LANES])
        lse_ref[rows, 2 * pr:2 * pr + 1] = lse[0:sub, 2 * pr:2 * pr + 1]
        lse_ref[rows, 2 * pr + 1:2 * pr + 2] = lse[sub:2 * sub, 2 * pr + 1:2 * pr + 2]

    pending = []
    for j in range(nsub):
        for pr in range(N_HEADS // 2):
            pending.append((j, pr, scores(j, pr)))
            if len(pending) > BAND_AHEAD:
                finish(*pending.pop(0))
    for unit in pending:
        finish(*unit)


def _band_attention(qkv, bias, seg_len):
    tq = TQ_BAND
    nt = SEQ // tq
    per = tq // BAND
    nhalo = SEQ // BAND
    left = lambda c: (lambda i: (jnp.maximum(i * per - 1, 0), c))
    right = lambda c: (lambda i: (jnp.minimum((i + 1) * per, nhalo - 1), c))
    main = lambda c: (lambda i: (i, c))
    kv_specs = []
    for c in (1, 2):
        kv_specs += [pl.BlockSpec((BAND, WIDTH), left(c)), pl.BlockSpec((tq, WIDTH), main(c)),
                     pl.BlockSpec((BAND, WIDTH), right(c))]
    nk = tq + 2 * BAND
    return pl.pallas_call(
        functools.partial(_band_kernel, seg_len=seg_len),
        grid=(nt,),
        in_specs=[pl.BlockSpec((tq, WIDTH), main(0))] + kv_specs
        + [pl.BlockSpec((N_HEADS // 2, 2 * SUB_BAND, SUB_BAND + 2 * BAND), lambda i: (0, 0, 0))],
        out_specs=[pl.BlockSpec((tq, WIDTH), lambda i: (i, 0)),
                   pl.BlockSpec((tq, LANES), lambda i: (i, 0))],
        out_shape=[jax.ShapeDtypeStruct((SEQ, WIDTH), BF16),
                   jax.ShapeDtypeStruct((SEQ, LANES), F32)],
        scratch_shapes=[pltpu.VMEM((nk, WIDTH), BF16), pltpu.VMEM((N_HEADS // 2, nk, 2 * LANES), BF16)],
        compiler_params=_params(("arbitrary",)),
        name=f"band_l{seg_len}",
    )(qkv, qkv, qkv, qkv, qkv, qkv, qkv, bias.reshape(N_HEADS // 2, 2 * SUB_BAND, SUB_BAND + 2 * BAND))


def _t5_bucket(rel):
    n = T5_BUCKETS // 2
    max_exact = n // 2
    sign_part = jnp.where(rel > 0, n, 0)
    a = jnp.abs(rel)
    af = jnp.maximum(a, 1).astype(F32)
    large = max_exact + (jnp.log(af / max_exact) / math.log(T5_MAX_DISTANCE / max_exact)
                         * (n - max_exact)).astype(jnp.int32)
    large = jnp.minimum(large, n - 1)
    return sign_part + jnp.where(a < max_exact, a, large)


def _band_bias(rel_bias, dilation):
    tq = SUB_BAND
    delta = jnp.arange(tq + 2 * BAND)[None, :] - BAND - jnp.arange(tq)[:, None]
    b = _lookup(rel_bias.astype(F32).T, _t5_bucket(delta * dilation), T5_BUCKETS)
    return jnp.where((jnp.abs(delta) <= BAND)[None], b * LOG2E, NEG)


def _lookup(table, idx, n):
    onehot = (idx[..., None] == jnp.arange(n)).astype(F32)
    return jnp.einsum("...b,xyb->...xy", table, onehot, precision=lax.Precision.HIGHEST)


def _na_kernel(q_ref, kp_ref, kc_ref, kn_ref, vp_ref, vc_ref, vn_ref, btab_ref, *rest):
    n_w = (len(rest) - 4) // 2
    w_in_refs, o_ref, w_out_refs = rest[:n_w], rest[n_w], rest[n_w + 1:2 * n_w + 1]
    kcat, vext, bias_ref = rest[2 * n_w + 1:]
    for src, dst in zip(w_in_refs, w_out_refs):
        dst[...] = src[...].astype(dst.dtype)
    tile = q_ref.shape[0]
    i = pl.program_id(0)

    @pl.when(i == 0)
    def _():
        for v in range(NA_KH):
            for h in range(N_HEADS):
                for k in range(NA_KH // 2):
                    bias_ref[v, h // 2, (h % 2) * GRID_W:(h % 2 + 1) * GRID_W, k * LANES:(k + 1) * LANES] = (
                        btab_ref[h, v + 2 * k])
    _fill_value_ones(vext, i == 0)
    for n, (k_ref, v_ref) in enumerate(((kp_ref, vp_ref), (kc_ref, vc_ref), (kn_ref, vn_ref))):
        kcat[n * tile:(n + 1) * tile, :] = k_ref[...]
        for pr in range(N_HEADS // 2):
            vext[pr, n * tile:(n + 1) * tile, 0:LANES] = v_ref[:, pr * LANES:(pr + 1) * LANES]
    nkeys = NA_KH * GRID_W

    def window(t):
        r = i * NA_KH + t
        rs = jnp.clip(r - NA_KH // 2, 0, ROWS - NA_KH)
        return pl.multiple_of((rs - (i - 1) * NA_KH) * GRID_W, GRID_W), rs - r + (NA_KH - 1)

    def scores(t, pr, off, var):
        ps = slice(pr * LANES, (pr + 1) * LANES)
        s = lax.dot_general(_stack_head_pair(q_ref[t * GRID_W:(t + 1) * GRID_W, ps]), kcat[pl.ds(off, nkeys), ps],
                            (((1,), (1,)), ((), ())), preferred_element_type=F32)
        return s + bias_ref[var, pr]

    def finish(t, pr, off, s):
        ps = slice(pr * LANES, (pr + 1) * LANES)
        m = jnp.max(s, axis=-1, keepdims=True)
        p = jnp.exp2(s - m)
        ol = jnp.dot(p.astype(BF16), vext[pr, pl.ds(off, nkeys), :], preferred_element_type=F32)
        o = ol[:, 0:LANES] / ol[:, LANES:2 * LANES]
        o_ref[t * GRID_W:(t + 1) * GRID_W, ps] = _unstack_head_pair(o).astype(o_ref.dtype)

    units = [(t, pr) for t in range(NA_KH) for pr in range(N_HEADS // 2)]
    wins = [window(t) for t in range(NA_KH)]
    pending = []
    for t, pr in units:
        off, var = wins[t]
        pending.append((t, pr, off, scores(t, pr, off, var)))
        if len(pending) > NA_AHEAD:
            finish(*pending.pop(0))
    for unit in pending:
        finish(*unit)


def _na_attention(qkv, btab, weights):
    tile = NA_KH * GRID_W
    nt = SEQ // tile
    w_specs = []
    for w in weights:
        rows = next(r for r in (w.shape[0] // nt, LANES) if r % HALO == 0 and w.shape[0] % r == 0)
        nblk = w.shape[0] // rows
        assert nblk <= nt
        w_specs.append(pl.BlockSpec((rows, w.shape[1]), lambda i, nblk=nblk: (jnp.minimum(i, nblk - 1), 0)))
    prev = lambda c: (lambda i: (jnp.maximum(i - 1, 0), c))
    cur = lambda c: (lambda i: (i, c))
    nxt = lambda c: (lambda i: (jnp.minimum(i + 1, nt - 1), c))
    kv_specs = [pl.BlockSpec((tile, WIDTH), f(c)) for c in (1, 2) for f in (prev, cur, nxt)]
    return pl.pallas_call(
        _na_kernel,
        grid=(nt,),
        in_specs=[pl.BlockSpec((tile, WIDTH), cur(0))] + kv_specs
        + [pl.BlockSpec(btab.shape, lambda i: (0, 0, 0, 0), pipeline_mode=pl.Buffered(1))] + w_specs,
        out_specs=[pl.BlockSpec((tile, WIDTH), lambda i: (i, 0))] + w_specs,
        out_shape=[jax.ShapeDtypeStruct((SEQ, WIDTH), BF16)]
        + [jax.ShapeDtypeStruct(w.shape, BF16) for w in weights],
        scratch_shapes=[pltpu.VMEM((3 * tile, WIDTH), BF16), pltpu.VMEM((N_HEADS // 2, 3 * tile, 2 * LANES), BF16),
                        pltpu.VMEM((NA_KH, N_HEADS // 2, 2 * GRID_W, NA_KH * GRID_W), F32)],
        compiler_params=_params(("arbitrary",)),
        name="na",
    )(qkv, qkv, qkv, qkv, qkv, qkv, qkv, btab, *weights)


def _na_bias(rpb):
    cq = jnp.arange(GRID_W)
    col_start = jnp.clip(cq - NA_COLS // 2, 0, GRID_W - NA_COLS)
    col_mask = (cq[None, :] >= col_start[:, None]) & (cq[None, :] < col_start[:, None] + NA_COLS)
    dc = jnp.clip(cq[None, :] - cq[:, None], -(NA_COLS - 1), NA_COLS - 1) + NA_COLS - 1
    e = _lookup(rpb.astype(F32), dc, 2 * NA_COLS - 1)
    e = jnp.where(col_mask[None, None], e * LOG2E, NEG)
    return jnp.concatenate([e[:, :-1], e[:, 1:]], axis=-1)


def _expand_heads(w, exp_ref):
    hi = w.astype(BF16)
    lo = (w - hi.astype(F32)).astype(BF16)
    return (jnp.dot(hi, exp_ref[...], preferred_element_type=F32)
            + jnp.dot(lo, exp_ref[...], preferred_element_type=F32))


def _merge_kernel(x_ref, o1_ref, o4_ref, o16_ref, l1_ref, l4_ref, l16_ref, ob_ref, g_ref,
                  wbd_ref, wbn_ref, wout_ref, exp_ref, gate_ref, mod_ref,
                  x1_ref, h2_ref, o4n, o16n, o16j, l4n, l16n, l16j, oa_scr):
    tm = x_ref.shape[0]
    for src4, src16, n4, j16, n16 in ((o4_ref, o16_ref, o4n, o16j, o16n), (l4_ref, l16_ref, l4n, l16j, l16n)):
        for t in range(n4.shape[0]):
            ls = slice(t * LANES, (t + 1) * LANES)
            for c in range(4):
                n4[t, pl.ds(c, tm // 4, stride=4), :] = src4[c, :, ls].astype(F32)
                for c2 in range(4):
                    j16[t, c, pl.ds(c2, tm // 16, stride=4), :] = src16[c + 4 * c2, :, ls].astype(F32)
                n16[t, pl.ds(c, tm // 4, stride=4), :] = j16[t, c]

    for sb in range(tm // SUB_MERGE):
        rows = slice(sb * SUB_MERGE, (sb + 1) * SUB_MERGE)
        l1, l4, l16 = l1_ref[rows, :], l4n[0, rows, :], l16n[0, rows, :]
        mx = jnp.maximum(jnp.maximum(l1, l4), l16)
        e1, e4, e16 = jnp.exp2(l1 - mx), jnp.exp2(l4 - mx), jnp.exp2(l16 - mx)
        den = e1 + e4 + e16
        w1, w4, w16 = (_expand_heads(e / den, exp_ref) for e in (e1, e4, e16))
        for t in range(WIDTH // LANES):
            ls = slice(t * LANES, (t + 1) * LANES)
            oa = w1[:, ls] * o1_ref[rows, ls] + w4[:, ls] * o4n[t, rows, :] + w16[:, ls] * o16n[t, rows, :]
            oa_scr[sb, :, ls] = oa.astype(BF16)
        ya = jnp.dot(oa_scr[sb], wbd_ref[...], preferred_element_type=F32)
        yb = jnp.dot(ob_ref[rows, :], wbn_ref[...], preferred_element_type=F32)
        merged = (g_ref[rows, 0:D_MODEL].astype(F32) * ya + g_ref[rows, D_MODEL:2 * D_MODEL].astype(F32) * yb)
        z = jnp.dot(merged.astype(BF16), wout_ref[...], preferred_element_type=F32)
        x1 = x_ref[rows, :] + gate_ref[...] * z
        x1_ref[rows, :] = x1
        h2_ref[rows, :] = _rms_modulate(x1, mod_ref).astype(BF16)


def _merge(x, o1, o4, o16, l1, l4, l16, ob, g, wbd, wbn, wout, gate1, mod2):
    tm = TM_MERGE
    nt = SEQ // tm
    row = lambda w: pl.BlockSpec((tm, w), lambda i: (i, 0))
    cls = lambda d, w: pl.BlockSpec((d, tm // d, w), lambda i: (0, i, 0))
    const = lambda shape: pl.BlockSpec(shape, lambda i: (0, 0))
    spread = (jnp.arange(LANES)[:, None] == jnp.arange(WIDTH)[None, :] // HEAD_DIM).astype(BF16)
    return pl.pallas_call(
        _merge_kernel,
        grid=(nt,),
        in_specs=[row(D_MODEL), row(WIDTH), cls(4, WIDTH), cls(16, WIDTH),
                  row(LANES), cls(4, LANES), cls(16, LANES), row(WIDTH), row(2 * D_MODEL),
                  const((WIDTH, D_MODEL)), const((WIDTH, D_MODEL)), const((D_MODEL, D_MODEL)),
                  const((LANES, WIDTH)), const((1, D_MODEL)), const((3, D_MODEL))],
        out_specs=[row(D_MODEL), row(D_MODEL)],
        out_shape=[jax.ShapeDtypeStruct((SEQ, D_MODEL), F32), jax.ShapeDtypeStruct((SEQ, D_MODEL), BF16)],
        scratch_shapes=[pltpu.VMEM((WIDTH // LANES, tm, LANES), F32), pltpu.VMEM((WIDTH // LANES, tm, LANES), F32),
                        pltpu.VMEM((WIDTH // LANES, 4, tm // 4, LANES), F32),
                        pltpu.VMEM((1, tm, LANES), F32), pltpu.VMEM((1, tm, LANES), F32),
                        pltpu.VMEM((1, 4, tm // 4, LANES), F32),
                        pltpu.VMEM((tm // SUB_MERGE, SUB_MERGE, WIDTH), BF16)],
        compiler_params=_params(("arbitrary",)),
        name="merge",
    )(x, o1, o4.reshape(4, SEQ // 4, WIDTH), o16.reshape(16, SEQ // 16, WIDTH),
      l1, l4.reshape(4, SEQ // 4, LANES), l16.reshape(16, SEQ // 16, LANES), ob, g,
      wbd, wbn, wout, spread, gate1, mod2)


def _ffn_kernel(x1_ref, hm_ref, hp_ref, hn_ref, wup_ref, cw_ref, cb_ref, wd_ref, gate_ref, gfin_ref,
                y_ref, hext, act):
    tm = x1_ref.shape[0]
    n = tm + 2 * HALO
    i = pl.program_id(0)
    hext[0:HALO, :] = jnp.where(i == 0, jnp.zeros_like(hp_ref), hp_ref[...])
    hext[HALO:HALO + tm, :] = hm_ref[...]
    hext[HALO + tm:n, :] = jnp.where(i == pl.num_programs(0) - 1, jnp.zeros_like(hn_ref), hn_ref[...])

    def conv(cols):
        p = jnp.dot(hext[...], wup_ref[:, cols], preferred_element_type=F32)
        prev = pltpu.roll(p, 1, 0)[HALO:HALO + tm]
        nxt = pltpu.roll(p, n - 1, 0)[HALO:HALO + tm]
        return (cw_ref[0:1, cols] * prev + cw_ref[1:2, cols] * p[HALO:HALO + tm] + cw_ref[2:3, cols] * nxt
                + cb_ref[:, cols])

    for c in range(D_FF // TF_FFN):
        val = slice(c * TF_FFN, (c + 1) * TF_FFN)
        gate = slice(D_FF + c * TF_FFN, D_FF + (c + 1) * TF_FFN)
        act[:, val] = (jax.nn.gelu(conv(gate), approximate=True) * conv(val)).astype(BF16)

    x2 = x1_ref[...] + gate_ref[...] * jnp.dot(act[...], wd_ref[...], preferred_element_type=F32)
    y = x2 * lax.rsqrt(jnp.mean(x2 * x2, axis=-1, keepdims=True) + RMS_EPS)
    y_ref[...] = y * gfin_ref[...]


def _ffn(x1, h2, w_up, conv_w, conv_b, w_down, gate2, g_final):
    tm = TM_FFN
    nt = SEQ // tm
    per = tm // HALO
    nhalo = SEQ // HALO
    row = lambda: pl.BlockSpec((tm, D_MODEL), lambda i: (i, 0))
    const = lambda a: pl.BlockSpec(a.shape, lambda i: (0, 0), pipeline_mode=pl.Buffered(1))
    return pl.pallas_call(
        _ffn_kernel,
        grid=(nt,),
        in_specs=[
            row(), row(),
            pl.BlockSpec((HALO, D_MODEL), lambda i: (jnp.maximum(i * per - 1, 0), 0)),
            pl.BlockSpec((HALO, D_MODEL), lambda i: (jnp.minimum((i + 1) * per, nhalo - 1), 0)),
            const(w_up), const(conv_w), const(conv_b), const(w_down), const(gate2), const(g_final),
        ],
        out_specs=row(),
        out_shape=jax.ShapeDtypeStruct((SEQ, D_MODEL), F32),
        scratch_shapes=[pltpu.VMEM((tm + 2 * HALO, D_MODEL), BF16), pltpu.VMEM((tm, D_FF), BF16)],
        compiler_params=_params(("arbitrary",)),
        name="ffn",
    )(x1, h2, h2, h2, w_up, conv_w, conv_b, w_down, gate2, g_final)


def kernel(x, c, w_ada, b_ada, g_mix, w_in, rel_bias, na_rpb, w_branch_dil, w_branch_na, w_out, g_ffn,
           w_up, conv_w, conv_b, w_down, g_final):
    assert x.shape == (1, SEQ, D_MODEL) and w_ada.shape[0] == 1
    xs = x[0]
    ada = _ada(c, w_ada[0], b_ada[0])
    shift1, scale1, gate1, shift2, scale2, gate2 = [ada[:, k * D_MODEL:(k + 1) * D_MODEL] for k in range(6)]
    mod1 = jnp.concatenate([g_mix[0][None], scale1, shift1], axis=0)
    mod2 = jnp.concatenate([g_ffn[0][None], scale2, shift2], axis=0)

    q_scale = np.ones((IN_COLS,), np.float32)
    q_scale[0:WIDTH] = HEAD_DIM ** -0.5 * LOG2E
    q_scale[3 * WIDTH:4 * WIDTH] = HEAD_DIM ** -0.5 * LOG2E
    w_in_b = (w_in[0] * q_scale).astype(BF16)

    a_nat, a_m4, a_m16, b_qkv, gates = _inproj(xs, mod1, w_in_b)

    outs = []
    for d, arr in zip(DILATIONS, (a_nat, a_m4, a_m16)):
        outs.append(_band_attention(arr.reshape(SEQ, 3 * WIDTH), _band_bias(rel_bias, d), SEQ // d))
    (o1, l1), (o4, l4), (o16, l16) = outs
    ob, wbd, wbn, wout, wup, wdown = _na_attention(
        b_qkv, _na_bias(na_rpb[0]), (w_branch_dil[0], w_branch_na[0], w_out[0], w_up[0], w_down[0]))

    x1, h2 = _merge(xs, o1, o4, o16, l1, l4, l16, ob, gates, wbd, wbn, wout, gate1, mod2)
    y = _ffn(x1, h2, wup, conv_w[0], conv_b[0].reshape(1, 2 * D_FF), wdown, gate2, g_final.reshape(1, D_MODEL))
    return y[None]
```

```python
import functools
import math

import numpy as np
import jax
import jax.numpy as jnp
from jax import lax
from jax.experimental import pallas as pl
from jax.experimental.pallas import tpu as pltpu

F32 = jnp.float32
BF16 = jnp.bfloat16

D_MODEL = 1024
SEQ = 16384
GRID_W = 64
ROWS = SEQ // GRID_W
HEAD_DIM = 64
N_HEADS = 8
WIDTH = N_HEADS * HEAD_DIM
DILATIONS = (1, 4, 16)
BAND = 64
NA_KH = 8
NA_COLS = 16
NA_ROWS_MAX = 8
T5_BUCKETS = 32
T5_MAX_DISTANCE = 1024
D_FF = 2816
RMS_EPS = 1e-6
IN_COLS = 6 * WIDTH + 2 * D_MODEL
NEG = -1e30
LOG2E = math.log2(math.e)

VMEM_LIMIT = 56 * 1024 * 1024

TM_IN = 512
TQ_BAND = 1024
SUB_BAND = 128
TM_MERGE = 512
SUB_MERGE = 256
TM_FFN = 1024
TF_FFN = 256
HALO = 16
LANES = 128
NA_AHEAD = 3
NA_TILE = 16
BAND_AHEAD = 3


def _params(sem):
    return pltpu.CompilerParams(dimension_semantics=sem, vmem_limit_bytes=VMEM_LIMIT)


def _ada_kernel(c_ref, w_ref, b_ref, o_ref):
    c = c_ref[...]
    s = c / (1.0 + jnp.exp(-c))
    o_ref[...] = jnp.sum(w_ref[...] * s, axis=0, keepdims=True) + b_ref[...]


def _ada(c, w_ada, b_ada):
    n_out = w_ada.shape[1]
    tn = D_MODEL
    return pl.pallas_call(
        _ada_kernel,
        grid=(n_out // tn,),
        in_specs=[
            pl.BlockSpec((D_MODEL, 1), lambda j: (0, 0)),
            pl.BlockSpec((D_MODEL, tn), lambda j: (0, j)),
            pl.BlockSpec((1, tn), lambda j: (0, j)),
        ],
        out_specs=pl.BlockSpec((1, tn), lambda j: (0, j)),
        out_shape=jax.ShapeDtypeStruct((1, n_out), F32),
        compiler_params=_params(("arbitrary",)),
        name="ada",
    )(c.reshape(D_MODEL, 1), w_ada, b_ada.reshape(1, n_out))


def _rms_modulate(x, mod_ref):
    y = x * lax.rsqrt(jnp.mean(x * x, axis=-1, keepdims=True) + RMS_EPS)
    return (y * mod_ref[0:1, :]) * (1.0 + mod_ref[1:2, :]) + mod_ref[2:3, :]


def _inproj_kernel(x0_ref, xn_ref, mod_ref, w_ref, anat_ref, am4_ref, am16_ref, b_ref, g_ref,
                   h_even, h_odd, res_scr, res4_scr):
    tm = xn_ref.shape[0]
    i = pl.program_id(0)

    @pl.when(i == 0)
    def _():
        h_even[...] = _rms_modulate(x0_ref[...], mod_ref).astype(BF16)

    def project(h_cur, h_next):
        for k, n in enumerate((6, 7, 8, 9, 0, 1, 2, 3, 4, 5)):
            cols = slice(n * WIDTH, (n + 1) * WIDTH)
            r = jnp.dot(h_cur[...], w_ref[:, cols], preferred_element_type=F32)
            if k == 2:
                h_next[...] = _rms_modulate(xn_ref[...], mod_ref).astype(BF16)
            if n < 3:
                anat_ref[:, cols] = r.astype(BF16)
                for t in range(WIDTH // LANES):
                    lanes = slice(n * WIDTH + t * LANES, n * WIDTH + (t + 1) * LANES)
                    res_scr[t] = r[:, t * LANES:(t + 1) * LANES]
                    for c in range(4):
                        cls4 = res_scr[t, pl.ds(c, tm // 4, stride=4), :]
                        am4_ref[c, :, lanes] = cls4.astype(BF16)
                        res4_scr[t, c] = cls4
                        for c2 in range(4):
                            am16_ref[c + 4 * c2, :, lanes] = (
                                res4_scr[t, c, pl.ds(c2, tm // 16, stride=4), :].astype(BF16))
            elif n < 6:
                b_ref[:, (n - 3) * WIDTH:(n - 2) * WIDTH] = r.astype(BF16)
            else:
                g_ref[:, (n - 6) * WIDTH:(n - 5) * WIDTH] = (1.0 / (1.0 + jnp.exp(-r))).astype(BF16)

    @pl.when(i % 2 == 0)
    def _():
        project(h_even, h_odd)

    @pl.when(i % 2 == 1)
    def _():
        project(h_odd, h_even)


def _inproj(x, mod1, w_in):
    tm = TM_IN
    nt = SEQ // tm
    qkv = 3 * WIDTH
    return pl.pallas_call(
        _inproj_kernel,
        grid=(nt,),
        in_specs=[
            pl.BlockSpec((tm, D_MODEL), lambda i: (0, 0)),
            pl.BlockSpec((tm, D_MODEL), lambda i: (jnp.minimum(i + 1, nt - 1), 0)),
            pl.BlockSpec((3, D_MODEL), lambda i: (0, 0)),
            pl.BlockSpec((D_MODEL, IN_COLS), lambda i: (0, 0), pipeline_mode=pl.Buffered(1)),
        ],
        out_specs=[
            pl.BlockSpec((tm, qkv), lambda i: (i, 0)),
            pl.BlockSpec((4, tm // 4, qkv), lambda i: (0, i, 0)),
            pl.BlockSpec((16, tm // 16, qkv), lambda i: (0, i, 0)),
            pl.BlockSpec((tm, qkv), lambda i: (i, 0)),
            pl.BlockSpec((tm, 2 * D_MODEL), lambda i: (i, 0)),
        ],
        out_shape=[
            jax.ShapeDtypeStruct((SEQ, qkv), BF16),
            jax.ShapeDtypeStruct((4, SEQ // 4, qkv), BF16),
            jax.ShapeDtypeStruct((16, SEQ // 16, qkv), BF16),
            jax.ShapeDtypeStruct((SEQ, qkv), BF16),
            jax.ShapeDtypeStruct((SEQ, 2 * D_MODEL), BF16),
        ],
        scratch_shapes=[pltpu.VMEM((tm, D_MODEL), BF16), pltpu.VMEM((tm, D_MODEL), BF16),
                        pltpu.VMEM((WIDTH // LANES, tm, LANES), F32),
                        pltpu.VMEM((WIDTH // LANES, 4, tm // 4, LANES), F32)],
        compiler_params=_params(("arbitrary",)),
        name="inproj",
    )(x, x, mod1, w_in)


def _stack_head_pair(q):
    first = lax.broadcasted_iota(jnp.int32, q.shape, 1) < HEAD_DIM
    qf = q.astype(F32)
    return jnp.concatenate([jnp.where(first, qf, 0.0), jnp.where(first, 0.0, qf)], axis=0).astype(q.dtype)


def _unstack_head_pair(o):
    n = o.shape[0] // 2
    first = lax.broadcasted_iota(jnp.int32, (n, o.shape[1]), 1) < HEAD_DIM
    return jnp.where(first, o[0:n], o[n:2 * n])


def _fill_value_ones(vext, first_step):
    @pl.when(first_step)
    def _():
        vext[:, :, LANES:2 * LANES] = jnp.ones(vext.shape[:2] + (LANES,), vext.dtype)


def _band_kernel(q_ref, kl_ref, km_ref, kr_ref, vl_ref, vm_ref, vr_ref, bias_ref,
                 o_ref, lse_ref, kcat, vext, *, seg_len):
    tq = q_ref.shape[0]
    sub = SUB_BAND
    nsub = tq // sub
    nk = sub + 2 * BAND
    start = pl.program_id(0) * tq
    kcat[0:BAND, :] = kl_ref[...]
    kcat[BAND:BAND + tq, :] = km_ref[...]
    kcat[BAND + tq:tq + 2 * BAND, :] = kr_ref[...]
    _fill_value_ones(vext, pl.program_id(0) == 0)
    for pr in range(N_HEADS // 2):
        ps = slice(pr * LANES, (pr + 1) * LANES)
        vext[pr, 0:BAND, 0:LANES] = vl_ref[:, ps]
        vext[pr, BAND:BAND + tq, 0:LANES] = vm_ref[:, ps]
        vext[pr, BAND + tq:tq + 2 * BAND, 0:LANES] = vr_ref[:, ps]
    col = lax.broadcasted_iota(jnp.int32, (1, nk), 1)
    edge = {0: jnp.where((col < BAND) & ((start % seg_len) == 0), NEG, 0.0),
            nsub - 1: jnp.where((col >= BAND + sub) & (((start + tq) % seg_len) == 0), NEG, 0.0)}
    lse_ref[...] = jnp.zeros_like(lse_ref)

    def scores(j, pr):
        ps = slice(pr * LANES, (pr + 1) * LANES)
        s = lax.dot_general(_stack_head_pair(q_ref[j * sub:(j + 1) * sub, ps]), kcat[j * sub:j * sub + nk, ps],
                            (((1,), (1,)), ((), ())), preferred_element_type=F32)
        s = s + bias_ref[pr]
        return s + edge[j] if j in edge else s

    def finish(j, pr, s):
        ps = slice(pr * LANES, (pr + 1) * LANES)
        rows = slice(j * sub, (j + 1) * sub)
        m = jnp.max(s, axis=-1, keepdims=True)
        p = jnp.exp2(s - m)
        ol = jnp.dot(p.astype(BF16), vext[pr, j * sub:j * sub + nk, :], preferred_element_type=F32)
        o = ol[:, 0:LANES] / ol[:, LANES:2 * LANES]
        o_ref[rows, ps] = _unstack_head_pair(o).astype(o_ref.dtype)
        lse = m + jnp.log2(ol[:, LANES:2 * LANES])
        lse_ref[rows, 2 * pr:2 * pr + 1] = lse[0:sub, 2 * pr:2 * pr + 1]
        lse_ref[rows, 2 * pr + 1:2 * pr + 2] = lse[sub:2 * sub, 2 * pr + 1:2 * pr + 2]

    pending = []
    for j in range(nsub):
        for pr in range(N_HEADS // 2):
            pending.append((j, pr, scores(j, pr)))
            if len(pending) > BAND_AHEAD:
                finish(*pending.pop(0))
    for unit in pending:
        finish(*unit)


def _band_attention(qkv, bias, seg_len):
    tq = TQ_BAND
    nt = SEQ // tq
    per = tq // BAND
    nhalo = SEQ // BAND
    left = lambda c: (lambda i: (jnp.maximum(i * per - 1, 0), c))
    right = lambda c: (lambda i: (jnp.minimum((i + 1) * per, nhalo - 1), c))
    main = lambda c: (lambda i: (i, c))
    kv_specs = []
    for c in (1, 2):
        kv_specs += [pl.BlockSpec((BAND, WIDTH), left(c)), pl.BlockSpec((tq, WIDTH), main(c)),
                     pl.BlockSpec((BAND, WIDTH), right(c))]
    nk = tq + 2 * BAND
    return pl.pallas_call(
        functools.partial(_band_kernel, seg_len=seg_len),
        grid=(nt,),
        in_specs=[pl.BlockSpec((tq, WIDTH), main(0))] + kv_specs
        + [pl.BlockSpec((N_HEADS // 2, 2 * SUB_BAND, SUB_BAND + 2 * BAND), lambda i: (0, 0, 0))],
        out_specs=[pl.BlockSpec((tq, WIDTH), lambda i: (i, 0)),
                   pl.BlockSpec((tq, LANES), lambda i: (i, 0))],
        out_shape=[jax.ShapeDtypeStruct((SEQ, WIDTH), BF16),
                   jax.ShapeDtypeStruct((SEQ, LANES), F32)],
        scratch_shapes=[pltpu.VMEM((nk, WIDTH), BF16), pltpu.VMEM((N_HEADS // 2, nk, 2 * LANES), BF16)],
        compiler_params=_params(("arbitrary",)),
        name=f"band_l{seg_len}",
    )(qkv, qkv, qkv, qkv, qkv, qkv, qkv, bias.reshape(N_HEADS // 2, 2 * SUB_BAND, SUB_BAND + 2 * BAND))


def _t5_bucket(rel):
    n = T5_BUCKETS // 2
    max_exact = n // 2
    sign_part = jnp.where(rel > 0, n, 0)
    a = jnp.abs(rel)
    af = jnp.maximum(a, 1).astype(F32)
    large = max_exact + (jnp.log(af / max_exact) / math.log(T5_MAX_DISTANCE / max_exact)
                         * (n - max_exact)).astype(jnp.int32)
    large = jnp.minimum(large, n - 1)
    return sign_part + jnp.where(a < max_exact, a, large)


def _band_bias(rel_bias, dilation):
    tq = SUB_BAND
    delta = jnp.arange(tq + 2 * BAND)[None, :] - BAND - jnp.arange(tq)[:, None]
    b = _lookup(rel_bias.astype(F32).T, _t5_bucket(delta * dilation), T5_BUCKETS)
    return jnp.where((jnp.abs(delta) <= BAND)[None], b * LOG2E, NEG)


def _lookup(table, idx, n):
    onehot = (idx[..., None] == jnp.arange(n)).astype(F32)
    return jnp.einsum("...b,xyb->...xy", table, onehot, precision=lax.Precision.HIGHEST)


def _na_kernel(q_ref, kp_ref, kc_ref, kn_ref, vp_ref, vc_ref, vn_ref, btab_ref, *rest):
    n_w = (len(rest) - 4) // 2
    w_in_refs, o_ref, w_out_refs = rest[:n_w], rest[n_w], rest[n_w + 1:2 * n_w + 1]
    kcat, vext, bias_ref = rest[2 * n_w + 1:]
    for src, dst in zip(w_in_refs, w_out_refs):
        dst[...] = src[...].astype(dst.dtype)
    tile = q_ref.shape[0]
    halo = NA_KH * GRID_W
    i = pl.program_id(0)

    @pl.when(i == 0)
    def _():
        for v in range(NA_KH):
            for h in range(N_HEADS):
                for k in range(NA_KH // 2):
                    bias_ref[v, h // 2, (h % 2) * GRID_W:(h % 2 + 1) * GRID_W, k * LANES:(k + 1) * LANES] = (
                        btab_ref[h, v + 2 * k])
    _fill_value_ones(vext, i == 0)
    for rows, k_ref, v_ref in ((slice(0, halo), kp_ref, vp_ref), (slice(halo, halo + tile), kc_ref, vc_ref),
                               (slice(halo + tile, 2 * halo + tile), kn_ref, vn_ref)):
        kcat[rows, :] = k_ref[...]
        for pr in range(N_HEADS // 2):
            vext[pr, rows, 0:LANES] = v_ref[:, pr * LANES:(pr + 1) * LANES]
    nkeys = NA_KH * GRID_W

    def window(t):
        r = i * NA_TILE + t
        rs = jnp.clip(r - NA_KH // 2, 0, ROWS - NA_KH)
        return pl.multiple_of((rs - (i * NA_TILE - NA_KH)) * GRID_W, GRID_W), rs - r + (NA_KH - 1)

    def scores(t, pr, off, var):
        ps = slice(pr * LANES, (pr + 1) * LANES)
        s = lax.dot_general(_stack_head_pair(q_ref[t * GRID_W:(t + 1) * GRID_W, ps]), kcat[pl.ds(off, nkeys), ps],
                            (((1,), (1,)), ((), ())), preferred_element_type=F32)
        return s + bias_ref[var, pr]

    def finish(t, pr, off, s):
        ps = slice(pr * LANES, (pr + 1) * LANES)
        m = jnp.max(s, axis=-1, keepdims=True)
        p = jnp.exp2(s - m)
        ol = jnp.dot(p.astype(BF16), vext[pr, pl.ds(off, nkeys), :], preferred_element_type=F32)
        o = ol[:, 0:LANES] / ol[:, LANES:2 * LANES]
        o_ref[t * GRID_W:(t + 1) * GRID_W, ps] = _unstack_head_pair(o).astype(o_ref.dtype)

    units = [(t, pr) for t in range(NA_TILE) for pr in range(N_HEADS // 2)]
    wins = [window(t) for t in range(NA_TILE)]
    pending = []
    for t, pr in units:
        off, var = wins[t]
        pending.append((t, pr, off, scores(t, pr, off, var)))
        if len(pending) > NA_AHEAD:
            finish(*pending.pop(0))
    for unit in pending:
        finish(*unit)


def _na_attention(qkv, btab, weights):
    tile = NA_TILE * GRID_W
    halo = NA_KH * GRID_W
    nt = SEQ // tile
    per = tile // halo
    w_specs = []
    for w in weights:
        rows = next(r for r in (w.shape[0] // nt, LANES) if r % HALO == 0 and w.shape[0] % r == 0)
        nblk = w.shape[0] // rows
        assert nblk <= nt
        w_specs.append(pl.BlockSpec((rows, w.shape[1]), lambda i, nblk=nblk: (jnp.minimum(i, nblk - 1), 0)))
    prev = lambda c: (lambda i: (jnp.maximum(i * per - 1, 0), c))
    cur = lambda c: (lambda i: (i, c))
    nxt = lambda c: (lambda i: (jnp.minimum((i + 1) * per, SEQ // halo - 1), c))
    kv_specs = [pl.BlockSpec((rows, WIDTH), f(c)) for c in (1, 2)
                for rows, f in ((halo, prev), (tile, cur), (halo, nxt))]
    return pl.pallas_call(
        _na_kernel,
        grid=(nt,),
        in_specs=[pl.BlockSpec((tile, WIDTH), cur(0))] + kv_specs
        + [pl.BlockSpec(btab.shape, lambda i: (0, 0, 0, 0), pipeline_mode=pl.Buffered(1))] + w_specs,
        out_specs=[pl.BlockSpec((tile, WIDTH), lambda i: (i, 0))] + w_specs,
        out_shape=[jax.ShapeDtypeStruct((SEQ, WIDTH), BF16)]
        + [jax.ShapeDtypeStruct(w.shape, BF16) for w in weights],
        scratch_shapes=[pltpu.VMEM((tile + 2 * halo, WIDTH), BF16),
                        pltpu.VMEM((N_HEADS // 2, tile + 2 * halo, 2 * LANES), BF16),
                        pltpu.VMEM((NA_KH, N_HEADS // 2, 2 * GRID_W, NA_KH * GRID_W), F32)],
        compiler_params=_params(("arbitrary",)),
        name="na",
    )(qkv, qkv, qkv, qkv, qkv, qkv, qkv, btab, *weights)


def _na_bias(rpb):
    cq = jnp.arange(GRID_W)
    col_start = jnp.clip(cq - NA_COLS // 2, 0, GRID_W - NA_COLS)
    col_mask = (cq[None, :] >= col_start[:, None]) & (cq[None, :] < col_start[:, None] + NA_COLS)
    dc = jnp.clip(cq[None, :] - cq[:, None], -(NA_COLS - 1), NA_COLS - 1) + NA_COLS - 1
    e = _lookup(rpb.astype(F32), dc, 2 * NA_COLS - 1)
    e = jnp.where(col_mask[None, None], e * LOG2E, NEG)
    return jnp.concatenate([e[:, :-1], e[:, 1:]], axis=-1)


def _expand_heads(w, exp_ref):
    hi = w.astype(BF16)
    lo = (w - hi.astype(F32)).astype(BF16)
    return (jnp.dot(hi, exp_ref[...], preferred_element_type=F32)
            + jnp.dot(lo, exp_ref[...], preferred_element_type=F32))


def _merge_kernel(x_ref, o1_ref, o4_ref, o16_ref, l1_ref, l4_ref, l16_ref, ob_ref, g_ref,
                  wbd_ref, wbn_ref, wout_ref, exp_ref, gate_ref, mod_ref,
                  x1_ref, h2_ref, o4n, o16n, o16j, l4n, l16n, l16j, oa_scr):
    tm = x_ref.shape[0]
    for src4, src16, n4, j16, n16 in ((o4_ref, o16_ref, o4n, o16j, o16n), (l4_ref, l16_ref, l4n, l16j, l16n)):
        for t in range(n4.shape[0]):
            ls = slice(t * LANES, (t + 1) * LANES)
            for c in range(4):
                n4[t, pl.ds(c, tm // 4, stride=4), :] = src4[c, :, ls].astype(F32)
                for c2 in range(4):
                    j16[t, c, pl.ds(c2, tm // 16, stride=4), :] = src16[c + 4 * c2, :, ls].astype(F32)
                n16[t, pl.ds(c, tm // 4, stride=4), :] = j16[t, c]

    for sb in range(tm // SUB_MERGE):
        rows = slice(sb * SUB_MERGE, (sb + 1) * SUB_MERGE)
        l1, l4, l16 = l1_ref[rows, :], l4n[0, rows, :], l16n[0, rows, :]
        mx = jnp.maximum(jnp.maximum(l1, l4), l16)
        e1, e4, e16 = jnp.exp2(l1 - mx), jnp.exp2(l4 - mx), jnp.exp2(l16 - mx)
        den = e1 + e4 + e16
        w1, w4, w16 = (_expand_heads(e / den, exp_ref) for e in (e1, e4, e16))
        for t in range(WIDTH // LANES):
            ls = slice(t * LANES, (t + 1) * LANES)
            oa = w1[:, ls] * o1_ref[rows, ls] + w4[:, ls] * o4n[t, rows, :] + w16[:, ls] * o16n[t, rows, :]
            oa_scr[sb, :, ls] = oa.astype(BF16)
        ya = jnp.dot(oa_scr[sb], wbd_ref[...], preferred_element_type=F32)
        yb = jnp.dot(ob_ref[rows, :], wbn_ref[...], preferred_element_type=F32)
        merged = (g_ref[rows, 0:D_MODEL].astype(F32) * ya + g_ref[rows, D_MODEL:2 * D_MODEL].astype(F32) * yb)
        z = jnp.dot(merged.astype(BF16), wout_ref[...], preferred_element_type=F32)
        x1 = x_ref[rows, :] + gate_ref[...] * z
        x1_ref[rows, :] = x1
        h2_ref[rows, :] = _rms_modulate(x1, mod_ref).astype(BF16)


def _merge(x, o1, o4, o16, l1, l4, l16, ob, g, wbd, wbn, wout, gate1, mod2):
    tm = TM_MERGE
    nt = SEQ // tm
    row = lambda w: pl.BlockSpec((tm, w), lambda i: (i, 0))
    cls = lambda d, w: pl.BlockSpec((d, tm // d, w), lambda i: (0, i, 0))
    const = lambda shape: pl.BlockSpec(shape, lambda i: (0, 0))
    spread = (jnp.arange(LANES)[:, None] == jnp.arange(WIDTH)[None, :] // HEAD_DIM).astype(BF16)
    return pl.pallas_call(
        _merge_kernel,
        grid=(nt,),
        in_specs=[row(D_MODEL), row(WIDTH), cls(4, WIDTH), cls(16, WIDTH),
                  row(LANES), cls(4, LANES), cls(16, LANES), row(WIDTH), row(2 * D_MODEL),
                  const((WIDTH, D_MODEL)), const((WIDTH, D_MODEL)), const((D_MODEL, D_MODEL)),
                  const((LANES, WIDTH)), const((1, D_MODEL)), const((3, D_MODEL))],
        out_specs=[row(D_MODEL), row(D_MODEL)],
        out_shape=[jax.ShapeDtypeStruct((SEQ, D_MODEL), F32), jax.ShapeDtypeStruct((SEQ, D_MODEL), BF16)],
        scratch_shapes=[pltpu.VMEM((WIDTH // LANES, tm, LANES), F32), pltpu.VMEM((WIDTH // LANES, tm, LANES), F32),
                        pltpu.VMEM((WIDTH // LANES, 4, tm // 4, LANES), F32),
                        pltpu.VMEM((1, tm, LANES), F32), pltpu.VMEM((1, tm, LANES), F32),
                        pltpu.VMEM((1, 4, tm // 4, LANES), F32),
                        pltpu.VMEM((tm // SUB_MERGE, SUB_MERGE, WIDTH), BF16)],
        compiler_params=_params(("arbitrary",)),
        name="merge",
    )(x, o1, o4.reshape(4, SEQ // 4, WIDTH), o16.reshape(16, SEQ // 16, WIDTH),
      l1, l4.reshape(4, SEQ // 4, LANES), l16.reshape(16, SEQ // 16, LANES), ob, g,
      wbd, wbn, wout, spread, gate1, mod2)


def _ffn_kernel(x1_ref, hm_ref, hp_ref, hn_ref, wup_ref, cw_ref, cb_ref, wd_ref, gate_ref, gfin_ref,
                y_ref, hext, act):
    tm = x1_ref.shape[0]
    n = tm + 2 * HALO
    i = pl.program_id(0)
    hext[0:HALO, :] = jnp.where(i == 0, jnp.zeros_like(hp_ref), hp_ref[...])
    hext[HALO:HALO + tm, :] = hm_ref[...]
    hext[HALO + tm:n, :] = jnp.where(i == pl.num_programs(0) - 1, jnp.zeros_like(hn_ref), hn_ref[...])

    def conv(cols):
        p = jnp.dot(hext[...], wup_ref[:, cols], preferred_element_type=F32)
        prev = pltpu.roll(p, 1, 0)[HALO:HALO + tm]
        nxt = pltpu.roll(p, n - 1, 0)[HALO:HALO + tm]
        return (cw_ref[0:1, cols] * prev + cw_ref[1:2, cols] * p[HALO:HALO + tm] + cw_ref[2:3, cols] * nxt
                + cb_ref[:, cols])

    for c in range(D_FF // TF_FFN):
        val = slice(c * TF_FFN, (c + 1) * TF_FFN)
        gate = slice(D_FF + c * TF_FFN, D_FF + (c + 1) * TF_FFN)
        act[:, val] = (jax.nn.gelu(conv(gate), approximate=True) * conv(val)).astype(BF16)

    x2 = x1_ref[...] + gate_ref[...] * jnp.dot(act[...], wd_ref[...], preferred_element_type=F32)
    y = x2 * lax.rsqrt(jnp.mean(x2 * x2, axis=-1, keepdims=True) + RMS_EPS)
    y_ref[...] = y * gfin_ref[...]


def _ffn(x1, h2, w_up, conv_w, conv_b, w_down, gate2, g_final):
    tm = TM_FFN
    nt = SEQ // tm
    per = tm // HALO
    nhalo = SEQ // HALO
    row = lambda: pl.BlockSpec((tm, D_MODEL), lambda i: (i, 0))
    const = lambda a: pl.BlockSpec(a.shape, lambda i: (0, 0), pipeline_mode=pl.Buffered(1))
    return pl.pallas_call(
        _ffn_kernel,
        grid=(nt,),
        in_specs=[
            row(), row(),
            pl.BlockSpec((HALO, D_MODEL), lambda i: (jnp.maximum(i * per - 1, 0), 0)),
            pl.BlockSpec((HALO, D_MODEL), lambda i: (jnp.minimum((i + 1) * per, nhalo - 1), 0)),
            const(w_up), const(conv_w), const(conv_b), const(w_down), const(gate2), const(g_final),
        ],
        out_specs=row(),
        out_shape=jax.ShapeDtypeStruct((SEQ, D_MODEL), F32),
        scratch_shapes=[pltpu.VMEM((tm + 2 * HALO, D_MODEL), BF16), pltpu.VMEM((tm, D_FF), BF16)],
        compiler_params=_params(("arbitrary",)),
        name="ffn",
    )(x1, h2, h2, h2, w_up, conv_w, conv_b, w_down, gate2, g_final)


def kernel(x, c, w_ada, b_ada, g_mix, w_in, rel_bias, na_rpb, w_branch_dil, w_branch_na, w_out, g_ffn,
           w_up, conv_w, conv_b, w_down, g_final):
    assert x.shape == (1, SEQ, D_MODEL) and w_ada.shape[0] == 1
    xs = x[0]
    ada = _ada(c, w_ada[0], b_ada[0])
    shift1, scale1, gate1, shift2, scale2, gate2 = [ada[:, k * D_MODEL:(k + 1) * D_MODEL] for k in range(6)]
    mod1 = jnp.concatenate([g_mix[0][None], scale1, shift1], axis=0)
    mod2 = jnp.concatenate([g_ffn[0][None], scale2, shift2], axis=0)

    q_scale = np.ones((IN_COLS,), np.float32)
    q_scale[0:WIDTH] = HEAD_DIM ** -0.5 * LOG2E
    q_scale[3 * WIDTH:4 * WIDTH] = HEAD_DIM ** -0.5 * LOG2E
    w_in_b = (w_in[0] * q_scale).astype(BF16)

    a_nat, a_m4, a_m16, b_qkv, gates = _inproj(xs, mod1, w_in_b)

    outs = []
    for d, arr in zip(DILATIONS, (a_nat, a_m4, a_m16)):
        outs.append(_band_attention(arr.reshape(SEQ, 3 * WIDTH), _band_bias(rel_bias, d), SEQ // d))
    (o1, l1), (o4, l4), (o16, l16) = outs
    ob, wbd, wbn, wout, wup, wdown = _na_attention(
        b_qkv, _na_bias(na_rpb[0]), (w_branch_dil[0], w_branch_na[0], w_out[0], w_up[0], w_down[0]))

    x1, h2 = _merge(xs, o1, o4, o16, l1, l4, l16, ob, gates, wbd, wbn, wout, gate1, mod2)
    y = _ffn(x1, h2, wup, conv_w[0], conv_b[0].reshape(1, 2 * D_FF), wdown, gate2, g_final.reshape(1, D_MODEL))
    return y[None]
```

```python
import functools
import math

import numpy as np
import jax
import jax.numpy as jnp
from jax import lax
from jax.experimental import pallas as pl
from jax.experimental.pallas import tpu as pltpu

F32 = jnp.float32
BF16 = jnp.bfloat16

D_MODEL = 1024
SEQ = 16384
GRID_W = 64
ROWS = SEQ // GRID_W
HEAD_DIM = 64
N_HEADS = 8
WIDTH = N_HEADS * HEAD_DIM
DILATIONS = (1, 4, 16)
BAND = 64
NA_KH = 8
NA_COLS = 16
NA_ROWS_MAX = 8
T5_BUCKETS = 32
T5_MAX_DISTANCE = 1024
D_FF = 2816
RMS_EPS = 1e-6
IN_COLS = 6 * WIDTH + 2 * D_MODEL
NEG = -1e30
LOG2E = math.log2(math.e)

VMEM_LIMIT = 56 * 1024 * 1024

TM_IN = 512
TQ_BAND = 1024
SUB_BAND = 128
TM_MERGE = 1024
SUB_MERGE = 256
TM_FFN = 1024
TF_FFN = 256
HALO = 16
LANES = 128
NA_AHEAD = 3
NA_TILE = 16
BAND_AHEAD = 3


def _params(sem):
    return pltpu.CompilerParams(dimension_semantics=sem, vmem_limit_bytes=VMEM_LIMIT)


def _ada_kernel(c_ref, w_ref, b_ref, o_ref):
    c = c_ref[...]
    s = c / (1.0 + jnp.exp(-c))
    o_ref[...] = jnp.sum(w_ref[...] * s, axis=0, keepdims=True) + b_ref[...]


def _ada(c, w_ada, b_ada):
    n_out = w_ada.shape[1]
    tn = D_MODEL
    return pl.pallas_call(
        _ada_kernel,
        grid=(n_out // tn,),
        in_specs=[
            pl.BlockSpec((D_MODEL, 1), lambda j: (0, 0)),
            pl.BlockSpec((D_MODEL, tn), lambda j: (0, j)),
            pl.BlockSpec((1, tn), lambda j: (0, j)),
        ],
        out_specs=pl.BlockSpec((1, tn), lambda j: (0, j)),
        out_shape=jax.ShapeDtypeStruct((1, n_out), F32),
        compiler_params=_params(("arbitrary",)),
        name="ada",
    )(c.reshape(D_MODEL, 1), w_ada, b_ada.reshape(1, n_out))


def _rms_modulate(x, mod_ref):
    y = x * lax.rsqrt(jnp.mean(x * x, axis=-1, keepdims=True) + RMS_EPS)
    return (y * mod_ref[0:1, :]) * (1.0 + mod_ref[1:2, :]) + mod_ref[2:3, :]


def _inproj_kernel(x0_ref, xn_ref, mod_ref, w_ref, anat_ref, am4_ref, am16_ref, b_ref, g_ref,
                   h_even, h_odd, res_scr, res4_scr):
    tm = xn_ref.shape[0]
    i = pl.program_id(0)

    @pl.when(i == 0)
    def _():
        h_even[...] = _rms_modulate(x0_ref[...], mod_ref).astype(BF16)

    def project(h_cur, h_next):
        for k, n in enumerate((6, 7, 8, 9, 0, 1, 2, 3, 4, 5)):
            cols = slice(n * WIDTH, (n + 1) * WIDTH)
            r = jnp.dot(h_cur[...], w_ref[:, cols], preferred_element_type=F32)
            if k == 2:
                h_next[...] = _rms_modulate(xn_ref[...], mod_ref).astype(BF16)
            if n < 3:
                anat_ref[:, cols] = r.astype(BF16)
                for t in range(WIDTH // LANES):
                    lanes = slice(n * WIDTH + t * LANES, n * WIDTH + (t + 1) * LANES)
                    res_scr[t] = r[:, t * LANES:(t + 1) * LANES]
                    for c in range(4):
                        cls4 = res_scr[t, pl.ds(c, tm // 4, stride=4), :]
                        am4_ref[c, :, lanes] = cls4.astype(BF16)
                        res4_scr[t, c] = cls4
                        for c2 in range(4):
                            am16_ref[c + 4 * c2, :, lanes] = (
                                res4_scr[t, c, pl.ds(c2, tm // 16, stride=4), :].astype(BF16))
            elif n < 6:
                b_ref[:, (n - 3) * WIDTH:(n - 2) * WIDTH] = r.astype(BF16)
            else:
                g_ref[:, (n - 6) * WIDTH:(n - 5) * WIDTH] = (1.0 / (1.0 + jnp.exp(-r))).astype(BF16)

    @pl.when(i % 2 == 0)
    def _():
        project(h_even, h_odd)

    @pl.when(i % 2 == 1)
    def _():
        project(h_odd, h_even)


def _inproj(x, mod1, w_in):
    tm = TM_IN
    nt = SEQ // tm
    qkv = 3 * WIDTH
    return pl.pallas_call(
        _inproj_kernel,
        grid=(nt,),
        in_specs=[
            pl.BlockSpec((tm, D_MODEL), lambda i: (0, 0)),
            pl.BlockSpec((tm, D_MODEL), lambda i: (jnp.minimum(i + 1, nt - 1), 0)),
            pl.BlockSpec((3, D_MODEL), lambda i: (0, 0)),
            pl.BlockSpec((D_MODEL, IN_COLS), lambda i: (0, 0), pipeline_mode=pl.Buffered(1)),
        ],
        out_specs=[
            pl.BlockSpec((tm, qkv), lambda i: (i, 0)),
            pl.BlockSpec((4, tm // 4, qkv), lambda i: (0, i, 0)),
            pl.BlockSpec((16, tm // 16, qkv), lambda i: (0, i, 0)),
            pl.BlockSpec((tm, qkv), lambda i: (i, 0)),
            pl.BlockSpec((tm, 2 * D_MODEL), lambda i: (i, 0)),
        ],
        out_shape=[
            jax.ShapeDtypeStruct((SEQ, qkv), BF16),
            jax.ShapeDtypeStruct((4, SEQ // 4, qkv), BF16),
            jax.ShapeDtypeStruct((16, SEQ // 16, qkv), BF16),
            jax.ShapeDtypeStruct((SEQ, qkv), BF16),
            jax.ShapeDtypeStruct((SEQ, 2 * D_MODEL), BF16),
        ],
        scratch_shapes=[pltpu.VMEM((tm, D_MODEL), BF16), pltpu.VMEM((tm, D_MODEL), BF16),
                        pltpu.VMEM((WIDTH // LANES, tm, LANES), F32),
                        pltpu.VMEM((WIDTH // LANES, 4, tm // 4, LANES), F32)],
        compiler_params=_params(("arbitrary",)),
        name="inproj",
    )(x, x, mod1, w_in)


def _stack_head_pair(q):
    first = lax.broadcasted_iota(jnp.int32, q.shape, 1) < HEAD_DIM
    qf = q.astype(F32)
    return jnp.concatenate([jnp.where(first, qf, 0.0), jnp.where(first, 0.0, qf)], axis=0).astype(q.dtype)


def _unstack_head_pair(o):
    n = o.shape[0] // 2
    first = lax.broadcasted_iota(jnp.int32, (n, o.shape[1]), 1) < HEAD_DIM
    return jnp.where(first, o[0:n], o[n:2 * n])


def _fill_value_ones(vext, first_step):
    @pl.when(first_step)
    def _():
        vext[:, :, LANES:2 * LANES] = jnp.ones(vext.shape[:2] + (LANES,), vext.dtype)


def _band_kernel(q_ref, kl_ref, km_ref, kr_ref, vl_ref, vm_ref, vr_ref, bias_ref,
                 o_ref, stat_ref, kcat, vext, *, seg_len):
    tq = q_ref.shape[0]
    sub = SUB_BAND
    nsub = tq // sub
    nk = sub + 2 * BAND
    start = pl.program_id(0) * tq
    kcat[0:BAND, :] = kl_ref[...]
    kcat[BAND:BAND + tq, :] = km_ref[...]
    kcat[BAND + tq:tq + 2 * BAND, :] = kr_ref[...]
    _fill_value_ones(vext, pl.program_id(0) == 0)
    for pr in range(N_HEADS // 2):
        ps = slice(pr * LANES, (pr + 1) * LANES)
        vext[pr, 0:BAND, 0:LANES] = vl_ref[:, ps]
        vext[pr, BAND:BAND + tq, 0:LANES] = vm_ref[:, ps]
        vext[pr, BAND + tq:tq + 2 * BAND, 0:LANES] = vr_ref[:, ps]
    col = lax.broadcasted_iota(jnp.int32, (1, nk), 1)
    edge = {0: jnp.where((col < BAND) & ((start % seg_len) == 0), NEG, 0.0),
            nsub - 1: jnp.where((col >= BAND + sub) & (((start + tq) % seg_len) == 0), NEG, 0.0)}
    stat_ref[...] = jnp.zeros_like(stat_ref)

    def scores(j, pr):
        ps = slice(pr * LANES, (pr + 1) * LANES)
        s = lax.dot_general(_stack_head_pair(q_ref[j * sub:(j + 1) * sub, ps]), kcat[j * sub:j * sub + nk, ps],
                            (((1,), (1,)), ((), ())), preferred_element_type=F32)
        s = s + bias_ref[pr]
        return s + edge[j] if j in edge else s

    def finish(j, pr, s):
        ps = slice(pr * LANES, (pr + 1) * LANES)
        rows = slice(j * sub, (j + 1) * sub)
        m = jnp.max(s, axis=-1, keepdims=True)
        p = jnp.exp2(s - m)
        ol = jnp.dot(p.astype(BF16), vext[pr, j * sub:j * sub + nk, :], preferred_element_type=F32)
        o_ref[rows, ps] = _unstack_head_pair(ol[:, 0:LANES]).astype(o_ref.dtype)
        stat_ref[rows, 2 * pr:2 * pr + 1] = m[0:sub]
        stat_ref[rows, 2 * pr + 1:2 * pr + 2] = m[sub:2 * sub]
        for h, r0 in ((2 * pr, 0), (2 * pr + 1, sub)):
            lane = LANES + N_HEADS + h
            stat_ref[rows, N_HEADS + h:N_HEADS + h + 1] = ol[r0:r0 + sub, lane:lane + 1]

    pending = []
    for j in range(nsub):
        for pr in range(N_HEADS // 2):
            pending.append((j, pr, scores(j, pr)))
            if len(pending) > BAND_AHEAD:
                finish(*pending.pop(0))
    for unit in pending:
        finish(*unit)


def _band_attention(qkv, bias, seg_len):
    tq = TQ_BAND
    nt = SEQ // tq
    per = tq // BAND
    nhalo = SEQ // BAND
    left = lambda c: (lambda i: (jnp.maximum(i * per - 1, 0), c))
    right = lambda c: (lambda i: (jnp.minimum((i + 1) * per, nhalo - 1), c))
    main = lambda c: (lambda i: (i, c))
    kv_specs = []
    for c in (1, 2):
        kv_specs += [pl.BlockSpec((BAND, WIDTH), left(c)), pl.BlockSpec((tq, WIDTH), main(c)),
                     pl.BlockSpec((BAND, WIDTH), right(c))]
    nk = tq + 2 * BAND
    return pl.pallas_call(
        functools.partial(_band_kernel, seg_len=seg_len),
        grid=(nt,),
        in_specs=[pl.BlockSpec((tq, WIDTH), main(0))] + kv_specs
        + [pl.BlockSpec((N_HEADS // 2, 2 * SUB_BAND, SUB_BAND + 2 * BAND), lambda i: (0, 0, 0))],
        out_specs=[pl.BlockSpec((tq, WIDTH), lambda i: (i, 0)),
                   pl.BlockSpec((tq, LANES), lambda i: (i, 0))],
        out_shape=[jax.ShapeDtypeStruct((SEQ, WIDTH), BF16),
                   jax.ShapeDtypeStruct((SEQ, LANES), F32)],
        scratch_shapes=[pltpu.VMEM((nk, WIDTH), BF16), pltpu.VMEM((N_HEADS // 2, nk, 2 * LANES), BF16)],
        compiler_params=_params(("arbitrary",)),
        name=f"band_l{seg_len}",
    )(qkv, qkv, qkv, qkv, qkv, qkv, qkv, bias.reshape(N_HEADS // 2, 2 * SUB_BAND, SUB_BAND + 2 * BAND))


def _t5_bucket(rel):
    n = T5_BUCKETS // 2
    max_exact = n // 2
    sign_part = jnp.where(rel > 0, n, 0)
    a = jnp.abs(rel)
    af = jnp.maximum(a, 1).astype(F32)
    large = max_exact + (jnp.log(af / max_exact) / math.log(T5_MAX_DISTANCE / max_exact)
                         * (n - max_exact)).astype(jnp.int32)
    large = jnp.minimum(large, n - 1)
    return sign_part + jnp.where(a < max_exact, a, large)


def _band_bias(rel_bias, dilation):
    tq = SUB_BAND
    delta = jnp.arange(tq + 2 * BAND)[None, :] - BAND - jnp.arange(tq)[:, None]
    b = _lookup(rel_bias.astype(F32).T, _t5_bucket(delta * dilation), T5_BUCKETS)
    return jnp.where((jnp.abs(delta) <= BAND)[None], b * LOG2E, NEG)


def _lookup(table, idx, n):
    onehot = (jnp.arange(n)[:, None] == idx.reshape(1, -1)).astype(F32)
    out = jnp.dot(table.reshape(-1, n), onehot, precision=lax.Precision.HIGHEST)
    return out.reshape(table.shape[:-1] + idx.shape)


def _na_kernel(q_ref, kp_ref, kc_ref, kn_ref, vp_ref, vc_ref, vn_ref, btab_ref, *rest):
    n_w = (len(rest) - 4) // 2
    w_in_refs, o_ref, w_out_refs = rest[:n_w], rest[n_w], rest[n_w + 1:2 * n_w + 1]
    kcat, vext, bias_ref = rest[2 * n_w + 1:]
    for src, dst in zip(w_in_refs, w_out_refs):
        dst[...] = src[...].astype(dst.dtype)
    tile = q_ref.shape[0]
    halo = NA_KH * GRID_W
    i = pl.program_id(0)

    @pl.when(i == 0)
    def _():
        for v in range(NA_KH):
            for h in range(N_HEADS):
                for k in range(NA_KH // 2):
                    bias_ref[v, h // 2, (h % 2) * GRID_W:(h % 2 + 1) * GRID_W, k * LANES:(k + 1) * LANES] = (
                        btab_ref[h, v + 2 * k])
    _fill_value_ones(vext, i == 0)
    for rows, k_ref, v_ref in ((slice(0, halo), kp_ref, vp_ref), (slice(halo, halo + tile), kc_ref, vc_ref),
                               (slice(halo + tile, 2 * halo + tile), kn_ref, vn_ref)):
        kcat[rows, :] = k_ref[...]
        for pr in range(N_HEADS // 2):
            vext[pr, rows, 0:LANES] = v_ref[:, pr * LANES:(pr + 1) * LANES]
    nkeys = NA_KH * GRID_W

    def window(t):
        r = i * NA_TILE + t
        rs = jnp.clip(r - NA_KH // 2, 0, ROWS - NA_KH)
        return pl.multiple_of((rs - (i * NA_TILE - NA_KH)) * GRID_W, GRID_W), rs - r + (NA_KH - 1)

    def scores(t, pr, off, var):
        ps = slice(pr * LANES, (pr + 1) * LANES)
        s = lax.dot_general(_stack_head_pair(q_ref[t * GRID_W:(t + 1) * GRID_W, ps]), kcat[pl.ds(off, nkeys), ps],
                            (((1,), (1,)), ((), ())), preferred_element_type=F32)
        return s + bias_ref[var, pr]

    def finish(t, pr, off, s):
        ps = slice(pr * LANES, (pr + 1) * LANES)
        m = jnp.max(s, axis=-1, keepdims=True)
        p = jnp.exp2(s - m)
        ol = jnp.dot(p.astype(BF16), vext[pr, pl.ds(off, nkeys), :], preferred_element_type=F32)
        o = ol[:, 0:LANES] / ol[:, LANES:2 * LANES]
        o_ref[t * GRID_W:(t + 1) * GRID_W, ps] = _unstack_head_pair(o).astype(o_ref.dtype)

    units = [(t, pr) for t in range(NA_TILE) for pr in range(N_HEADS // 2)]
    wins = [window(t) for t in range(NA_TILE)]
    pending = []
    for t, pr in units:
        off, var = wins[t]
        pending.append((t, pr, off, scores(t, pr, off, var)))
        if len(pending) > NA_AHEAD:
            finish(*pending.pop(0))
    for unit in pending:
        finish(*unit)


def _na_attention(qkv, btab, weights):
    tile = NA_TILE * GRID_W
    halo = NA_KH * GRID_W
    nt = SEQ // tile
    per = tile // halo
    w_specs = []
    for w in weights:
        rows = next(r for r in (w.shape[0] // nt, LANES) if r % HALO == 0 and w.shape[0] % r == 0)
        nblk = w.shape[0] // rows
        assert nblk <= nt
        w_specs.append(pl.BlockSpec((rows, w.shape[1]), lambda i, nblk=nblk: (jnp.minimum(i, nblk - 1), 0)))
    prev = lambda c: (lambda i: (jnp.maximum(i * per - 1, 0), c))
    cur = lambda c: (lambda i: (i, c))
    nxt = lambda c: (lambda i: (jnp.minimum((i + 1) * per, SEQ // halo - 1), c))
    kv_specs = [pl.BlockSpec((rows, WIDTH), f(c)) for c in (1, 2)
                for rows, f in ((halo, prev), (tile, cur), (halo, nxt))]
    return pl.pallas_call(
        _na_kernel,
        grid=(nt,),
        in_specs=[pl.BlockSpec((tile, WIDTH), cur(0))] + kv_specs
        + [pl.BlockSpec(btab.shape, lambda i: (0, 0, 0, 0), pipeline_mode=pl.Buffered(1))] + w_specs,
        out_specs=[pl.BlockSpec((tile, WIDTH), lambda i: (i, 0))] + w_specs,
        out_shape=[jax.ShapeDtypeStruct((SEQ, WIDTH), BF16)]
        + [jax.ShapeDtypeStruct(w.shape, BF16) for w in weights],
        scratch_shapes=[pltpu.VMEM((tile + 2 * halo, WIDTH), BF16),
                        pltpu.VMEM((N_HEADS // 2, tile + 2 * halo, 2 * LANES), BF16),
                        pltpu.VMEM((NA_KH, N_HEADS // 2, 2 * GRID_W, NA_KH * GRID_W), F32)],
        compiler_params=_params(("arbitrary",)),
        name="na",
    )(qkv, qkv, qkv, qkv, qkv, qkv, qkv, btab, *weights)


def _na_bias(rpb):
    cq = jnp.arange(GRID_W)
    col_start = jnp.clip(cq - NA_COLS // 2, 0, GRID_W - NA_COLS)
    col_mask = (cq[None, :] >= col_start[:, None]) & (cq[None, :] < col_start[:, None] + NA_COLS)
    dc = jnp.clip(cq[None, :] - cq[:, None], -(NA_COLS - 1), NA_COLS - 1) + NA_COLS - 1
    e = _lookup(rpb.astype(F32), dc, 2 * NA_COLS - 1)
    e = jnp.where(col_mask[None, None], e * LOG2E, NEG)
    return jnp.concatenate([e[:, :-1], e[:, 1:]], axis=-1)


def _expand_heads(w, exp_ref):
    hi = w.astype(BF16)
    lo = (w - hi.astype(F32)).astype(BF16)
    return (jnp.dot(hi, exp_ref[...], preferred_element_type=F32)
            + jnp.dot(lo, exp_ref[...], preferred_element_type=F32))


def _merge_kernel(x_ref, o1_ref, o4_ref, o16_ref, l1_ref, l4_ref, l16_ref, ob_ref, g_ref,
                  wbd_ref, wbn_ref, wout_ref, exp_ref, gate_ref, mod_ref,
                  x1_ref, h2_ref, o4n, o16n, o16j, l4n, l16n, l16j, oa_scr):
    tm = x_ref.shape[0]
    for src4, src16, n4, j16, n16 in ((o4_ref, o16_ref, o4n, o16j, o16n), (l4_ref, l16_ref, l4n, l16j, l16n)):
        for t in range(n4.shape[0]):
            ls = slice(t * LANES, (t + 1) * LANES)
            for c in range(4):
                n4[t, pl.ds(c, tm // 4, stride=4), :] = src4[c, :, ls].astype(F32)
                for c2 in range(4):
                    j16[t, c, pl.ds(c2, tm // 16, stride=4), :] = src16[c + 4 * c2, :, ls].astype(F32)
                n16[t, pl.ds(c, tm // 4, stride=4), :] = j16[t, c]

    for sb in range(tm // SUB_MERGE):
        rows = slice(sb * SUB_MERGE, (sb + 1) * SUB_MERGE)
        stats = (l1_ref[rows, :], l4n[0, rows, :], l16n[0, rows, :])
        mx = jnp.maximum(jnp.maximum(stats[0], stats[1]), stats[2])
        es = [jnp.exp2(st - mx) for st in stats]
        den = sum(e * pltpu.roll(st, LANES - N_HEADS, 1) for e, st in zip(es, stats))
        is_head = lax.broadcasted_iota(jnp.int32, den.shape, 1) < N_HEADS
        w1, w4, w16 = (_expand_heads(jnp.where(is_head, e / den, 0.0), exp_ref) for e in es)
        for t in range(WIDTH // LANES):
            ls = slice(t * LANES, (t + 1) * LANES)
            oa = w1[:, ls] * o1_ref[rows, ls] + w4[:, ls] * o4n[t, rows, :] + w16[:, ls] * o16n[t, rows, :]
            oa_scr[sb, :, ls] = oa.astype(BF16)
        ya = jnp.dot(oa_scr[sb], wbd_ref[...], preferred_element_type=F32)
        yb = jnp.dot(ob_ref[rows, :], wbn_ref[...], preferred_element_type=F32)
        merged = (g_ref[rows, 0:D_MODEL].astype(F32) * ya + g_ref[rows, D_MODEL:2 * D_MODEL].astype(F32) * yb)
        z = jnp.dot(merged.astype(BF16), wout_ref[...], preferred_element_type=F32)
        x1 = x_ref[rows, :] + gate_ref[...] * z
        x1_ref[rows, :] = x1
        h2_ref[rows, :] = _rms_modulate(x1, mod_ref).astype(BF16)


def _merge(x, o1, o4, o16, l1, l4, l16, ob, g, wbd, wbn, wout, gate1, mod2):
    tm = TM_MERGE
    nt = SEQ // tm
    row = lambda w: pl.BlockSpec((tm, w), lambda i: (i, 0))
    cls = lambda d, w: pl.BlockSpec((d, tm // d, w), lambda i: (0, i, 0))
    const = lambda shape: pl.BlockSpec(shape, lambda i: (0, 0))
    spread = (jnp.arange(LANES)[:, None] == jnp.arange(WIDTH)[None, :] // HEAD_DIM).astype(BF16)
    return pl.pallas_call(
        _merge_kernel,
        grid=(nt,),
        in_specs=[row(D_MODEL), row(WIDTH), cls(4, WIDTH), cls(16, WIDTH),
                  row(LANES), cls(4, LANES), cls(16, LANES), row(WIDTH), row(2 * D_MODEL),
                  const((WIDTH, D_MODEL)), const((WIDTH, D_MODEL)), const((D_MODEL, D_MODEL)),
                  const((LANES, WIDTH)), const((1, D_MODEL)), const((3, D_MODEL))],
        out_specs=[row(D_MODEL), row(D_MODEL)],
        out_shape=[jax.ShapeDtypeStruct((SEQ, D_MODEL), F32), jax.ShapeDtypeStruct((SEQ, D_MODEL), BF16)],
        scratch_shapes=[pltpu.VMEM((WIDTH // LANES, tm, LANES), F32), pltpu.VMEM((WIDTH // LANES, tm, LANES), F32),
                        pltpu.VMEM((WIDTH // LANES, 4, tm // 4, LANES), F32),
                        pltpu.VMEM((1, tm, LANES), F32), pltpu.VMEM((1, tm, LANES), F32),
                        pltpu.VMEM((1, 4, tm // 4, LANES), F32),
                        pltpu.VMEM((tm // SUB_MERGE, SUB_MERGE, WIDTH), BF16)],
        compiler_params=_params(("arbitrary",)),
        name="merge",
    )(x, o1, o4.reshape(4, SEQ // 4, WIDTH), o16.reshape(16, SEQ // 16, WIDTH),
      l1, l4.reshape(4, SEQ // 4, LANES), l16.reshape(16, SEQ // 16, LANES), ob, g,
      wbd, wbn, wout, spread, gate1, mod2)


def _ffn_kernel(x1_ref, hm_ref, hp_ref, hn_ref, wup_ref, cw_ref, cb_ref, wd_ref, gate_ref, gfin_ref,
                y_ref, hext, act):
    tm = x1_ref.shape[0]
    n = tm + 2 * HALO
    i = pl.program_id(0)
    hext[0:HALO, :] = jnp.where(i == 0, jnp.zeros_like(hp_ref), hp_ref[...])
    hext[HALO:HALO + tm, :] = hm_ref[...]
    hext[HALO + tm:n, :] = jnp.where(i == pl.num_programs(0) - 1, jnp.zeros_like(hn_ref), hn_ref[...])

    def conv(cols):
        p = jnp.dot(hext[...], wup_ref[:, cols], preferred_element_type=F32)
        prev = pltpu.roll(p, 1, 0)[HALO:HALO + tm]
        nxt = pltpu.roll(p, n - 1, 0)[HALO:HALO + tm]
        return (cw_ref[0:1, cols] * prev + cw_ref[1:2, cols] * p[HALO:HALO + tm] + cw_ref[2:3, cols] * nxt
                + cb_ref[:, cols])

    for c in range(D_FF // TF_FFN):
        val = slice(c * TF_FFN, (c + 1) * TF_FFN)
        gate = slice(D_FF + c * TF_FFN, D_FF + (c + 1) * TF_FFN)
        act[:, val] = (jax.nn.gelu(conv(gate), approximate=True) * conv(val)).astype(BF16)

    x2 = x1_ref[...] + gate_ref[...] * jnp.dot(act[...], wd_ref[...], preferred_element_type=F32)
    y = x2 * lax.rsqrt(jnp.mean(x2 * x2, axis=-1, keepdims=True) + RMS_EPS)
    y_ref[...] = y * gfin_ref[...]


def _ffn(x1, h2, w_up, conv_w, conv_b, w_down, gate2, g_final):
    tm = TM_FFN
    nt = SEQ // tm
    per = tm // HALO
    nhalo = SEQ // HALO
    row = lambda: pl.BlockSpec((tm, D_MODEL), lambda i: (i, 0))
    const = lambda a: pl.BlockSpec(a.shape, lambda i: (0, 0), pipeline_mode=pl.Buffered(1))
    return pl.pallas_call(
        _ffn_kernel,
        grid=(nt,),
        in_specs=[
            row(), row(),
            pl.BlockSpec((HALO, D_MODEL), lambda i: (jnp.maximum(i * per - 1, 0), 0)),
            pl.BlockSpec((HALO, D_MODEL), lambda i: (jnp.minimum((i + 1) * per, nhalo - 1), 0)),
            const(w_up), const(conv_w), const(conv_b), const(w_down), const(gate2), const(g_final),
        ],
        out_specs=row(),
        out_shape=jax.ShapeDtypeStruct((SEQ, D_MODEL), F32),
        scratch_shapes=[pltpu.VMEM((tm + 2 * HALO, D_MODEL), BF16), pltpu.VMEM((tm, D_FF), BF16)],
        compiler_params=_params(("arbitrary",)),
        name="ffn",
    )(x1, h2, h2, h2, w_up, conv_w, conv_b, w_down, gate2, g_final)


def kernel(x, c, w_ada, b_ada, g_mix, w_in, rel_bias, na_rpb, w_branch_dil, w_branch_na, w_out, g_ffn,
           w_up, conv_w, conv_b, w_down, g_final):
    assert x.shape == (1, SEQ, D_MODEL) and w_ada.shape[0] == 1
    xs = x[0]
    ada = _ada(c, w_ada[0], b_ada[0])
    shift1, scale1, gate1, shift2, scale2, gate2 = [ada[:, k * D_MODEL:(k + 1) * D_MODEL] for k in range(6)]
    mod1 = jnp.concatenate([g_mix[0][None], scale1, shift1], axis=0)
    mod2 = jnp.concatenate([g_ffn[0][None], scale2, shift2], axis=0)

    q_scale = np.ones((IN_COLS,), np.float32)
    q_scale[0:WIDTH] = HEAD_DIM ** -0.5 * LOG2E
    q_scale[3 * WIDTH:4 * WIDTH] = HEAD_DIM ** -0.5 * LOG2E
    w_in_b = (w_in[0] * q_scale).astype(BF16)

    a_nat, a_m4, a_m16, b_qkv, gates = _inproj(xs, mod1, w_in_b)

    outs = []
    for d, arr in zip(DILATIONS, (a_nat, a_m4, a_m16)):
        outs.append(_band_attention(arr.reshape(SEQ, 3 * WIDTH), _band_bias(rel_bias, d), SEQ // d))
    (o1, l1), (o4, l4), (o16, l16) = outs
    ob, wbd, wbn, wout, wup, wdown = _na_attention(
        b_qkv, _na_bias(na_rpb[0]), (w_branch_dil[0], w_branch_na[0], w_out[0], w_up[0], w_down[0]))

    x1, h2 = _merge(xs, o1, o4, o16, l1, l4, l16, ob, gates, wbd, wbn, wout, gate1, mod2)
    y = _ffn(x1, h2, wup, conv_w[0], conv_b[0].reshape(1, 2 * D_FF), wdown, gate2, g_final.reshape(1, D_MODEL))
    return y[None]
```

```python
import functools
import math

import jax
import jax.numpy as jnp
from jax import lax
from jax.experimental import pallas as pl
from jax.experimental.pallas import tpu as pltpu

F32 = jnp.float32
BF16 = jnp.bfloat16

D_MODEL = 1024
SEQ = 16384
GRID_W = 64
ROWS = SEQ // GRID_W
HEAD_DIM = 64
N_HEADS = 8
WIDTH = N_HEADS * HEAD_DIM
DILATIONS = (1, 4, 16)
BAND = 64
NA_KH = 8
NA_COLS = 16
NA_ROWS_MAX = 8
T5_BUCKETS = 32
T5_MAX_DISTANCE = 1024
D_FF = 2816
RMS_EPS = 1e-6
IN_COLS = 6 * WIDTH + 2 * D_MODEL
NEG = -1e30
LOG2E = math.log2(math.e)

VMEM_LIMIT = 56 * 1024 * 1024

TM_IN = 512
TQ_BAND = 1024
SUB_BAND = 128
TM_MERGE = 1024
SUB_MERGE = 256
TM_FFN = 1024
TF_FFN = 256
HALO = 16
LANES = 128
NA_AHEAD = 3
NA_TILE = 16
BAND_AHEAD = 3


def _params(sem):
    return pltpu.CompilerParams(dimension_semantics=sem, vmem_limit_bytes=VMEM_LIMIT)


def _ada_kernel(c_ref, w_ref, b_ref, o_ref):
    c = c_ref[...]
    s = c / (1.0 + jnp.exp(-c))
    o_ref[...] = jnp.sum(w_ref[...] * s, axis=0, keepdims=True) + b_ref[...]


def _ada(c, w_ada, b_ada):
    n_out = w_ada.shape[1]
    tn = D_MODEL
    return pl.pallas_call(
        _ada_kernel,
        grid=(n_out // tn,),
        in_specs=[
            pl.BlockSpec((D_MODEL, 1), lambda j: (0, 0)),
            pl.BlockSpec((D_MODEL, tn), lambda j: (0, j)),
            pl.BlockSpec((1, tn), lambda j: (0, j)),
        ],
        out_specs=pl.BlockSpec((1, tn), lambda j: (0, j)),
        out_shape=jax.ShapeDtypeStruct((1, n_out), F32),
        compiler_params=_params(("arbitrary",)),
        name="ada",
    )(c.reshape(D_MODEL, 1), w_ada, b_ada.reshape(1, n_out))


def _rms_modulate(x, mod_ref):
    y = x * lax.rsqrt(jnp.mean(x * x, axis=-1, keepdims=True) + RMS_EPS)
    return (y * mod_ref[0:1, :]) * (1.0 + mod_ref[1:2, :]) + mod_ref[2:3, :]


def _inproj_kernel(x0_ref, xn_ref, mod_ref, w_ref, anat_ref, am4_ref, am16_ref, b_ref, g_ref,
                   h_even, h_odd, res_scr, res4_scr):
    tm = xn_ref.shape[0]
    i = pl.program_id(0)

    @pl.when(i == 0)
    def _():
        h_even[...] = _rms_modulate(x0_ref[...], mod_ref).astype(BF16)

    def project(h_cur, h_next):
        for k, n in enumerate((6, 7, 8, 9, 0, 1, 2, 3, 4, 5)):
            cols = slice(n * WIDTH, (n + 1) * WIDTH)
            w = w_ref[:, cols]
            if n in (0, 3):
                w = w * (HEAD_DIM ** -0.5 * LOG2E)
            r = jnp.dot(h_cur[...], w.astype(BF16), preferred_element_type=F32)
            if k == 2:
                h_next[...] = _rms_modulate(xn_ref[...], mod_ref).astype(BF16)
            if n < 3:
                anat_ref[:, cols] = r.astype(BF16)
                for t in range(WIDTH // LANES):
                    lanes = slice(n * WIDTH + t * LANES, n * WIDTH + (t + 1) * LANES)
                    res_scr[t] = r[:, t * LANES:(t + 1) * LANES]
                    for c in range(4):
                        cls4 = res_scr[t, pl.ds(c, tm // 4, stride=4), :]
                        am4_ref[c, :, lanes] = cls4.astype(BF16)
                        res4_scr[t, c] = cls4
                        for c2 in range(4):
                            am16_ref[c + 4 * c2, :, lanes] = (
                                res4_scr[t, c, pl.ds(c2, tm // 16, stride=4), :].astype(BF16))
            elif n < 6:
                b_ref[:, (n - 3) * WIDTH:(n - 2) * WIDTH] = r.astype(BF16)
            else:
                g_ref[:, (n - 6) * WIDTH:(n - 5) * WIDTH] = (1.0 / (1.0 + jnp.exp(-r))).astype(BF16)

    @pl.when(i % 2 == 0)
    def _():
        project(h_even, h_odd)

    @pl.when(i % 2 == 1)
    def _():
        project(h_odd, h_even)


def _inproj(x, mod1, w_in):
    tm = TM_IN
    nt = SEQ // tm
    qkv = 3 * WIDTH
    return pl.pallas_call(
        _inproj_kernel,
        grid=(nt,),
        in_specs=[
            pl.BlockSpec((tm, D_MODEL), lambda i: (0, 0)),
            pl.BlockSpec((tm, D_MODEL), lambda i: (jnp.minimum(i + 1, nt - 1), 0)),
            pl.BlockSpec((3, D_MODEL), lambda i: (0, 0)),
            pl.BlockSpec((D_MODEL, IN_COLS), lambda i: (0, 0), pipeline_mode=pl.Buffered(1)),
        ],
        out_specs=[
            pl.BlockSpec((tm, qkv), lambda i: (i, 0)),
            pl.BlockSpec((4, tm // 4, qkv), lambda i: (0, i, 0)),
            pl.BlockSpec((16, tm // 16, qkv), lambda i: (0, i, 0)),
            pl.BlockSpec((tm, qkv), lambda i: (i, 0)),
            pl.BlockSpec((tm, 2 * D_MODEL), lambda i: (i, 0)),
        ],
        out_shape=[
            jax.ShapeDtypeStruct((SEQ, qkv), BF16),
            jax.ShapeDtypeStruct((4, SEQ // 4, qkv), BF16),
            jax.ShapeDtypeStruct((16, SEQ // 16, qkv), BF16),
            jax.ShapeDtypeStruct((SEQ, qkv), BF16),
            jax.ShapeDtypeStruct((SEQ, 2 * D_MODEL), BF16),
        ],
        scratch_shapes=[pltpu.VMEM((tm, D_MODEL), BF16), pltpu.VMEM((tm, D_MODEL), BF16),
                        pltpu.VMEM((WIDTH // LANES, tm, LANES), F32),
                        pltpu.VMEM((WIDTH // LANES, 4, tm // 4, LANES), F32)],
        compiler_params=_params(("arbitrary",)),
        name="inproj",
    )(x, x, mod1, w_in)


def _stack_head_pair(q):
    first = lax.broadcasted_iota(jnp.int32, q.shape, 1) < HEAD_DIM
    qf = q.astype(F32)
    return jnp.concatenate([jnp.where(first, qf, 0.0), jnp.where(first, 0.0, qf)], axis=0).astype(q.dtype)


def _unstack_head_pair(o):
    n = o.shape[0] // 2
    first = lax.broadcasted_iota(jnp.int32, (n, o.shape[1]), 1) < HEAD_DIM
    return jnp.where(first, o[0:n], o[n:2 * n])


def _fill_value_ones(vext, first_step):
    @pl.when(first_step)
    def _():
        vext[:, :, LANES:2 * LANES] = jnp.ones(vext.shape[:2] + (LANES,), vext.dtype)


def _band_kernel(q_ref, kl_ref, km_ref, kr_ref, vl_ref, vm_ref, vr_ref, bias_ref,
                 o_ref, stat_ref, kcat, vext, *, seg_len):
    tq = q_ref.shape[0]
    sub = SUB_BAND
    nsub = tq // sub
    nk = sub + 2 * BAND
    start = pl.program_id(0) * tq
    kcat[0:BAND, :] = kl_ref[...]
    kcat[BAND:BAND + tq, :] = km_ref[...]
    kcat[BAND + tq:tq + 2 * BAND, :] = kr_ref[...]
    _fill_value_ones(vext, pl.program_id(0) == 0)
    for pr in range(N_HEADS // 2):
        ps = slice(pr * LANES, (pr + 1) * LANES)
        vext[pr, 0:BAND, 0:LANES] = vl_ref[:, ps]
        vext[pr, BAND:BAND + tq, 0:LANES] = vm_ref[:, ps]
        vext[pr, BAND + tq:tq + 2 * BAND, 0:LANES] = vr_ref[:, ps]
    col = lax.broadcasted_iota(jnp.int32, (1, nk), 1)
    edge = {0: jnp.where((col < BAND) & ((start % seg_len) == 0), NEG, 0.0),
            nsub - 1: jnp.where((col >= BAND + sub) & (((start + tq) % seg_len) == 0), NEG, 0.0)}
    stat_ref[...] = jnp.zeros_like(stat_ref)

    def scores(j, pr):
        ps = slice(pr * LANES, (pr + 1) * LANES)
        s = lax.dot_general(_stack_head_pair(q_ref[j * sub:(j + 1) * sub, ps]), kcat[j * sub:j * sub + nk, ps],
                            (((1,), (1,)), ((), ())), preferred_element_type=F32)
        s = s + bias_ref[pr]
        return s + edge[j] if j in edge else s

    def finish(j, pr, s):
        ps = slice(pr * LANES, (pr + 1) * LANES)
        rows = slice(j * sub, (j + 1) * sub)
        m = jnp.max(s, axis=-1, keepdims=True)
        p = jnp.exp2(s - m)
        ol = jnp.dot(p.astype(BF16), vext[pr, j * sub:j * sub + nk, :], preferred_element_type=F32)
        o_ref[rows, ps] = _unstack_head_pair(ol[:, 0:LANES]).astype(o_ref.dtype)
        stat_ref[rows, 2 * pr:2 * pr + 1] = m[0:sub]
        stat_ref[rows, 2 * pr + 1:2 * pr + 2] = m[sub:2 * sub]
        for h, r0 in ((2 * pr, 0), (2 * pr + 1, sub)):
            lane = LANES + N_HEADS + h
            stat_ref[rows, N_HEADS + h:N_HEADS + h + 1] = ol[r0:r0 + sub, lane:lane + 1]

    pending = []
    for j in range(nsub):
        for pr in range(N_HEADS // 2):
            pending.append((j, pr, scores(j, pr)))
            if len(pending) > BAND_AHEAD:
                finish(*pending.pop(0))
    for unit in pending:
        finish(*unit)


def _band_attention(qkv, bias, seg_len):
    tq = TQ_BAND
    nt = SEQ // tq
    per = tq // BAND
    nhalo = SEQ // BAND
    left = lambda c: (lambda i: (jnp.maximum(i * per - 1, 0), c))
    right = lambda c: (lambda i: (jnp.minimum((i + 1) * per, nhalo - 1), c))
    main = lambda c: (lambda i: (i, c))
    kv_specs = []
    for c in (1, 2):
        kv_specs += [pl.BlockSpec((BAND, WIDTH), left(c)), pl.BlockSpec((tq, WIDTH), main(c)),
                     pl.BlockSpec((BAND, WIDTH), right(c))]
    nk = tq + 2 * BAND
    return pl.pallas_call(
        functools.partial(_band_kernel, seg_len=seg_len),
        grid=(nt,),
        in_specs=[pl.BlockSpec((tq, WIDTH), main(0))] + kv_specs
        + [pl.BlockSpec((N_HEADS // 2, 2 * SUB_BAND, SUB_BAND + 2 * BAND), lambda i: (0, 0, 0))],
        out_specs=[pl.BlockSpec((tq, WIDTH), lambda i: (i, 0)),
                   pl.BlockSpec((tq, LANES), lambda i: (i, 0))],
        out_shape=[jax.ShapeDtypeStruct((SEQ, WIDTH), BF16),
                   jax.ShapeDtypeStruct((SEQ, LANES), F32)],
        scratch_shapes=[pltpu.VMEM((nk, WIDTH), BF16), pltpu.VMEM((N_HEADS // 2, nk, 2 * LANES), BF16)],
        compiler_params=_params(("arbitrary",)),
        name=f"band_l{seg_len}",
    )(qkv, qkv, qkv, qkv, qkv, qkv, qkv, bias.reshape(N_HEADS // 2, 2 * SUB_BAND, SUB_BAND + 2 * BAND))


def _t5_bucket(rel):
    n = T5_BUCKETS // 2
    max_exact = n // 2
    sign_part = jnp.where(rel > 0, n, 0)
    a = jnp.abs(rel)
    af = jnp.maximum(a, 1).astype(F32)
    large = max_exact + (jnp.log(af / max_exact) / math.log(T5_MAX_DISTANCE / max_exact)
                         * (n - max_exact)).astype(jnp.int32)
    large = jnp.minimum(large, n - 1)
    return sign_part + jnp.where(a < max_exact, a, large)


def _band_bias(rel_bias, dilation):
    tq = SUB_BAND
    delta = jnp.arange(tq + 2 * BAND)[None, :] - BAND - jnp.arange(tq)[:, None]
    b = _lookup(rel_bias.astype(F32).T, _t5_bucket(delta * dilation), T5_BUCKETS)
    return jnp.where((jnp.abs(delta) <= BAND)[None], b * LOG2E, NEG)


def _lookup(table, idx, n):
    onehot = (jnp.arange(n)[:, None] == idx.reshape(1, -1)).astype(F32)
    out = jnp.dot(table.reshape(-1, n), onehot, precision=lax.Precision.HIGHEST)
    return out.reshape(table.shape[:-1] + idx.shape)


def _na_kernel(q_ref, kp_ref, kc_ref, kn_ref, vp_ref, vc_ref, vn_ref, btab_ref, *rest):
    n_w = (len(rest) - 4) // 2
    w_in_refs, o_ref, w_out_refs = rest[:n_w], rest[n_w], rest[n_w + 1:2 * n_w + 1]
    kcat, vext, bias_ref = rest[2 * n_w + 1:]
    for src, dst in zip(w_in_refs, w_out_refs):
        dst[...] = src[...].astype(dst.dtype)
    tile = q_ref.shape[0]
    halo = NA_KH * GRID_W
    i = pl.program_id(0)

    @pl.when(i == 0)
    def _():
        for v in range(NA_KH):
            for h in range(N_HEADS):
                for k in range(NA_KH // 2):
                    bias_ref[v, h // 2, (h % 2) * GRID_W:(h % 2 + 1) * GRID_W, k * LANES:(k + 1) * LANES] = (
                        btab_ref[h, v + 2 * k])
    _fill_value_ones(vext, i == 0)
    for rows, k_ref, v_ref in ((slice(0, halo), kp_ref, vp_ref), (slice(halo, halo + tile), kc_ref, vc_ref),
                               (slice(halo + tile, 2 * halo + tile), kn_ref, vn_ref)):
        kcat[rows, :] = k_ref[...]
        for pr in range(N_HEADS // 2):
            vext[pr, rows, 0:LANES] = v_ref[:, pr * LANES:(pr + 1) * LANES]
    nkeys = NA_KH * GRID_W

    def window(t):
        r = i * NA_TILE + t
        rs = jnp.clip(r - NA_KH // 2, 0, ROWS - NA_KH)
        return pl.multiple_of((rs - (i * NA_TILE - NA_KH)) * GRID_W, GRID_W), rs - r + (NA_KH - 1)

    def scores(t, pr, off, var):
        ps = slice(pr * LANES, (pr + 1) * LANES)
        s = lax.dot_general(_stack_head_pair(q_ref[t * GRID_W:(t + 1) * GRID_W, ps]), kcat[pl.ds(off, nkeys), ps],
                            (((1,), (1,)), ((), ())), preferred_element_type=F32)
        return s + bias_ref[var, pr]

    def finish(t, pr, off, s):
        ps = slice(pr * LANES, (pr + 1) * LANES)
        m = jnp.max(s, axis=-1, keepdims=True)
        p = jnp.exp2(s - m)
        ol = jnp.dot(p.astype(BF16), vext[pr, pl.ds(off, nkeys), :], preferred_element_type=F32)
        o = ol[:, 0:LANES] / ol[:, LANES:2 * LANES]
        o_ref[t * GRID_W:(t + 1) * GRID_W, ps] = _unstack_head_pair(o).astype(o_ref.dtype)

    units = [(t, pr) for t in range(NA_TILE) for pr in range(N_HEADS // 2)]
    wins = [window(t) for t in range(NA_TILE)]
    pending = []
    for t, pr in units:
        off, var = wins[t]
        pending.append((t, pr, off, scores(t, pr, off, var)))
        if len(pending) > NA_AHEAD:
            finish(*pending.pop(0))
    for unit in pending:
        finish(*unit)


def _na_attention(qkv, btab, weights):
    tile = NA_TILE * GRID_W
    halo = NA_KH * GRID_W
    nt = SEQ // tile
    per = tile // halo
    w_specs = []
    for w in weights:
        rows = next(r for r in (w.shape[0] // nt, LANES) if r % HALO == 0 and w.shape[0] % r == 0)
        nblk = w.shape[0] // rows
        assert nblk <= nt
        w_specs.append(pl.BlockSpec((rows, w.shape[1]), lambda i, nblk=nblk: (jnp.minimum(i, nblk - 1), 0)))
    prev = lambda c: (lambda i: (jnp.maximum(i * per - 1, 0), c))
    cur = lambda c: (lambda i: (i, c))
    nxt = lambda c: (lambda i: (jnp.minimum((i + 1) * per, SEQ // halo - 1), c))
    kv_specs = [pl.BlockSpec((rows, WIDTH), f(c)) for c in (1, 2)
                for rows, f in ((halo, prev), (tile, cur), (halo, nxt))]
    return pl.pallas_call(
        _na_kernel,
        grid=(nt,),
        in_specs=[pl.BlockSpec((tile, WIDTH), cur(0))] + kv_specs
        + [pl.BlockSpec(btab.shape, lambda i: (0, 0, 0, 0), pipeline_mode=pl.Buffered(1))] + w_specs,
        out_specs=[pl.BlockSpec((tile, WIDTH), lambda i: (i, 0))] + w_specs,
        out_shape=[jax.ShapeDtypeStruct((SEQ, WIDTH), BF16)]
        + [jax.ShapeDtypeStruct(w.shape, BF16) for w in weights],
        scratch_shapes=[pltpu.VMEM((tile + 2 * halo, WIDTH), BF16),
                        pltpu.VMEM((N_HEADS // 2, tile + 2 * halo, 2 * LANES), BF16),
                        pltpu.VMEM((NA_KH, N_HEADS // 2, 2 * GRID_W, NA_KH * GRID_W), F32)],
        compiler_params=_params(("arbitrary",)),
        name="na",
    )(qkv, qkv, qkv, qkv, qkv, qkv, qkv, btab, *weights)


def _na_bias(rpb):
    cq = jnp.arange(GRID_W)
    col_start = jnp.clip(cq - NA_COLS // 2, 0, GRID_W - NA_COLS)
    col_mask = (cq[None, :] >= col_start[:, None]) & (cq[None, :] < col_start[:, None] + NA_COLS)
    dc = jnp.clip(cq[None, :] - cq[:, None], -(NA_COLS - 1), NA_COLS - 1) + NA_COLS - 1
    e = _lookup(rpb.astype(F32), dc, 2 * NA_COLS - 1)
    e = jnp.where(col_mask[None, None], e * LOG2E, NEG)
    return jnp.concatenate([e[:, :-1], e[:, 1:]], axis=-1)


def _expand_heads(w, exp_ref):
    hi = w.astype(BF16)
    lo = (w - hi.astype(F32)).astype(BF16)
    return (jnp.dot(hi, exp_ref[...], preferred_element_type=F32)
            + jnp.dot(lo, exp_ref[...], preferred_element_type=F32))


def _merge_kernel(x_ref, o1_ref, o4_ref, o16_ref, l1_ref, l4_ref, l16_ref, ob_ref, g_ref,
                  wbd_ref, wbn_ref, wout_ref, exp_ref, gate_ref, mod_ref,
                  x1_ref, h2_ref, o4n, o16n, o16j, l4n, l16n, l16j, oa_scr):
    tm = x_ref.shape[0]
    for src4, src16, n4, j16, n16 in ((o4_ref, o16_ref, o4n, o16j, o16n), (l4_ref, l16_ref, l4n, l16j, l16n)):
        for t in range(n4.shape[0]):
            ls = slice(t * LANES, (t + 1) * LANES)
            for c in range(4):
                n4[t, pl.ds(c, tm // 4, stride=4), :] = src4[c, :, ls].astype(F32)
                for c2 in range(4):
                    j16[t, c, pl.ds(c2, tm // 16, stride=4), :] = src16[c + 4 * c2, :, ls].astype(F32)
                n16[t, pl.ds(c, tm // 4, stride=4), :] = j16[t, c]

    for sb in range(tm // SUB_MERGE):
        rows = slice(sb * SUB_MERGE, (sb + 1) * SUB_MERGE)
        stats = (l1_ref[rows, :], l4n[0, rows, :], l16n[0, rows, :])
        mx = jnp.maximum(jnp.maximum(stats[0], stats[1]), stats[2])
        es = [jnp.exp2(st - mx) for st in stats]
        den = sum(e * pltpu.roll(st, LANES - N_HEADS, 1) for e, st in zip(es, stats))
        is_head = lax.broadcasted_iota(jnp.int32, den.shape, 1) < N_HEADS
        w1, w4, w16 = (_expand_heads(jnp.where(is_head, e / den, 0.0), exp_ref) for e in es)
        for t in range(WIDTH // LANES):
            ls = slice(t * LANES, (t + 1) * LANES)
            oa = w1[:, ls] * o1_ref[rows, ls] + w4[:, ls] * o4n[t, rows, :] + w16[:, ls] * o16n[t, rows, :]
            oa_scr[sb, :, ls] = oa.astype(BF16)
        ya = jnp.dot(oa_scr[sb], wbd_ref[...], preferred_element_type=F32)
        yb = jnp.dot(ob_ref[rows, :], wbn_ref[...], preferred_element_type=F32)
        merged = (g_ref[rows, 0:D_MODEL].astype(F32) * ya + g_ref[rows, D_MODEL:2 * D_MODEL].astype(F32) * yb)
        z = jnp.dot(merged.astype(BF16), wout_ref[...], preferred_element_type=F32)
        x1 = x_ref[rows, :] + gate_ref[...] * z
        x1_ref[rows, :] = x1
        h2_ref[rows, :] = _rms_modulate(x1, mod_ref).astype(BF16)


def _merge(x, o1, o4, o16, l1, l4, l16, ob, g, wbd, wbn, wout, gate1, mod2):
    tm = TM_MERGE
    nt = SEQ // tm
    row = lambda w: pl.BlockSpec((tm, w), lambda i: (i, 0))
    cls = lambda d, w: pl.BlockSpec((d, tm // d, w), lambda i: (0, i, 0))
    const = lambda shape: pl.BlockSpec(shape, lambda i: (0, 0))
    spread = (jnp.arange(LANES)[:, None] == jnp.arange(WIDTH)[None, :] // HEAD_DIM).astype(BF16)
    return pl.pallas_call(
        _merge_kernel,
        grid=(nt,),
        in_specs=[row(D_MODEL), row(WIDTH), cls(4, WIDTH), cls(16, WIDTH),
                  row(LANES), cls(4, LANES), cls(16, LANES), row(WIDTH), row(2 * D_MODEL),
                  const((WIDTH, D_MODEL)), const((WIDTH, D_MODEL)), const((D_MODEL, D_MODEL)),
                  const((LANES, WIDTH)), const((1, D_MODEL)), const((3, D_MODEL))],
        out_specs=[row(D_MODEL), row(D_MODEL)],
        out_shape=[jax.ShapeDtypeStruct((SEQ, D_MODEL), F32), jax.ShapeDtypeStruct((SEQ, D_MODEL), BF16)],
        scratch_shapes=[pltpu.VMEM((WIDTH // LANES, tm, LANES), F32), pltpu.VMEM((WIDTH // LANES, tm, LANES), F32),
                        pltpu.VMEM((WIDTH // LANES, 4, tm // 4, LANES), F32),
                        pltpu.VMEM((1, tm, LANES), F32), pltpu.VMEM((1, tm, LANES), F32),
                        pltpu.VMEM((1, 4, tm // 4, LANES), F32),
                        pltpu.VMEM((tm // SUB_MERGE, SUB_MERGE, WIDTH), BF16)],
        compiler_params=_params(("arbitrary",)),
        name="merge",
    )(x, o1, o4.reshape(4, SEQ // 4, WIDTH), o16.reshape(16, SEQ // 16, WIDTH),
      l1, l4.reshape(4, SEQ // 4, LANES), l16.reshape(16, SEQ // 16, LANES), ob, g,
      wbd, wbn, wout, spread, gate1, mod2)


def _ffn_kernel(x1_ref, hm_ref, hp_ref, hn_ref, wup_ref, cw_ref, cb_ref, wd_ref, gate_ref, gfin_ref,
                y_ref, hext, act):
    tm = x1_ref.shape[0]
    n = tm + 2 * HALO
    i = pl.program_id(0)
    hext[0:HALO, :] = jnp.where(i == 0, jnp.zeros_like(hp_ref), hp_ref[...])
    hext[HALO:HALO + tm, :] = hm_ref[...]
    hext[HALO + tm:n, :] = jnp.where(i == pl.num_programs(0) - 1, jnp.zeros_like(hn_ref), hn_ref[...])

    def conv(cols):
        p = jnp.dot(hext[...], wup_ref[:, cols], preferred_element_type=F32)
        prev = pltpu.roll(p, 1, 0)[HALO:HALO + tm]
        nxt = pltpu.roll(p, n - 1, 0)[HALO:HALO + tm]
        return (cw_ref[0:1, cols] * prev + cw_ref[1:2, cols] * p[HALO:HALO + tm] + cw_ref[2:3, cols] * nxt
                + cb_ref[:, cols])

    for c in range(D_FF // TF_FFN):
        val = slice(c * TF_FFN, (c + 1) * TF_FFN)
        gate = slice(D_FF + c * TF_FFN, D_FF + (c + 1) * TF_FFN)
        act[:, val] = (jax.nn.gelu(conv(gate), approximate=True) * conv(val)).astype(BF16)

    x2 = x1_ref[...] + gate_ref[...] * jnp.dot(act[...], wd_ref[...], preferred_element_type=F32)
    y = x2 * lax.rsqrt(jnp.mean(x2 * x2, axis=-1, keepdims=True) + RMS_EPS)
    y_ref[...] = y * gfin_ref[...]


def _ffn(x1, h2, w_up, conv_w, conv_b, w_down, gate2, g_final):
    tm = TM_FFN
    nt = SEQ // tm
    per = tm // HALO
    nhalo = SEQ // HALO
    row = lambda: pl.BlockSpec((tm, D_MODEL), lambda i: (i, 0))
    const = lambda a: pl.BlockSpec(a.shape, lambda i: (0, 0), pipeline_mode=pl.Buffered(1))
    return pl.pallas_call(
        _ffn_kernel,
        grid=(nt,),
        in_specs=[
            row(), row(),
            pl.BlockSpec((HALO, D_MODEL), lambda i: (jnp.maximum(i * per - 1, 0), 0)),
            pl.BlockSpec((HALO, D_MODEL), lambda i: (jnp.minimum((i + 1) * per, nhalo - 1), 0)),
            const(w_up), const(conv_w), const(conv_b), const(w_down), const(gate2), const(g_final),
        ],
        out_specs=row(),
        out_shape=jax.ShapeDtypeStruct((SEQ, D_MODEL), F32),
        scratch_shapes=[pltpu.VMEM((tm + 2 * HALO, D_MODEL), BF16), pltpu.VMEM((tm, D_FF), BF16)],
        compiler_params=_params(("arbitrary",)),
        name="ffn",
    )(x1, h2, h2, h2, w_up, conv_w, conv_b, w_down, gate2, g_final)


def kernel(x, c, w_ada, b_ada, g_mix, w_in, rel_bias, na_rpb, w_branch_dil, w_branch_na, w_out, g_ffn,
           w_up, conv_w, conv_b, w_down, g_final):
    assert x.shape == (1, SEQ, D_MODEL) and w_ada.shape[0] == 1
    xs = x[0]
    ada = _ada(c, w_ada[0], b_ada[0])
    shift1, scale1, gate1, shift2, scale2, gate2 = [ada[:, k * D_MODEL:(k + 1) * D_MODEL] for k in range(6)]
    mod1 = jnp.concatenate([g_mix[0][None], scale1, shift1], axis=0)
    mod2 = jnp.concatenate([g_ffn[0][None], scale2, shift2], axis=0)

    a_nat, a_m4, a_m16, b_qkv, gates = _inproj(xs, mod1, w_in[0])

    outs = []
    for d, arr in zip(DILATIONS, (a_nat, a_m4, a_m16)):
        outs.append(_band_attention(arr.reshape(SEQ, 3 * WIDTH), _band_bias(rel_bias, d), SEQ // d))
    (o1, l1), (o4, l4), (o16, l16) = outs
    ob, wbd, wbn, wout, wup, wdown = _na_attention(
        b_qkv, _na_bias(na_rpb[0]), (w_branch_dil[0], w_branch_na[0], w_out[0], w_up[0], w_down[0]))

    x1, h2 = _merge(xs, o1, o4, o16, l1, l4, l16, ob, gates, wbd, wbn, wout, gate1, mod2)
    y = _ffn(x1, h2, wup, conv_w[0], conv_b[0].reshape(1, 2 * D_FF), wdown, gate2, g_final.reshape(1, D_MODEL))
    return y[None]
```

```python
import functools
import math

import jax
import jax.numpy as jnp
from jax import lax
from jax.experimental import pallas as pl
from jax.experimental.pallas import tpu as pltpu

F32 = jnp.float32
BF16 = jnp.bfloat16

D_MODEL = 1024
SEQ = 16384
GRID_W = 64
ROWS = SEQ // GRID_W
HEAD_DIM = 64
N_HEADS = 8
WIDTH = N_HEADS * HEAD_DIM
DILATIONS = (1, 4, 16)
BAND = 64
NA_KH = 8
NA_COLS = 16
NA_ROWS_MAX = 8
T5_BUCKETS = 32
T5_MAX_DISTANCE = 1024
D_FF = 2816
RMS_EPS = 1e-6
IN_COLS = 6 * WIDTH + 2 * D_MODEL
NEG = -1e30
LOG2E = math.log2(math.e)

VMEM_LIMIT = 56 * 1024 * 1024

TM_IN = 512
TQ_BAND = 2048
SUB_BAND = 128
TM_MERGE = 1024
SUB_MERGE = 256
TM_FFN = 1024
TF_FFN = 256
HALO = 16
LANES = 128
NA_AHEAD = 3
NA_TILE = 16
BAND_AHEAD = 3


def _params(sem):
    return pltpu.CompilerParams(dimension_semantics=sem, vmem_limit_bytes=VMEM_LIMIT)


def _ada_kernel(c_ref, w_ref, b_ref, o_ref):
    c = c_ref[...]
    s = c / (1.0 + jnp.exp(-c))
    o_ref[...] = jnp.sum(w_ref[...] * s, axis=0, keepdims=True) + b_ref[...]


def _ada(c, w_ada, b_ada):
    n_out = w_ada.shape[1]
    tn = D_MODEL
    return pl.pallas_call(
        _ada_kernel,
        grid=(n_out // tn,),
        in_specs=[
            pl.BlockSpec((D_MODEL, 1), lambda j: (0, 0)),
            pl.BlockSpec((D_MODEL, tn), lambda j: (0, j)),
            pl.BlockSpec((1, tn), lambda j: (0, j)),
        ],
        out_specs=pl.BlockSpec((1, tn), lambda j: (0, j)),
        out_shape=jax.ShapeDtypeStruct((1, n_out), F32),
        compiler_params=_params(("arbitrary",)),
        name="ada",
    )(c.reshape(D_MODEL, 1), w_ada, b_ada.reshape(1, n_out))


def _rms_modulate(x, mod_ref):
    y = x * lax.rsqrt(jnp.mean(x * x, axis=-1, keepdims=True) + RMS_EPS)
    return (y * mod_ref[0:1, :]) * (1.0 + mod_ref[1:2, :]) + mod_ref[2:3, :]


def _inproj_kernel(x0_ref, xn_ref, mod_ref, w_ref, anat_ref, am4_ref, am16_ref, b_ref, g_ref,
                   h_even, h_odd, res_scr, res4_scr):
    tm = xn_ref.shape[0]
    i = pl.program_id(0)

    @pl.when(i == 0)
    def _():
        h_even[...] = _rms_modulate(x0_ref[...], mod_ref).astype(BF16)

    def project(h_cur, h_next):
        for k, n in enumerate((6, 7, 8, 9, 0, 1, 2, 3, 4, 5)):
            cols = slice(n * WIDTH, (n + 1) * WIDTH)
            w = w_ref[:, cols]
            if n in (0, 3):
                w = w * (HEAD_DIM ** -0.5 * LOG2E)
            r = jnp.dot(h_cur[...], w.astype(BF16), preferred_element_type=F32)
            if k == 2:
                h_next[...] = _rms_modulate(xn_ref[...], mod_ref).astype(BF16)
            if n < 3:
                anat_ref[:, cols] = r.astype(BF16)
                for t in range(WIDTH // LANES):
                    lanes = slice(n * WIDTH + t * LANES, n * WIDTH + (t + 1) * LANES)
                    res_scr[t] = r[:, t * LANES:(t + 1) * LANES]
                    for c in range(4):
                        cls4 = res_scr[t, pl.ds(c, tm // 4, stride=4), :]
                        am4_ref[c, :, lanes] = cls4.astype(BF16)
                        res4_scr[t, c] = cls4
                        for c2 in range(4):
                            am16_ref[c + 4 * c2, :, lanes] = (
                                res4_scr[t, c, pl.ds(c2, tm // 16, stride=4), :].astype(BF16))
            elif n < 6:
                b_ref[:, (n - 3) * WIDTH:(n - 2) * WIDTH] = r.astype(BF16)
            else:
                g_ref[:, (n - 6) * WIDTH:(n - 5) * WIDTH] = (1.0 / (1.0 + jnp.exp(-r))).astype(BF16)

    @pl.when(i % 2 == 0)
    def _():
        project(h_even, h_odd)

    @pl.when(i % 2 == 1)
    def _():
        project(h_odd, h_even)


def _inproj(x, mod1, w_in):
    tm = TM_IN
    nt = SEQ // tm
    qkv = 3 * WIDTH
    return pl.pallas_call(
        _inproj_kernel,
        grid=(nt,),
        in_specs=[
            pl.BlockSpec((tm, D_MODEL), lambda i: (0, 0)),
            pl.BlockSpec((tm, D_MODEL), lambda i: (jnp.minimum(i + 1, nt - 1), 0)),
            pl.BlockSpec((3, D_MODEL), lambda i: (0, 0)),
            pl.BlockSpec((D_MODEL, IN_COLS), lambda i: (0, 0), pipeline_mode=pl.Buffered(1)),
        ],
        out_specs=[
            pl.BlockSpec((tm, qkv), lambda i: (i, 0)),
            pl.BlockSpec((4, tm // 4, qkv), lambda i: (0, i, 0)),
            pl.BlockSpec((16, tm // 16, qkv), lambda i: (0, i, 0)),
            pl.BlockSpec((tm, qkv), lambda i: (i, 0)),
            pl.BlockSpec((tm, 2 * D_MODEL), lambda i: (i, 0)),
        ],
        out_shape=[
            jax.ShapeDtypeStruct((SEQ, qkv), BF16),
            jax.ShapeDtypeStruct((4, SEQ // 4, qkv), BF16),
            jax.ShapeDtypeStruct((16, SEQ // 16, qkv), BF16),
            jax.ShapeDtypeStruct((SEQ, qkv), BF16),
            jax.ShapeDtypeStruct((SEQ, 2 * D_MODEL), BF16),
        ],
        scratch_shapes=[pltpu.VMEM((tm, D_MODEL), BF16), pltpu.VMEM((tm, D_MODEL), BF16),
                        pltpu.VMEM((WIDTH // LANES, tm, LANES), F32),
                        pltpu.VMEM((WIDTH // LANES, 4, tm // 4, LANES), F32)],
        compiler_params=_params(("arbitrary",)),
        name="inproj",
    )(x, x, mod1, w_in)


def _stack_head_pair(q):
    first = lax.broadcasted_iota(jnp.int32, q.shape, 1) < HEAD_DIM
    qf = q.astype(F32)
    return jnp.concatenate([jnp.where(first, qf, 0.0), jnp.where(first, 0.0, qf)], axis=0).astype(q.dtype)


def _unstack_head_pair(o):
    n = o.shape[0] // 2
    first = lax.broadcasted_iota(jnp.int32, (n, o.shape[1]), 1) < HEAD_DIM
    return jnp.where(first, o[0:n], o[n:2 * n])


def _fill_value_ones(vext, first_step):
    @pl.when(first_step)
    def _():
        vext[:, :, LANES:2 * LANES] = jnp.ones(vext.shape[:2] + (LANES,), vext.dtype)


def _band_kernel(q_ref, kl_ref, km_ref, kr_ref, vl_ref, vm_ref, vr_ref, bias_ref,
                 o_ref, stat_ref, kcat, vext, *, seg_len):
    tq = q_ref.shape[0]
    sub = SUB_BAND
    nsub = tq // sub
    nk = sub + 2 * BAND
    start = pl.program_id(0) * tq
    kcat[0:BAND, :] = kl_ref[...]
    kcat[BAND:BAND + tq, :] = km_ref[...]
    kcat[BAND + tq:tq + 2 * BAND, :] = kr_ref[...]
    _fill_value_ones(vext, pl.program_id(0) == 0)
    for pr in range(N_HEADS // 2):
        ps = slice(pr * LANES, (pr + 1) * LANES)
        vext[pr, 0:BAND, 0:LANES] = vl_ref[:, ps]
        vext[pr, BAND:BAND + tq, 0:LANES] = vm_ref[:, ps]
        vext[pr, BAND + tq:tq + 2 * BAND, 0:LANES] = vr_ref[:, ps]
    col = lax.broadcasted_iota(jnp.int32, (1, nk), 1)
    period = min(tq, seg_len)
    edge = {}
    for j in range(nsub):
        masks = []
        if (j * sub) % period == 0:
            masks.append((col < BAND) & (((start + j * sub) % seg_len) == 0))
        if ((j + 1) * sub) % period == 0:
            masks.append((col >= BAND + sub) & (((start + (j + 1) * sub) % seg_len) == 0))
        if masks:
            edge[j] = jnp.where(functools.reduce(jnp.logical_or, masks), NEG, 0.0)
    stat_ref[...] = jnp.zeros_like(stat_ref)

    def scores(j, pr):
        ps = slice(pr * LANES, (pr + 1) * LANES)
        s = lax.dot_general(_stack_head_pair(q_ref[j * sub:(j + 1) * sub, ps]), kcat[j * sub:j * sub + nk, ps],
                            (((1,), (1,)), ((), ())), preferred_element_type=F32)
        s = s + bias_ref[pr]
        return s + edge[j] if j in edge else s

    def finish(j, pr, s):
        ps = slice(pr * LANES, (pr + 1) * LANES)
        rows = slice(j * sub, (j + 1) * sub)
        m = jnp.max(s, axis=-1, keepdims=True)
        p = jnp.exp2(s - m)
        ol = jnp.dot(p.astype(BF16), vext[pr, j * sub:j * sub + nk, :], preferred_element_type=F32)
        o_ref[rows, ps] = _unstack_head_pair(ol[:, 0:LANES]).astype(o_ref.dtype)
        stat_ref[rows, 2 * pr:2 * pr + 1] = m[0:sub]
        stat_ref[rows, 2 * pr + 1:2 * pr + 2] = m[sub:2 * sub]
        for h, r0 in ((2 * pr, 0), (2 * pr + 1, sub)):
            lane = LANES + N_HEADS + h
            stat_ref[rows, N_HEADS + h:N_HEADS + h + 1] = ol[r0:r0 + sub, lane:lane + 1]

    pending = []
    for j in range(nsub):
        for pr in range(N_HEADS // 2):
            pending.append((j, pr, scores(j, pr)))
            if len(pending) > BAND_AHEAD:
                finish(*pending.pop(0))
    for unit in pending:
        finish(*unit)


def _band_attention(qkv, bias, seg_len):
    tq = TQ_BAND
    nt = SEQ // tq
    per = tq // BAND
    nhalo = SEQ // BAND
    left = lambda c: (lambda i: (jnp.maximum(i * per - 1, 0), c))
    right = lambda c: (lambda i: (jnp.minimum((i + 1) * per, nhalo - 1), c))
    main = lambda c: (lambda i: (i, c))
    kv_specs = []
    for c in (1, 2):
        kv_specs += [pl.BlockSpec((BAND, WIDTH), left(c)), pl.BlockSpec((tq, WIDTH), main(c)),
                     pl.BlockSpec((BAND, WIDTH), right(c))]
    nk = tq + 2 * BAND
    return pl.pallas_call(
        functools.partial(_band_kernel, seg_len=seg_len),
        grid=(nt,),
        in_specs=[pl.BlockSpec((tq, WIDTH), main(0))] + kv_specs
        + [pl.BlockSpec((N_HEADS // 2, 2 * SUB_BAND, SUB_BAND + 2 * BAND), lambda i: (0, 0, 0))],
        out_specs=[pl.BlockSpec((tq, WIDTH), lambda i: (i, 0)),
                   pl.BlockSpec((tq, LANES), lambda i: (i, 0))],
        out_shape=[jax.ShapeDtypeStruct((SEQ, WIDTH), BF16),
                   jax.ShapeDtypeStruct((SEQ, LANES), F32)],
        scratch_shapes=[pltpu.VMEM((nk, WIDTH), BF16), pltpu.VMEM((N_HEADS // 2, nk, 2 * LANES), BF16)],
        compiler_params=_params(("arbitrary",)),
        name=f"band_l{seg_len}",
    )(qkv, qkv, qkv, qkv, qkv, qkv, qkv, bias.reshape(N_HEADS // 2, 2 * SUB_BAND, SUB_BAND + 2 * BAND))


def _t5_bucket(rel):
    n = T5_BUCKETS // 2
    max_exact = n // 2
    sign_part = jnp.where(rel > 0, n, 0)
    a = jnp.abs(rel)
    af = jnp.maximum(a, 1).astype(F32)
    large = max_exact + (jnp.log(af / max_exact) / math.log(T5_MAX_DISTANCE / max_exact)
                         * (n - max_exact)).astype(jnp.int32)
    large = jnp.minimum(large, n - 1)
    return sign_part + jnp.where(a < max_exact, a, large)


def _band_bias(rel_bias, dilation):
    tq = SUB_BAND
    delta = jnp.arange(tq + 2 * BAND)[None, :] - BAND - jnp.arange(tq)[:, None]
    b = _lookup(rel_bias.astype(F32).T, _t5_bucket(delta * dilation), T5_BUCKETS)
    return jnp.where((jnp.abs(delta) <= BAND)[None], b * LOG2E, NEG)


def _lookup(table, idx, n):
    onehot = (jnp.arange(n)[:, None] == idx.reshape(1, -1)).astype(F32)
    out = jnp.dot(table.reshape(-1, n), onehot, precision=lax.Precision.HIGHEST)
    return out.reshape(table.shape[:-1] + idx.shape)


def _na_kernel(q_ref, kp_ref, kc_ref, kn_ref, vp_ref, vc_ref, vn_ref, btab_ref, *rest):
    n_w = (len(rest) - 4) // 2
    w_in_refs, o_ref, w_out_refs = rest[:n_w], rest[n_w], rest[n_w + 1:2 * n_w + 1]
    kcat, vext, bias_ref = rest[2 * n_w + 1:]
    for src, dst in zip(w_in_refs, w_out_refs):
        dst[...] = src[...].astype(dst.dtype)
    tile = q_ref.shape[0]
    halo = NA_KH * GRID_W
    i = pl.program_id(0)

    @pl.when(i == 0)
    def _():
        for v in range(NA_KH):
            for h in range(N_HEADS):
                for k in range(NA_KH // 2):
                    bias_ref[v, h // 2, (h % 2) * GRID_W:(h % 2 + 1) * GRID_W, k * LANES:(k + 1) * LANES] = (
                        btab_ref[h, v + 2 * k])
    _fill_value_ones(vext, i == 0)
    for rows, k_ref, v_ref in ((slice(0, halo), kp_ref, vp_ref), (slice(halo, halo + tile), kc_ref, vc_ref),
                               (slice(halo + tile, 2 * halo + tile), kn_ref, vn_ref)):
        kcat[rows, :] = k_ref[...]
        for pr in range(N_HEADS // 2):
            vext[pr, rows, 0:LANES] = v_ref[:, pr * LANES:(pr + 1) * LANES]
    nkeys = NA_KH * GRID_W

    def window(t):
        r = i * NA_TILE + t
        rs = jnp.clip(r - NA_KH // 2, 0, ROWS - NA_KH)
        return pl.multiple_of((rs - (i * NA_TILE - NA_KH)) * GRID_W, GRID_W), rs - r + (NA_KH - 1)

    def scores(t, pr, off, var):
        ps = slice(pr * LANES, (pr + 1) * LANES)
        s = lax.dot_general(_stack_head_pair(q_ref[t * GRID_W:(t + 1) * GRID_W, ps]), kcat[pl.ds(off, nkeys), ps],
                            (((1,), (1,)), ((), ())), preferred_element_type=F32)
        return s + bias_ref[var, pr]

    def finish(t, pr, off, s):
        ps = slice(pr * LANES, (pr + 1) * LANES)
        m = jnp.max(s, axis=-1, keepdims=True)
        p = jnp.exp2(s - m)
        ol = jnp.dot(p.astype(BF16), vext[pr, pl.ds(off, nkeys), :], preferred_element_type=F32)
        o = ol[:, 0:LANES] / ol[:, LANES:2 * LANES]
        o_ref[t * GRID_W:(t + 1) * GRID_W, ps] = _unstack_head_pair(o).astype(o_ref.dtype)

    units = [(t, pr) for t in range(NA_TILE) for pr in range(N_HEADS // 2)]
    wins = [window(t) for t in range(NA_TILE)]
    pending = []
    for t, pr in units:
        off, var = wins[t]
        pending.append((t, pr, off, scores(t, pr, off, var)))
        if len(pending) > NA_AHEAD:
            finish(*pending.pop(0))
    for unit in pending:
        finish(*unit)


def _na_attention(qkv, btab, weights):
    tile = NA_TILE * GRID_W
    halo = NA_KH * GRID_W
    nt = SEQ // tile
    per = tile // halo
    w_specs = []
    for w in weights:
        rows = next(r for r in (w.shape[0] // nt, LANES) if r % HALO == 0 and w.shape[0] % r == 0)
        nblk = w.shape[0] // rows
        assert nblk <= nt
        w_specs.append(pl.BlockSpec((rows, w.shape[1]), lambda i, nblk=nblk: (jnp.minimum(i, nblk - 1), 0)))
    prev = lambda c: (lambda i: (jnp.maximum(i * per - 1, 0), c))
    cur = lambda c: (lambda i: (i, c))
    nxt = lambda c: (lambda i: (jnp.minimum((i + 1) * per, SEQ // halo - 1), c))
    kv_specs = [pl.BlockSpec((rows, WIDTH), f(c)) for c in (1, 2)
                for rows, f in ((halo, prev), (tile, cur), (halo, nxt))]
    return pl.pallas_call(
        _na_kernel,
        grid=(nt,),
        in_specs=[pl.BlockSpec((tile, WIDTH), cur(0))] + kv_specs
        + [pl.BlockSpec(btab.shape, lambda i: (0, 0, 0, 0), pipeline_mode=pl.Buffered(1))] + w_specs,
        out_specs=[pl.BlockSpec((tile, WIDTH), lambda i: (i, 0))] + w_specs,
        out_shape=[jax.ShapeDtypeStruct((SEQ, WIDTH), BF16)]
        + [jax.ShapeDtypeStruct(w.shape, BF16) for w in weights],
        scratch_shapes=[pltpu.VMEM((tile + 2 * halo, WIDTH), BF16),
                        pltpu.VMEM((N_HEADS // 2, tile + 2 * halo, 2 * LANES), BF16),
                        pltpu.VMEM((NA_KH, N_HEADS // 2, 2 * GRID_W, NA_KH * GRID_W), F32)],
        compiler_params=_params(("arbitrary",)),
        name="na",
    )(qkv, qkv, qkv, qkv, qkv, qkv, qkv, btab, *weights)


def _na_bias(rpb):
    cq = jnp.arange(GRID_W)
    col_start = jnp.clip(cq - NA_COLS // 2, 0, GRID_W - NA_COLS)
    col_mask = (cq[None, :] >= col_start[:, None]) & (cq[None, :] < col_start[:, None] + NA_COLS)
    dc = jnp.clip(cq[None, :] - cq[:, None], -(NA_COLS - 1), NA_COLS - 1) + NA_COLS - 1
    e = _lookup(rpb.astype(F32), dc, 2 * NA_COLS - 1)
    e = jnp.where(col_mask[None, None], e * LOG2E, NEG)
    return jnp.concatenate([e[:, :-1], e[:, 1:]], axis=-1)


def _expand_heads(w, exp_ref):
    hi = w.astype(BF16)
    lo = (w - hi.astype(F32)).astype(BF16)
    return (jnp.dot(hi, exp_ref[...], preferred_element_type=F32)
            + jnp.dot(lo, exp_ref[...], preferred_element_type=F32))


def _merge_kernel(x_ref, o1_ref, o4_ref, o16_ref, l1_ref, l4_ref, l16_ref, ob_ref, g_ref,
                  wbd_ref, wbn_ref, wout_ref, exp_ref, gate_ref, mod_ref,
                  x1_ref, h2_ref, o4n, o16n, o16j, l4n, l16n, l16j, oa_scr):
    tm = x_ref.shape[0]
    for src4, src16, n4, j16, n16 in ((o4_ref, o16_ref, o4n, o16j, o16n), (l4_ref, l16_ref, l4n, l16j, l16n)):
        for t in range(n4.shape[0]):
            ls = slice(t * LANES, (t + 1) * LANES)
            for c in range(4):
                n4[t, pl.ds(c, tm // 4, stride=4), :] = src4[c, :, ls].astype(F32)
                for c2 in range(4):
                    j16[t, c, pl.ds(c2, tm // 16, stride=4), :] = src16[c + 4 * c2, :, ls].astype(F32)
                n16[t, pl.ds(c, tm // 4, stride=4), :] = j16[t, c]

    for sb in range(tm // SUB_MERGE):
        rows = slice(sb * SUB_MERGE, (sb + 1) * SUB_MERGE)
        stats = (l1_ref[rows, :], l4n[0, rows, :], l16n[0, rows, :])
        mx = jnp.maximum(jnp.maximum(stats[0], stats[1]), stats[2])
        es = [jnp.exp2(st - mx) for st in stats]
        den = sum(e * pltpu.roll(st, LANES - N_HEADS, 1) for e, st in zip(es, stats))
        is_head = lax.broadcasted_iota(jnp.int32, den.shape, 1) < N_HEADS
        w1, w4, w16 = (_expand_heads(jnp.where(is_head, e / den, 0.0), exp_ref) for e in es)
        for t in range(WIDTH // LANES):
            ls = slice(t * LANES, (t + 1) * LANES)
            oa = w1[:, ls] * o1_ref[rows, ls] + w4[:, ls] * o4n[t, rows, :] + w16[:, ls] * o16n[t, rows, :]
            oa_scr[sb, :, ls] = oa.astype(BF16)
        ya = jnp.dot(oa_scr[sb], wbd_ref[...], preferred_element_type=F32)
        yb = jnp.dot(ob_ref[rows, :], wbn_ref[...], preferred_element_type=F32)
        merged = (g_ref[rows, 0:D_MODEL].astype(F32) * ya + g_ref[rows, D_MODEL:2 * D_MODEL].astype(F32) * yb)
        z = jnp.dot(merged.astype(BF16), wout_ref[...], preferred_element_type=F32)
        x1 = x_ref[rows, :] + gate_ref[...] * z
        x1_ref[rows, :] = x1
        h2_ref[rows, :] = _rms_modulate(x1, mod_ref).astype(BF16)


def _merge(x, o1, o4, o16, l1, l4, l16, ob, g, wbd, wbn, wout, gate1, mod2):
    tm = TM_MERGE
    nt = SEQ // tm
    row = lambda w: pl.BlockSpec((tm, w), lambda i: (i, 0))
    cls = lambda d, w: pl.BlockSpec((d, tm // d, w), lambda i: (0, i, 0))
    const = lambda shape: pl.BlockSpec(shape, lambda i: (0, 0))
    spread = (jnp.arange(LANES)[:, None] == jnp.arange(WIDTH)[None, :] // HEAD_DIM).astype(BF16)
    return pl.pallas_call(
        _merge_kernel,
        grid=(nt,),
        in_specs=[row(D_MODEL), row(WIDTH), cls(4, WIDTH), cls(16, WIDTH),
                  row(LANES), cls(4, LANES), cls(16, LANES), row(WIDTH), row(2 * D_MODEL),
                  const((WIDTH, D_MODEL)), const((WIDTH, D_MODEL)), const((D_MODEL, D_MODEL)),
                  const((LANES, WIDTH)), const((1, D_MODEL)), const((3, D_MODEL))],
        out_specs=[row(D_MODEL), row(D_MODEL)],
        out_shape=[jax.ShapeDtypeStruct((SEQ, D_MODEL), F32), jax.ShapeDtypeStruct((SEQ, D_MODEL), BF16)],
        scratch_shapes=[pltpu.VMEM((WIDTH // LANES, tm, LANES), F32), pltpu.VMEM((WIDTH // LANES, tm, LANES), F32),
                        pltpu.VMEM((WIDTH // LANES, 4, tm // 4, LANES), F32),
                        pltpu.VMEM((1, tm, LANES), F32), pltpu.VMEM((1, tm, LANES), F32),
                        pltpu.VMEM((1, 4, tm // 4, LANES), F32),
                        pltpu.VMEM((tm // SUB_MERGE, SUB_MERGE, WIDTH), BF16)],
        compiler_params=_params(("arbitrary",)),
        name="merge",
    )(x, o1, o4.reshape(4, SEQ // 4, WIDTH), o16.reshape(16, SEQ // 16, WIDTH),
      l1, l4.reshape(4, SEQ // 4, LANES), l16.reshape(16, SEQ // 16, LANES), ob, g,
      wbd, wbn, wout, spread, gate1, mod2)


def _ffn_kernel(x1_ref, hm_ref, hp_ref, hn_ref, wup_ref, cw_ref, cb_ref, wd_ref, gate_ref, gfin_ref,
                y_ref, hext, act):
    tm = x1_ref.shape[0]
    n = tm + 2 * HALO
    i = pl.program_id(0)
    hext[0:HALO, :] = jnp.where(i == 0, jnp.zeros_like(hp_ref), hp_ref[...])
    hext[HALO:HALO + tm, :] = hm_ref[...]
    hext[HALO + tm:n, :] = jnp.where(i == pl.num_programs(0) - 1, jnp.zeros_like(hn_ref), hn_ref[...])

    def conv(cols):
        p = jnp.dot(hext[...], wup_ref[:, cols], preferred_element_type=F32)
        prev = pltpu.roll(p, 1, 0)[HALO:HALO + tm]
        nxt = pltpu.roll(p, n - 1, 0)[HALO:HALO + tm]
        return (cw_ref[0:1, cols] * prev + cw_ref[1:2, cols] * p[HALO:HALO + tm] + cw_ref[2:3, cols] * nxt
                + cb_ref[:, cols])

    for c in range(D_FF // TF_FFN):
        val = slice(c * TF_FFN, (c + 1) * TF_FFN)
        gate = slice(D_FF + c * TF_FFN, D_FF + (c + 1) * TF_FFN)
        act[:, val] = (jax.nn.gelu(conv(gate), approximate=True) * conv(val)).astype(BF16)

    x2 = x1_ref[...] + gate_ref[...] * jnp.dot(act[...], wd_ref[...], preferred_element_type=F32)
    y = x2 * lax.rsqrt(jnp.mean(x2 * x2, axis=-1, keepdims=True) + RMS_EPS)
    y_ref[...] = y * gfin_ref[...]


def _ffn(x1, h2, w_up, conv_w, conv_b, w_down, gate2, g_final):
    tm = TM_FFN
    nt = SEQ // tm
    per = tm // HALO
    nhalo = SEQ // HALO
    row = lambda: pl.BlockSpec((tm, D_MODEL), lambda i: (i, 0))
    const = lambda a: pl.BlockSpec(a.shape, lambda i: (0, 0), pipeline_mode=pl.Buffered(1))
    return pl.pallas_call(
        _ffn_kernel,
        grid=(nt,),
        in_specs=[
            row(), row(),
            pl.BlockSpec((HALO, D_MODEL), lambda i: (jnp.maximum(i * per - 1, 0), 0)),
            pl.BlockSpec((HALO, D_MODEL), lambda i: (jnp.minimum((i + 1) * per, nhalo - 1), 0)),
            const(w_up), const(conv_w), const(conv_b), const(w_down), const(gate2), const(g_final),
        ],
        out_specs=row(),
        out_shape=jax.ShapeDtypeStruct((SEQ, D_MODEL), F32),
        scratch_shapes=[pltpu.VMEM((tm + 2 * HALO, D_MODEL), BF16), pltpu.VMEM((tm, D_FF), BF16)],
        compiler_params=_params(("arbitrary",)),
        name="ffn",
    )(x1, h2, h2, h2, w_up, conv_w, conv_b, w_down, gate2, g_final)


def kernel(x, c, w_ada, b_ada, g_mix, w_in, rel_bias, na_rpb, w_branch_dil, w_branch_na, w_out, g_ffn,
           w_up, conv_w, conv_b, w_down, g_final):
    assert x.shape == (1, SEQ, D_MODEL) and w_ada.shape[0] == 1
    xs = x[0]
    ada = _ada(c, w_ada[0], b_ada[0])
    shift1, scale1, gate1, shift2, scale2, gate2 = [ada[:, k * D_MODEL:(k + 1) * D_MODEL] for k in range(6)]
    mod1 = jnp.concatenate([g_mix[0][None], scale1, shift1], axis=0)
    mod2 = jnp.concatenate([g_ffn[0][None], scale2, shift2], axis=0)

    a_nat, a_m4, a_m16, b_qkv, gates = _inproj(xs, mod1, w_in[0])

    outs = []
    for d, arr in zip(DILATIONS, (a_nat, a_m4, a_m16)):
        outs.append(_band_attention(arr.reshape(SEQ, 3 * WIDTH), _band_bias(rel_bias, d), SEQ // d))
    (o1, l1), (o4, l4), (o16, l16) = outs
    ob, wbd, wbn, wout, wup, wdown = _na_attention(
        b_qkv, _na_bias(na_rpb[0]), (w_branch_dil[0], w_branch_na[0], w_out[0], w_up[0], w_down[0]))

    x1, h2 = _merge(xs, o1, o4, o16, l1, l4, l16, ob, gates, wbd, wbn, wout, gate1, mod2)
    y = _ffn(x1, h2, wup, conv_w[0], conv_b[0].reshape(1, 2 * D_FF), wdown, gate2, g_final.reshape(1, D_MODEL))
    return y[None]
```

```python
import functools
import math

import jax
import jax.numpy as jnp
from jax import lax
from jax.experimental import pallas as pl
from jax.experimental.pallas import tpu as pltpu

F32 = jnp.float32
BF16 = jnp.bfloat16

D_MODEL = 1024
SEQ = 16384
GRID_W = 64
ROWS = SEQ // GRID_W
HEAD_DIM = 64
N_HEADS = 8
WIDTH = N_HEADS * HEAD_DIM
DILATIONS = (1, 4, 16)
BAND = 64
NA_KH = 8
NA_COLS = 16
NA_ROWS_MAX = 8
T5_BUCKETS = 32
T5_MAX_DISTANCE = 1024
D_FF = 2816
RMS_EPS = 1e-6
IN_COLS = 6 * WIDTH + 2 * D_MODEL
NEG = -1e30
LOG2E = math.log2(math.e)

VMEM_LIMIT = 56 * 1024 * 1024

TM_IN = 512
TQ_BAND = 2048
SUB_BAND = 128
TM_MERGE = 1024
SUB_MERGE = 256
TM_FFN = 1024
TF_FFN = 256
HALO = 16
LANES = 128
NA_AHEAD = 3
NA_TILE = 16
BAND_AHEAD = 3


def _params(sem):
    return pltpu.CompilerParams(dimension_semantics=sem, vmem_limit_bytes=VMEM_LIMIT)


def _ada_kernel(c_ref, w_ref, b_ref, o_ref):
    c = c_ref[...]
    s = c / (1.0 + jnp.exp(-c))
    o_ref[...] = jnp.sum(w_ref[...] * s, axis=0, keepdims=True) + b_ref[...]


def _ada(c, w_ada, b_ada):
    n_out = w_ada.shape[1]
    tn = D_MODEL
    return pl.pallas_call(
        _ada_kernel,
        grid=(n_out // tn,),
        in_specs=[
            pl.BlockSpec((D_MODEL, 1), lambda j: (0, 0)),
            pl.BlockSpec((D_MODEL, tn), lambda j: (0, j)),
            pl.BlockSpec((1, tn), lambda j: (0, j)),
        ],
        out_specs=pl.BlockSpec((1, tn), lambda j: (0, j)),
        out_shape=jax.ShapeDtypeStruct((1, n_out), F32),
        compiler_params=_params(("arbitrary",)),
        name="ada",
    )(c.reshape(D_MODEL, 1), w_ada, b_ada.reshape(1, n_out))


def _rms_modulate(x, mod_ref):
    y = x * lax.rsqrt(jnp.mean(x * x, axis=-1, keepdims=True) + RMS_EPS)
    return (y * mod_ref[0:1, :]) * (1.0 + mod_ref[1:2, :]) + mod_ref[2:3, :]


def _inproj_kernel(x0_ref, xn_ref, mod_ref, w_ref, anat_ref, am4_ref, am16_ref, b_ref,
                   h_even, h_odd, res_scr, res4_scr):
    tm = xn_ref.shape[0]
    i = pl.program_id(0)

    @pl.when(i == 0)
    def _():
        h_even[...] = _rms_modulate(x0_ref[...], mod_ref).astype(BF16)

    def project(h_cur, h_next):
        for k, n in enumerate((3, 4, 0, 1, 2, 5)):
            cols = slice(n * WIDTH, (n + 1) * WIDTH)
            w = w_ref[:, cols]
            if n in (0, 3):
                w = w * (HEAD_DIM ** -0.5 * LOG2E)
            r = jnp.dot(h_cur[...], w.astype(BF16), preferred_element_type=F32)
            if k == 1:
                h_next[...] = _rms_modulate(xn_ref[...], mod_ref).astype(BF16)
            if n < 3:
                anat_ref[:, cols] = r.astype(BF16)
                for t in range(WIDTH // LANES):
                    lanes = slice(n * WIDTH + t * LANES, n * WIDTH + (t + 1) * LANES)
                    res_scr[t] = r[:, t * LANES:(t + 1) * LANES]
                    for c in range(4):
                        cls4 = res_scr[t, pl.ds(c, tm // 4, stride=4), :]
                        am4_ref[c, :, lanes] = cls4.astype(BF16)
                        res4_scr[t, c] = cls4
                        for c2 in range(4):
                            am16_ref[c + 4 * c2, :, lanes] = (
                                res4_scr[t, c, pl.ds(c2, tm // 16, stride=4), :].astype(BF16))
            else:
                b_ref[:, (n - 3) * WIDTH:(n - 2) * WIDTH] = r.astype(BF16)

    @pl.when(i % 2 == 0)
    def _():
        project(h_even, h_odd)

    @pl.when(i % 2 == 1)
    def _():
        project(h_odd, h_even)


def _inproj(x, mod1, w_in):
    tm = TM_IN
    nt = SEQ // tm
    qkv = 3 * WIDTH
    return pl.pallas_call(
        _inproj_kernel,
        grid=(nt,),
        in_specs=[
            pl.BlockSpec((tm, D_MODEL), lambda i: (0, 0)),
            pl.BlockSpec((tm, D_MODEL), lambda i: (jnp.minimum(i + 1, nt - 1), 0)),
            pl.BlockSpec((3, D_MODEL), lambda i: (0, 0)),
            pl.BlockSpec((D_MODEL, 2 * qkv), lambda i: (0, 0), pipeline_mode=pl.Buffered(1)),
        ],
        out_specs=[
            pl.BlockSpec((tm, qkv), lambda i: (i, 0)),
            pl.BlockSpec((4, tm // 4, qkv), lambda i: (0, i, 0)),
            pl.BlockSpec((16, tm // 16, qkv), lambda i: (0, i, 0)),
            pl.BlockSpec((tm, qkv), lambda i: (i, 0)),
        ],
        out_shape=[
            jax.ShapeDtypeStruct((SEQ, qkv), BF16),
            jax.ShapeDtypeStruct((4, SEQ // 4, qkv), BF16),
            jax.ShapeDtypeStruct((16, SEQ // 16, qkv), BF16),
            jax.ShapeDtypeStruct((SEQ, qkv), BF16),
        ],
        scratch_shapes=[pltpu.VMEM((tm, D_MODEL), BF16), pltpu.VMEM((tm, D_MODEL), BF16),
                        pltpu.VMEM((WIDTH // LANES, tm, LANES), F32),
                        pltpu.VMEM((WIDTH // LANES, 4, tm // 4, LANES), F32)],
        compiler_params=_params(("arbitrary",)),
        name="inproj",
    )(x, x, mod1, w_in)


def _stack_head_pair(q):
    first = lax.broadcasted_iota(jnp.int32, q.shape, 1) < HEAD_DIM
    qf = q.astype(F32)
    return jnp.concatenate([jnp.where(first, qf, 0.0), jnp.where(first, 0.0, qf)], axis=0).astype(q.dtype)


def _unstack_head_pair(o):
    n = o.shape[0] // 2
    first = lax.broadcasted_iota(jnp.int32, (n, o.shape[1]), 1) < HEAD_DIM
    return jnp.where(first, o[0:n], o[n:2 * n])


def _fill_value_ones(vext, first_step):
    @pl.when(first_step)
    def _():
        vext[:, :, LANES:2 * LANES] = jnp.ones(vext.shape[:2] + (LANES,), vext.dtype)


def _band_kernel(q_ref, kl_ref, km_ref, kr_ref, vl_ref, vm_ref, vr_ref, bias_ref,
                 o_ref, stat_ref, kcat, vext, *, seg_len):
    tq = q_ref.shape[0]
    sub = SUB_BAND
    nsub = tq // sub
    nk = sub + 2 * BAND
    start = pl.program_id(0) * tq
    kcat[0:BAND, :] = kl_ref[...]
    kcat[BAND:BAND + tq, :] = km_ref[...]
    kcat[BAND + tq:tq + 2 * BAND, :] = kr_ref[...]
    _fill_value_ones(vext, pl.program_id(0) == 0)
    for pr in range(N_HEADS // 2):
        ps = slice(pr * LANES, (pr + 1) * LANES)
        vext[pr, 0:BAND, 0:LANES] = vl_ref[:, ps]
        vext[pr, BAND:BAND + tq, 0:LANES] = vm_ref[:, ps]
        vext[pr, BAND + tq:tq + 2 * BAND, 0:LANES] = vr_ref[:, ps]
    col = lax.broadcasted_iota(jnp.int32, (1, nk), 1)
    period = min(tq, seg_len)
    edge = {}
    for j in range(nsub):
        masks = []
        if (j * sub) % period == 0:
            masks.append((col < BAND) & (((start + j * sub) % seg_len) == 0))
        if ((j + 1) * sub) % period == 0:
            masks.append((col >= BAND + sub) & (((start + (j + 1) * sub) % seg_len) == 0))
        if masks:
            edge[j] = jnp.where(functools.reduce(jnp.logical_or, masks), NEG, 0.0)
    stat_ref[...] = jnp.zeros_like(stat_ref)

    def scores(j, pr):
        ps = slice(pr * LANES, (pr + 1) * LANES)
        s = lax.dot_general(_stack_head_pair(q_ref[j * sub:(j + 1) * sub, ps]), kcat[j * sub:j * sub + nk, ps],
                            (((1,), (1,)), ((), ())), preferred_element_type=F32)
        s = s + bias_ref[pr]
        return s + edge[j] if j in edge else s

    def finish(j, pr, s):
        ps = slice(pr * LANES, (pr + 1) * LANES)
        rows = slice(j * sub, (j + 1) * sub)
        m = jnp.max(s, axis=-1, keepdims=True)
        p = jnp.exp2(s - m)
        ol = jnp.dot(p.astype(BF16), vext[pr, j * sub:j * sub + nk, :], preferred_element_type=F32)
        o_ref[rows, ps] = _unstack_head_pair(ol[:, 0:LANES]).astype(o_ref.dtype)
        stat_ref[rows, 2 * pr:2 * pr + 1] = m[0:sub]
        stat_ref[rows, 2 * pr + 1:2 * pr + 2] = m[sub:2 * sub]
        for h, r0 in ((2 * pr, 0), (2 * pr + 1, sub)):
            lane = LANES + N_HEADS + h
            stat_ref[rows, N_HEADS + h:N_HEADS + h + 1] = ol[r0:r0 + sub, lane:lane + 1]

    pending = []
    for j in range(nsub):
        for pr in range(N_HEADS // 2):
            pending.append((j, pr, scores(j, pr)))
            if len(pending) > BAND_AHEAD:
                finish(*pending.pop(0))
    for unit in pending:
        finish(*unit)


def _band_attention(qkv, bias, seg_len):
    tq = TQ_BAND
    nt = SEQ // tq
    per = tq // BAND
    nhalo = SEQ // BAND
    left = lambda c: (lambda i: (jnp.maximum(i * per - 1, 0), c))
    right = lambda c: (lambda i: (jnp.minimum((i + 1) * per, nhalo - 1), c))
    main = lambda c: (lambda i: (i, c))
    kv_specs = []
    for c in (1, 2):
        kv_specs += [pl.BlockSpec((BAND, WIDTH), left(c)), pl.BlockSpec((tq, WIDTH), main(c)),
                     pl.BlockSpec((BAND, WIDTH), right(c))]
    nk = tq + 2 * BAND
    return pl.pallas_call(
        functools.partial(_band_kernel, seg_len=seg_len),
        grid=(nt,),
        in_specs=[pl.BlockSpec((tq, WIDTH), main(0))] + kv_specs
        + [pl.BlockSpec((N_HEADS // 2, 2 * SUB_BAND, SUB_BAND + 2 * BAND), lambda i: (0, 0, 0))],
        out_specs=[pl.BlockSpec((tq, WIDTH), lambda i: (i, 0)),
                   pl.BlockSpec((tq, LANES), lambda i: (i, 0))],
        out_shape=[jax.ShapeDtypeStruct((SEQ, WIDTH), BF16),
                   jax.ShapeDtypeStruct((SEQ, LANES), F32)],
        scratch_shapes=[pltpu.VMEM((nk, WIDTH), BF16), pltpu.VMEM((N_HEADS // 2, nk, 2 * LANES), BF16)],
        compiler_params=_params(("arbitrary",)),
        name=f"band_l{seg_len}",
    )(qkv, qkv, qkv, qkv, qkv, qkv, qkv, bias.reshape(N_HEADS // 2, 2 * SUB_BAND, SUB_BAND + 2 * BAND))


def _t5_bucket(rel):
    n = T5_BUCKETS // 2
    max_exact = n // 2
    sign_part = jnp.where(rel > 0, n, 0)
    a = jnp.abs(rel)
    af = jnp.maximum(a, 1).astype(F32)
    large = max_exact + (jnp.log(af / max_exact) / math.log(T5_MAX_DISTANCE / max_exact)
                         * (n - max_exact)).astype(jnp.int32)
    large = jnp.minimum(large, n - 1)
    return sign_part + jnp.where(a < max_exact, a, large)


def _band_bias(rel_bias, dilation):
    tq = SUB_BAND
    delta = jnp.arange(tq + 2 * BAND)[None, :] - BAND - jnp.arange(tq)[:, None]
    b = _lookup(rel_bias.astype(F32).T, _t5_bucket(delta * dilation), T5_BUCKETS)
    return jnp.where((jnp.abs(delta) <= BAND)[None], b * LOG2E, NEG)


def _lookup(table, idx, n):
    onehot = (jnp.arange(n)[:, None] == idx.reshape(1, -1)).astype(F32)
    out = jnp.dot(table.reshape(-1, n), onehot, precision=lax.Precision.HIGHEST)
    return out.reshape(table.shape[:-1] + idx.shape)


def _na_kernel(q_ref, kp_ref, kc_ref, kn_ref, vp_ref, vc_ref, vn_ref, btab_ref, *rest):
    n_w = (len(rest) - 4) // 2
    w_in_refs, o_ref, w_out_refs = rest[:n_w], rest[n_w], rest[n_w + 1:2 * n_w + 1]
    kcat, vext, bias_ref = rest[2 * n_w + 1:]
    for src, dst in zip(w_in_refs, w_out_refs):
        dst[...] = src[...].astype(dst.dtype)
    tile = q_ref.shape[0]
    halo = NA_KH * GRID_W
    i = pl.program_id(0)

    @pl.when(i == 0)
    def _():
        for v in range(NA_KH):
            for h in range(N_HEADS):
                for k in range(NA_KH // 2):
                    bias_ref[v, h // 2, (h % 2) * GRID_W:(h % 2 + 1) * GRID_W, k * LANES:(k + 1) * LANES] = (
                        btab_ref[h, v + 2 * k])
    _fill_value_ones(vext, i == 0)
    for rows, k_ref, v_ref in ((slice(0, halo), kp_ref, vp_ref), (slice(halo, halo + tile), kc_ref, vc_ref),
                               (slice(halo + tile, 2 * halo + tile), kn_ref, vn_ref)):
        kcat[rows, :] = k_ref[...]
        for pr in range(N_HEADS // 2):
            vext[pr, rows, 0:LANES] = v_ref[:, pr * LANES:(pr + 1) * LANES]
    nkeys = NA_KH * GRID_W

    def window(t):
        r = i * NA_TILE + t
        rs = jnp.clip(r - NA_KH // 2, 0, ROWS - NA_KH)
        return pl.multiple_of((rs - (i * NA_TILE - NA_KH)) * GRID_W, GRID_W), rs - r + (NA_KH - 1)

    def scores(t, pr, off, var):
        ps = slice(pr * LANES, (pr + 1) * LANES)
        s = lax.dot_general(_stack_head_pair(q_ref[t * GRID_W:(t + 1) * GRID_W, ps]), kcat[pl.ds(off, nkeys), ps],
                            (((1,), (1,)), ((), ())), preferred_element_type=F32)
        return s + bias_ref[var, pr]

    def finish(t, pr, off, s):
        ps = slice(pr * LANES, (pr + 1) * LANES)
        m = jnp.max(s, axis=-1, keepdims=True)
        p = jnp.exp2(s - m)
        ol = jnp.dot(p.astype(BF16), vext[pr, pl.ds(off, nkeys), :], preferred_element_type=F32)
        o = ol[:, 0:LANES] / ol[:, LANES:2 * LANES]
        o_ref[t * GRID_W:(t + 1) * GRID_W, ps] = _unstack_head_pair(o).astype(o_ref.dtype)

    units = [(t, pr) for t in range(NA_TILE) for pr in range(N_HEADS // 2)]
    wins = [window(t) for t in range(NA_TILE)]
    pending = []
    for t, pr in units:
        off, var = wins[t]
        pending.append((t, pr, off, scores(t, pr, off, var)))
        if len(pending) > NA_AHEAD:
            finish(*pending.pop(0))
    for unit in pending:
        finish(*unit)


def _na_attention(qkv, btab, weights):
    tile = NA_TILE * GRID_W
    halo = NA_KH * GRID_W
    nt = SEQ // tile
    per = tile // halo
    w_specs = []
    for w in weights:
        rows = next(r for r in (w.shape[0] // nt, LANES) if r % HALO == 0 and w.shape[0] % r == 0)
        nblk = w.shape[0] // rows
        assert nblk <= nt
        w_specs.append(pl.BlockSpec((rows, w.shape[1]), lambda i, nblk=nblk: (jnp.minimum(i, nblk - 1), 0)))
    prev = lambda c: (lambda i: (jnp.maximum(i * per - 1, 0), c))
    cur = lambda c: (lambda i: (i, c))
    nxt = lambda c: (lambda i: (jnp.minimum((i + 1) * per, SEQ // halo - 1), c))
    kv_specs = [pl.BlockSpec((rows, WIDTH), f(c)) for c in (1, 2)
                for rows, f in ((halo, prev), (tile, cur), (halo, nxt))]
    return pl.pallas_call(
        _na_kernel,
        grid=(nt,),
        in_specs=[pl.BlockSpec((tile, WIDTH), cur(0))] + kv_specs
        + [pl.BlockSpec(btab.shape, lambda i: (0, 0, 0, 0), pipeline_mode=pl.Buffered(1))] + w_specs,
        out_specs=[pl.BlockSpec((tile, WIDTH), lambda i: (i, 0))] + w_specs,
        out_shape=[jax.ShapeDtypeStruct((SEQ, WIDTH), BF16)]
        + [jax.ShapeDtypeStruct(w.shape, BF16) for w in weights],
        scratch_shapes=[pltpu.VMEM((tile + 2 * halo, WIDTH), BF16),
                        pltpu.VMEM((N_HEADS // 2, tile + 2 * halo, 2 * LANES), BF16),
                        pltpu.VMEM((NA_KH, N_HEADS // 2, 2 * GRID_W, NA_KH * GRID_W), F32)],
        compiler_params=_params(("arbitrary",)),
        name="na",
    )(qkv, qkv, qkv, qkv, qkv, qkv, qkv, btab, *weights)


def _na_bias(rpb):
    cq = jnp.arange(GRID_W)
    col_start = jnp.clip(cq - NA_COLS // 2, 0, GRID_W - NA_COLS)
    col_mask = (cq[None, :] >= col_start[:, None]) & (cq[None, :] < col_start[:, None] + NA_COLS)
    dc = jnp.clip(cq[None, :] - cq[:, None], -(NA_COLS - 1), NA_COLS - 1) + NA_COLS - 1
    e = _lookup(rpb.astype(F32), dc, 2 * NA_COLS - 1)
    e = jnp.where(col_mask[None, None], e * LOG2E, NEG)
    return jnp.concatenate([e[:, :-1], e[:, 1:]], axis=-1)


def _expand_heads(w, exp_ref):
    hi = w.astype(BF16)
    lo = (w - hi.astype(F32)).astype(BF16)
    return (jnp.dot(hi, exp_ref[...], preferred_element_type=F32)
            + jnp.dot(lo, exp_ref[...], preferred_element_type=F32))


def _merge_kernel(x_ref, o1_ref, o4_ref, o16_ref, l1_ref, l4_ref, l16_ref, ob_ref, mod1_ref, wga_ref, wgb_ref,
                  wbd_ref, wbn_ref, wout_ref, exp_ref, gate_ref, mod_ref,
                  x1_ref, h2_ref, o4n, o16n, o16j, l4n, l16n, l16j, oa_scr):
    tm = x_ref.shape[0]
    for src4, src16, n4, j16, n16 in ((o4_ref, o16_ref, o4n, o16j, o16n), (l4_ref, l16_ref, l4n, l16j, l16n)):
        for t in range(n4.shape[0]):
            ls = slice(t * LANES, (t + 1) * LANES)
            for c in range(4):
                n4[t, pl.ds(c, tm // 4, stride=4), :] = src4[c, :, ls].astype(F32)
                for c2 in range(4):
                    j16[t, c, pl.ds(c2, tm // 16, stride=4), :] = src16[c + 4 * c2, :, ls].astype(F32)
                n16[t, pl.ds(c, tm // 4, stride=4), :] = j16[t, c]

    for sb in range(tm // SUB_MERGE):
        rows = slice(sb * SUB_MERGE, (sb + 1) * SUB_MERGE)
        stats = (l1_ref[rows, :], l4n[0, rows, :], l16n[0, rows, :])
        mx = jnp.maximum(jnp.maximum(stats[0], stats[1]), stats[2])
        es = [jnp.exp2(st - mx) for st in stats]
        den = sum(e * pltpu.roll(st, LANES - N_HEADS, 1) for e, st in zip(es, stats))
        is_head = lax.broadcasted_iota(jnp.int32, den.shape, 1) < N_HEADS
        w1, w4, w16 = (_expand_heads(jnp.where(is_head, e / den, 0.0), exp_ref) for e in es)
        for t in range(WIDTH // LANES):
            ls = slice(t * LANES, (t + 1) * LANES)
            oa = w1[:, ls] * o1_ref[rows, ls] + w4[:, ls] * o4n[t, rows, :] + w16[:, ls] * o16n[t, rows, :]
            oa_scr[sb, :, ls] = oa.astype(BF16)
        h = _rms_modulate(x_ref[rows, :], mod1_ref).astype(BF16)
        ga = jnp.dot(h, wga_ref[...], preferred_element_type=F32)
        ya = jnp.dot(oa_scr[sb], wbd_ref[...], preferred_element_type=F32)
        merged = ya / (1.0 + jnp.exp(-ga))
        gb = jnp.dot(h, wgb_ref[...], preferred_element_type=F32)
        yb = jnp.dot(ob_ref[rows, :], wbn_ref[...], preferred_element_type=F32)
        merged = merged + yb / (1.0 + jnp.exp(-gb))
        z = jnp.dot(merged.astype(BF16), wout_ref[...], preferred_element_type=F32)
        x1 = x_ref[rows, :] + gate_ref[...] * z
        x1_ref[rows, :] = x1
        h2_ref[rows, :] = _rms_modulate(x1, mod_ref).astype(BF16)


def _merge(x, o1, o4, o16, l1, l4, l16, ob, mod1, w_in_b, wbd, wbn, wout, gate1, mod2):
    tm = TM_MERGE
    nt = SEQ // tm
    row = lambda w: pl.BlockSpec((tm, w), lambda i: (i, 0))
    cls = lambda d, w: pl.BlockSpec((d, tm // d, w), lambda i: (0, i, 0))
    const = lambda shape: pl.BlockSpec(shape, lambda i: (0, 0))
    spread = (jnp.arange(LANES)[:, None] == jnp.arange(WIDTH)[None, :] // HEAD_DIM).astype(BF16)
    return pl.pallas_call(
        _merge_kernel,
        grid=(nt,),
        in_specs=[row(D_MODEL), row(WIDTH), cls(4, WIDTH), cls(16, WIDTH),
                  row(LANES), cls(4, LANES), cls(16, LANES), row(WIDTH), const((3, D_MODEL)),
                  pl.BlockSpec((D_MODEL, D_MODEL), lambda i: (0, 6 * WIDTH // D_MODEL)),
                  pl.BlockSpec((D_MODEL, D_MODEL), lambda i: (0, 6 * WIDTH // D_MODEL + 1)),
                  const((WIDTH, D_MODEL)), const((WIDTH, D_MODEL)), const((D_MODEL, D_MODEL)),
                  const((LANES, WIDTH)), const((1, D_MODEL)), const((3, D_MODEL))],
        out_specs=[row(D_MODEL), row(D_MODEL)],
        out_shape=[jax.ShapeDtypeStruct((SEQ, D_MODEL), F32), jax.ShapeDtypeStruct((SEQ, D_MODEL), BF16)],
        scratch_shapes=[pltpu.VMEM((WIDTH // LANES, tm, LANES), F32), pltpu.VMEM((WIDTH // LANES, tm, LANES), F32),
                        pltpu.VMEM((WIDTH // LANES, 4, tm // 4, LANES), F32),
                        pltpu.VMEM((1, tm, LANES), F32), pltpu.VMEM((1, tm, LANES), F32),
                        pltpu.VMEM((1, 4, tm // 4, LANES), F32),
                        pltpu.VMEM((tm // SUB_MERGE, SUB_MERGE, WIDTH), BF16)],
        compiler_params=_params(("arbitrary",)),
        name="merge",
    )(x, o1, o4.reshape(4, SEQ // 4, WIDTH), o16.reshape(16, SEQ // 16, WIDTH),
      l1, l4.reshape(4, SEQ // 4, LANES), l16.reshape(16, SEQ // 16, LANES), ob, mod1, w_in_b, w_in_b,
      wbd, wbn, wout, spread, gate1, mod2)


def _ffn_kernel(x1_ref, hm_ref, hp_ref, hn_ref, wup_ref, cw_ref, cb_ref, wd_ref, gate_ref, gfin_ref,
                y_ref, hext, act):
    tm = x1_ref.shape[0]
    n = tm + 2 * HALO
    i = pl.program_id(0)
    hext[0:HALO, :] = jnp.where(i == 0, jnp.zeros_like(hp_ref), hp_ref[...])
    hext[HALO:HALO + tm, :] = hm_ref[...]
    hext[HALO + tm:n, :] = jnp.where(i == pl.num_programs(0) - 1, jnp.zeros_like(hn_ref), hn_ref[...])

    def conv(cols):
        p = jnp.dot(hext[...], wup_ref[:, cols], preferred_element_type=F32)
        prev = pltpu.roll(p, 1, 0)[HALO:HALO + tm]
        nxt = pltpu.roll(p, n - 1, 0)[HALO:HALO + tm]
        return (cw_ref[0:1, cols] * prev + cw_ref[1:2, cols] * p[HALO:HALO + tm] + cw_ref[2:3, cols] * nxt
                + cb_ref[:, cols])

    for c in range(D_FF // TF_FFN):
        val = slice(c * TF_FFN, (c + 1) * TF_FFN)
        gate = slice(D_FF + c * TF_FFN, D_FF + (c + 1) * TF_FFN)
        act[:, val] = (jax.nn.gelu(conv(gate), approximate=True) * conv(val)).astype(BF16)

    x2 = x1_ref[...] + gate_ref[...] * jnp.dot(act[...], wd_ref[...], preferred_element_type=F32)
    y = x2 * lax.rsqrt(jnp.mean(x2 * x2, axis=-1, keepdims=True) + RMS_EPS)
    y_ref[...] = y * gfin_ref[...]


def _ffn(x1, h2, w_up, conv_w, conv_b, w_down, gate2, g_final):
    tm = TM_FFN
    nt = SEQ // tm
    per = tm // HALO
    nhalo = SEQ // HALO
    row = lambda: pl.BlockSpec((tm, D_MODEL), lambda i: (i, 0))
    const = lambda a: pl.BlockSpec(a.shape, lambda i: (0, 0), pipeline_mode=pl.Buffered(1))
    return pl.pallas_call(
        _ffn_kernel,
        grid=(nt,),
        in_specs=[
            row(), row(),
            pl.BlockSpec((HALO, D_MODEL), lambda i: (jnp.maximum(i * per - 1, 0), 0)),
            pl.BlockSpec((HALO, D_MODEL), lambda i: (jnp.minimum((i + 1) * per, nhalo - 1), 0)),
            const(w_up), const(conv_w), const(conv_b), const(w_down), const(gate2), const(g_final),
        ],
        out_specs=row(),
        out_shape=jax.ShapeDtypeStruct((SEQ, D_MODEL), F32),
        scratch_shapes=[pltpu.VMEM((tm + 2 * HALO, D_MODEL), BF16), pltpu.VMEM((tm, D_FF), BF16)],
        compiler_params=_params(("arbitrary",)),
        name="ffn",
    )(x1, h2, h2, h2, w_up, conv_w, conv_b, w_down, gate2, g_final)


def kernel(x, c, w_ada, b_ada, g_mix, w_in, rel_bias, na_rpb, w_branch_dil, w_branch_na, w_out, g_ffn,
           w_up, conv_w, conv_b, w_down, g_final):
    assert x.shape == (1, SEQ, D_MODEL) and w_ada.shape[0] == 1
    xs = x[0]
    ada = _ada(c, w_ada[0], b_ada[0])
    shift1, scale1, gate1, shift2, scale2, gate2 = [ada[:, k * D_MODEL:(k + 1) * D_MODEL] for k in range(6)]
    mod1 = jnp.concatenate([g_mix[0][None], scale1, shift1], axis=0)
    mod2 = jnp.concatenate([g_ffn[0][None], scale2, shift2], axis=0)

    a_nat, a_m4, a_m16, b_qkv = _inproj(xs, mod1, w_in[0])

    outs = []
    for d, arr in zip(DILATIONS, (a_nat, a_m4, a_m16)):
        outs.append(_band_attention(arr.reshape(SEQ, 3 * WIDTH), _band_bias(rel_bias, d), SEQ // d))
    (o1, l1), (o4, l4), (o16, l16) = outs
    ob, wbd, wbn, wout, wup, wdown, w_in_b = _na_attention(
        b_qkv, _na_bias(na_rpb[0]), (w_branch_dil[0], w_branch_na[0], w_out[0], w_up[0], w_down[0], w_in[0]))

    x1, h2 = _merge(xs, o1, o4, o16, l1, l4, l16, ob, mod1, w_in_b, wbd, wbn, wout, gate1, mod2)
    y = _ffn(x1, h2, wup, conv_w[0], conv_b[0].reshape(1, 2 * D_FF), wdown, gate2, g_final.reshape(1, D_MODEL))
    return y[None]
```

```python
import functools
import math

import jax
import jax.numpy as jnp
from jax import lax
from jax.experimental import pallas as pl
from jax.experimental.pallas import tpu as pltpu

F32 = jnp.float32
BF16 = jnp.bfloat16

D_MODEL = 1024
SEQ = 16384
GRID_W = 64
ROWS = SEQ // GRID_W
HEAD_DIM = 64
N_HEADS = 8
WIDTH = N_HEADS * HEAD_DIM
DILATIONS = (1, 4, 16)
BAND = 64
NA_KH = 8
NA_COLS = 16
NA_ROWS_MAX = 8
T5_BUCKETS = 32
T5_MAX_DISTANCE = 1024
D_FF = 2816
RMS_EPS = 1e-6
IN_COLS = 6 * WIDTH + 2 * D_MODEL
NEG = -1e30
LOG2E = math.log2(math.e)

VMEM_LIMIT = 56 * 1024 * 1024

TM_IN = 512
TQ_BAND = 1024
SUB_BAND = 128
TM_MERGE = 1024
SUB_MERGE = 256
TM_FFN = 1024
TF_FFN = 256
HALO = 16
LANES = 128
NA_AHEAD = 3
NA_TILE = 16
BAND_AHEAD = 3


def _params(sem):
    return pltpu.CompilerParams(dimension_semantics=sem, vmem_limit_bytes=VMEM_LIMIT)


def _ada_kernel(c_ref, w_ref, b_ref, o_ref):
    c = c_ref[...]
    s = c / (1.0 + jnp.exp(-c))
    o_ref[...] = jnp.sum(w_ref[...] * s, axis=0, keepdims=True) + b_ref[...]


def _ada(c, w_ada, b_ada):
    n_out = w_ada.shape[1]
    tn = D_MODEL
    return pl.pallas_call(
        _ada_kernel,
        grid=(n_out // tn,),
        in_specs=[
            pl.BlockSpec((D_MODEL, 1), lambda j: (0, 0)),
            pl.BlockSpec((D_MODEL, tn), lambda j: (0, j)),
            pl.BlockSpec((1, tn), lambda j: (0, j)),
        ],
        out_specs=pl.BlockSpec((1, tn), lambda j: (0, j)),
        out_shape=jax.ShapeDtypeStruct((1, n_out), F32),
        compiler_params=_params(("arbitrary",)),
        name="ada",
    )(c.reshape(D_MODEL, 1), w_ada, b_ada.reshape(1, n_out))


def _rms_modulate(x, mod_ref):
    y = x * lax.rsqrt(jnp.mean(x * x, axis=-1, keepdims=True) + RMS_EPS)
    return (y * mod_ref[0:1, :]) * (1.0 + mod_ref[1:2, :]) + mod_ref[2:3, :]


def _inproj_kernel(x0_ref, xn_ref, mod_ref, w_ref, anat_ref, am4_ref, am16_ref, b_ref,
                   h_even, h_odd, res_scr, res4_scr):
    tm = xn_ref.shape[0]
    i = pl.program_id(0)

    @pl.when(i == 0)
    def _():
        h_even[...] = _rms_modulate(x0_ref[...], mod_ref).astype(BF16)

    def project(h_cur, h_next):
        for k, n in enumerate((3, 4, 0, 1, 2, 5)):
            cols = slice(n * WIDTH, (n + 1) * WIDTH)
            w = w_ref[:, cols]
            if n in (0, 3):
                w = w * (HEAD_DIM ** -0.5 * LOG2E)
            r = jnp.dot(h_cur[...], w.astype(BF16), preferred_element_type=F32)
            if k == 1:
                h_next[...] = _rms_modulate(xn_ref[...], mod_ref).astype(BF16)
            if n < 3:
                anat_ref[:, cols] = r.astype(BF16)
                for t in range(WIDTH // LANES):
                    lanes = slice(n * WIDTH + t * LANES, n * WIDTH + (t + 1) * LANES)
                    res_scr[t] = r[:, t * LANES:(t + 1) * LANES]
                    for c in range(4):
                        cls4 = res_scr[t, pl.ds(c, tm // 4, stride=4), :]
                        am4_ref[c, :, lanes] = cls4.astype(BF16)
                        res4_scr[t, c] = cls4
                        for c2 in range(4):
                            am16_ref[c + 4 * c2, :, lanes] = (
                                res4_scr[t, c, pl.ds(c2, tm // 16, stride=4), :].astype(BF16))
            else:
                b_ref[:, (n - 3) * WIDTH:(n - 2) * WIDTH] = r.astype(BF16)

    @pl.when(i % 2 == 0)
    def _():
        project(h_even, h_odd)

    @pl.when(i % 2 == 1)
    def _():
        project(h_odd, h_even)


def _inproj(x, mod1, w_in):
    tm = TM_IN
    nt = SEQ // tm
    qkv = 3 * WIDTH
    return pl.pallas_call(
        _inproj_kernel,
        grid=(nt,),
        in_specs=[
            pl.BlockSpec((tm, D_MODEL), lambda i: (0, 0)),
            pl.BlockSpec((tm, D_MODEL), lambda i: (jnp.minimum(i + 1, nt - 1), 0)),
            pl.BlockSpec((3, D_MODEL), lambda i: (0, 0)),
            pl.BlockSpec((D_MODEL, 2 * qkv), lambda i: (0, 0), pipeline_mode=pl.Buffered(1)),
        ],
        out_specs=[
            pl.BlockSpec((tm, qkv), lambda i: (i, 0)),
            pl.BlockSpec((4, tm // 4, qkv), lambda i: (0, i, 0)),
            pl.BlockSpec((16, tm // 16, qkv), lambda i: (0, i, 0)),
            pl.BlockSpec((tm, qkv), lambda i: (i, 0)),
        ],
        out_shape=[
            jax.ShapeDtypeStruct((SEQ, qkv), BF16),
            jax.ShapeDtypeStruct((4, SEQ // 4, qkv), BF16),
            jax.ShapeDtypeStruct((16, SEQ // 16, qkv), BF16),
            jax.ShapeDtypeStruct((SEQ, qkv), BF16),
        ],
        scratch_shapes=[pltpu.VMEM((tm, D_MODEL), BF16), pltpu.VMEM((tm, D_MODEL), BF16),
                        pltpu.VMEM((WIDTH // LANES, tm, LANES), F32),
                        pltpu.VMEM((WIDTH // LANES, 4, tm // 4, LANES), F32)],
        compiler_params=_params(("arbitrary",)),
        name="inproj",
    )(x, x, mod1, w_in)


def _stack_head_pair(q):
    first = lax.broadcasted_iota(jnp.int32, q.shape, 1) < HEAD_DIM
    qf = q.astype(F32)
    return jnp.concatenate([jnp.where(first, qf, 0.0), jnp.where(first, 0.0, qf)], axis=0).astype(q.dtype)


def _unstack_head_pair(o):
    n = o.shape[0] // 2
    first = lax.broadcasted_iota(jnp.int32, (n, o.shape[1]), 1) < HEAD_DIM
    return jnp.where(first, o[0:n], o[n:2 * n])


def _fill_value_ones(vext, first_step):
    @pl.when(first_step)
    def _():
        vext[:, :, LANES:2 * LANES] = jnp.ones(vext.shape[:2] + (LANES,), vext.dtype)


def _band_kernel(q_ref, kl_ref, km_ref, kr_ref, vl_ref, vm_ref, vr_ref, bias_ref,
                 o_ref, stat_ref, kcat, vext, *, seg_len):
    tq = q_ref.shape[0]
    sub = SUB_BAND
    nsub = tq // sub
    nk = sub + 2 * BAND
    start = pl.program_id(0) * tq
    kcat[0:BAND, :] = kl_ref[...]
    kcat[BAND:BAND + tq, :] = km_ref[...]
    kcat[BAND + tq:tq + 2 * BAND, :] = kr_ref[...]
    _fill_value_ones(vext, pl.program_id(0) == 0)
    for pr in range(N_HEADS // 2):
        ps = slice(pr * LANES, (pr + 1) * LANES)
        vext[pr, 0:BAND, 0:LANES] = vl_ref[:, ps]
        vext[pr, BAND:BAND + tq, 0:LANES] = vm_ref[:, ps]
        vext[pr, BAND + tq:tq + 2 * BAND, 0:LANES] = vr_ref[:, ps]
    col = lax.broadcasted_iota(jnp.int32, (1, nk), 1)
    period = min(tq, seg_len)
    edge = {}
    for j in range(nsub):
        masks = []
        if (j * sub) % period == 0:
            masks.append((col < BAND) & (((start + j * sub) % seg_len) == 0))
        if ((j + 1) * sub) % period == 0:
            masks.append((col >= BAND + sub) & (((start + (j + 1) * sub) % seg_len) == 0))
        if masks:
            edge[j] = jnp.where(functools.reduce(jnp.logical_or, masks), NEG, 0.0)
    stat_ref[...] = jnp.zeros_like(stat_ref)

    def scores(j, pr):
        ps = slice(pr * LANES, (pr + 1) * LANES)
        s = lax.dot_general(_stack_head_pair(q_ref[j * sub:(j + 1) * sub, ps]), kcat[j * sub:j * sub + nk, ps],
                            (((1,), (1,)), ((), ())), preferred_element_type=F32)
        s = s + bias_ref[pr]
        return s + edge[j] if j in edge else s

    def finish(j, pr, s):
        ps = slice(pr * LANES, (pr + 1) * LANES)
        rows = slice(j * sub, (j + 1) * sub)
        m = jnp.max(s, axis=-1, keepdims=True)
        p = jnp.exp2(s - m)
        ol = jnp.dot(p.astype(BF16), vext[pr, j * sub:j * sub + nk, :], preferred_element_type=F32)
        o_ref[rows, ps] = _unstack_head_pair(ol[:, 0:LANES]).astype(o_ref.dtype)
        stat_ref[rows, 2 * pr:2 * pr + 1] = m[0:sub]
        stat_ref[rows, 2 * pr + 1:2 * pr + 2] = m[sub:2 * sub]
        for h, r0 in ((2 * pr, 0), (2 * pr + 1, sub)):
            lane = LANES + N_HEADS + h
            stat_ref[rows, N_HEADS + h:N_HEADS + h + 1] = ol[r0:r0 + sub, lane:lane + 1]

    pending = []
    for j in range(nsub):
        for pr in range(N_HEADS // 2):
            pending.append((j, pr, scores(j, pr)))
            if len(pending) > BAND_AHEAD:
                finish(*pending.pop(0))
    for unit in pending:
        finish(*unit)


def _band_all_kernel(*refs, seg_lens):
    n = len(seg_lens)
    ins, outs, scr = refs[:8 * n], refs[8 * n:10 * n], refs[10 * n:12 * n]
    for g, seg_len in enumerate(seg_lens):
        _band_kernel(*ins[8 * g:8 * g + 8], *outs[2 * g:2 * g + 2], *scr[2 * g:2 * g + 2], seg_len=seg_len)


def _band_attention(qkvs, biases, seg_lens):
    tq = TQ_BAND
    n = len(qkvs)
    nt = SEQ // tq
    per = tq // BAND
    nhalo = SEQ // BAND
    left = lambda c: (lambda i: (jnp.maximum(i * per - 1, 0), c))
    right = lambda c: (lambda i: (jnp.minimum((i + 1) * per, nhalo - 1), c))
    main = lambda c: (lambda i: (i, c))
    kv_specs = []
    for c in (1, 2):
        kv_specs += [pl.BlockSpec((BAND, WIDTH), left(c)), pl.BlockSpec((tq, WIDTH), main(c)),
                     pl.BlockSpec((BAND, WIDTH), right(c))]
    nk = tq + 2 * BAND
    bias_shape = (N_HEADS // 2, 2 * SUB_BAND, SUB_BAND + 2 * BAND)
    flat = pl.pallas_call(
        functools.partial(_band_all_kernel, seg_lens=tuple(seg_lens)),
        grid=(nt,),
        in_specs=([pl.BlockSpec((tq, WIDTH), main(0))] + kv_specs
                  + [pl.BlockSpec(bias_shape, lambda i: (0, 0, 0))]) * n,
        out_specs=[pl.BlockSpec((tq, WIDTH), lambda i: (i, 0)),
                   pl.BlockSpec((tq, LANES), lambda i: (i, 0))] * n,
        out_shape=[jax.ShapeDtypeStruct((SEQ, WIDTH), BF16),
                   jax.ShapeDtypeStruct((SEQ, LANES), F32)] * n,
        scratch_shapes=[pltpu.VMEM((nk, WIDTH), BF16), pltpu.VMEM((N_HEADS // 2, nk, 2 * LANES), BF16)] * n,
        compiler_params=_params(("arbitrary",)),
        name="band",
    )(*[a for qkv, bias in zip(qkvs, biases) for a in [qkv] * 7 + [bias.reshape(bias_shape)]])
    return [(flat[2 * g], flat[2 * g + 1]) for g in range(n)]


def _t5_bucket(rel):
    n = T5_BUCKETS // 2
    max_exact = n // 2
    sign_part = jnp.where(rel > 0, n, 0)
    a = jnp.abs(rel)
    af = jnp.maximum(a, 1).astype(F32)
    large = max_exact + (jnp.log(af / max_exact) / math.log(T5_MAX_DISTANCE / max_exact)
                         * (n - max_exact)).astype(jnp.int32)
    large = jnp.minimum(large, n - 1)
    return sign_part + jnp.where(a < max_exact, a, large)


def _band_bias(rel_bias, dilation):
    tq = SUB_BAND
    delta = jnp.arange(tq + 2 * BAND)[None, :] - BAND - jnp.arange(tq)[:, None]
    b = _lookup(rel_bias.astype(F32).T, _t5_bucket(delta * dilation), T5_BUCKETS)
    return jnp.where((jnp.abs(delta) <= BAND)[None], b * LOG2E, NEG)


def _lookup(table, idx, n):
    onehot = (jnp.arange(n)[:, None] == idx.reshape(1, -1)).astype(F32)
    out = jnp.dot(table.reshape(-1, n), onehot, precision=lax.Precision.HIGHEST)
    return out.reshape(table.shape[:-1] + idx.shape)


def _na_kernel(q_ref, kp_ref, kc_ref, kn_ref, vp_ref, vc_ref, vn_ref, btab_ref, *rest):
    n_w = (len(rest) - 4) // 2
    w_in_refs, o_ref, w_out_refs = rest[:n_w], rest[n_w], rest[n_w + 1:2 * n_w + 1]
    kcat, vext, bias_ref = rest[2 * n_w + 1:]
    for src, dst in zip(w_in_refs, w_out_refs):
        dst[...] = src[...].astype(dst.dtype)
    tile = q_ref.shape[0]
    halo = NA_KH * GRID_W
    i = pl.program_id(0)

    @pl.when(i == 0)
    def _():
        for v in range(NA_KH):
            for h in range(N_HEADS):
                for k in range(NA_KH // 2):
                    bias_ref[v, h // 2, (h % 2) * GRID_W:(h % 2 + 1) * GRID_W, k * LANES:(k + 1) * LANES] = (
                        btab_ref[h, v + 2 * k])
    _fill_value_ones(vext, i == 0)
    for rows, k_ref, v_ref in ((slice(0, halo), kp_ref, vp_ref), (slice(halo, halo + tile), kc_ref, vc_ref),
                               (slice(halo + tile, 2 * halo + tile), kn_ref, vn_ref)):
        kcat[rows, :] = k_ref[...]
        for pr in range(N_HEADS // 2):
            vext[pr, rows, 0:LANES] = v_ref[:, pr * LANES:(pr + 1) * LANES]
    nkeys = NA_KH * GRID_W

    def window(t):
        r = i * NA_TILE + t
        rs = jnp.clip(r - NA_KH // 2, 0, ROWS - NA_KH)
        return pl.multiple_of((rs - (i * NA_TILE - NA_KH)) * GRID_W, GRID_W), rs - r + (NA_KH - 1)

    def scores(t, pr, off, var):
        ps = slice(pr * LANES, (pr + 1) * LANES)
        s = lax.dot_general(_stack_head_pair(q_ref[t * GRID_W:(t + 1) * GRID_W, ps]), kcat[pl.ds(off, nkeys), ps],
                            (((1,), (1,)), ((), ())), preferred_element_type=F32)
        return s + bias_ref[var, pr]

    def finish(t, pr, off, s):
        ps = slice(pr * LANES, (pr + 1) * LANES)
        m = jnp.max(s, axis=-1, keepdims=True)
        p = jnp.exp2(s - m)
        ol = jnp.dot(p.astype(BF16), vext[pr, pl.ds(off, nkeys), :], preferred_element_type=F32)
        o = ol[:, 0:LANES] / ol[:, LANES:2 * LANES]
        o_ref[t * GRID_W:(t + 1) * GRID_W, ps] = _unstack_head_pair(o).astype(o_ref.dtype)

    units = [(t, pr) for t in range(NA_TILE) for pr in range(N_HEADS // 2)]
    wins = [window(t) for t in range(NA_TILE)]
    pending = []
    for t, pr in units:
        off, var = wins[t]
        pending.append((t, pr, off, scores(t, pr, off, var)))
        if len(pending) > NA_AHEAD:
            finish(*pending.pop(0))
    for unit in pending:
        finish(*unit)


def _na_attention(qkv, btab, weights):
    tile = NA_TILE * GRID_W
    halo = NA_KH * GRID_W
    nt = SEQ // tile
    per = tile // halo
    w_specs = []
    for w in weights:
        rows = next(r for r in (w.shape[0] // nt, LANES) if r % HALO == 0 and w.shape[0] % r == 0)
        nblk = w.shape[0] // rows
        assert nblk <= nt
        w_specs.append(pl.BlockSpec((rows, w.shape[1]), lambda i, nblk=nblk: (jnp.minimum(i, nblk - 1), 0)))
    prev = lambda c: (lambda i: (jnp.maximum(i * per - 1, 0), c))
    cur = lambda c: (lambda i: (i, c))
    nxt = lambda c: (lambda i: (jnp.minimum((i + 1) * per, SEQ // halo - 1), c))
    kv_specs = [pl.BlockSpec((rows, WIDTH), f(c)) for c in (1, 2)
                for rows, f in ((halo, prev), (tile, cur), (halo, nxt))]
    return pl.pallas_call(
        _na_kernel,
        grid=(nt,),
        in_specs=[pl.BlockSpec((tile, WIDTH), cur(0))] + kv_specs
        + [pl.BlockSpec(btab.shape, lambda i: (0, 0, 0, 0), pipeline_mode=pl.Buffered(1))] + w_specs,
        out_specs=[pl.BlockSpec((tile, WIDTH), lambda i: (i, 0))] + w_specs,
        out_shape=[jax.ShapeDtypeStruct((SEQ, WIDTH), BF16)]
        + [jax.ShapeDtypeStruct(w.shape, BF16) for w in weights],
        scratch_shapes=[pltpu.VMEM((tile + 2 * halo, WIDTH), BF16),
                        pltpu.VMEM((N_HEADS // 2, tile + 2 * halo, 2 * LANES), BF16),
                        pltpu.VMEM((NA_KH, N_HEADS // 2, 2 * GRID_W, NA_KH * GRID_W), F32)],
        compiler_params=_params(("arbitrary",)),
        name="na",
    )(qkv, qkv, qkv, qkv, qkv, qkv, qkv, btab, *weights)


def _na_bias(rpb):
    cq = jnp.arange(GRID_W)
    col_start = jnp.clip(cq - NA_COLS // 2, 0, GRID_W - NA_COLS)
    col_mask = (cq[None, :] >= col_start[:, None]) & (cq[None, :] < col_start[:, None] + NA_COLS)
    dc = jnp.clip(cq[None, :] - cq[:, None], -(NA_COLS - 1), NA_COLS - 1) + NA_COLS - 1
    e = _lookup(rpb.astype(F32), dc, 2 * NA_COLS - 1)
    e = jnp.where(col_mask[None, None], e * LOG2E, NEG)
    return jnp.concatenate([e[:, :-1], e[:, 1:]], axis=-1)


def _expand_heads(w, exp_ref):
    hi = w.astype(BF16)
    lo = (w - hi.astype(F32)).astype(BF16)
    return (jnp.dot(hi, exp_ref[...], preferred_element_type=F32)
            + jnp.dot(lo, exp_ref[...], preferred_element_type=F32))


def _merge_kernel(x_ref, o1_ref, o4_ref, o16_ref, l1_ref, l4_ref, l16_ref, ob_ref, mod1_ref, wga_ref, wgb_ref,
                  wbd_ref, wbn_ref, wout_ref, exp_ref, gate_ref, mod_ref,
                  x1_ref, h2_ref, o4n, o16n, o16j, l4n, l16n, l16j, oa_scr):
    tm = x_ref.shape[0]
    for src4, src16, n4, j16, n16 in ((o4_ref, o16_ref, o4n, o16j, o16n), (l4_ref, l16_ref, l4n, l16j, l16n)):
        for t in range(n4.shape[0]):
            ls = slice(t * LANES, (t + 1) * LANES)
            for c in range(4):
                n4[t, pl.ds(c, tm // 4, stride=4), :] = src4[c, :, ls].astype(F32)
                for c2 in range(4):
                    j16[t, c, pl.ds(c2, tm // 16, stride=4), :] = src16[c + 4 * c2, :, ls].astype(F32)
                n16[t, pl.ds(c, tm // 4, stride=4), :] = j16[t, c]

    for sb in range(tm // SUB_MERGE):
        rows = slice(sb * SUB_MERGE, (sb + 1) * SUB_MERGE)
        stats = (l1_ref[rows, :], l4n[0, rows, :], l16n[0, rows, :])
        mx = jnp.maximum(jnp.maximum(stats[0], stats[1]), stats[2])
        es = [jnp.exp2(st - mx) for st in stats]
        den = sum(e * pltpu.roll(st, LANES - N_HEADS, 1) for e, st in zip(es, stats))
        is_head = lax.broadcasted_iota(jnp.int32, den.shape, 1) < N_HEADS
        w1, w4, w16 = (_expand_heads(jnp.where(is_head, e / den, 0.0), exp_ref) for e in es)
        for t in range(WIDTH // LANES):
            ls = slice(t * LANES, (t + 1) * LANES)
            oa = w1[:, ls] * o1_ref[rows, ls] + w4[:, ls] * o4n[t, rows, :] + w16[:, ls] * o16n[t, rows, :]
            oa_scr[sb, :, ls] = oa.astype(BF16)
        h = _rms_modulate(x_ref[rows, :], mod1_ref).astype(BF16)
        ga = jnp.dot(h, wga_ref[...], preferred_element_type=F32)
        ya = jnp.dot(oa_scr[sb], wbd_ref[...], preferred_element_type=F32)
        merged = ya / (1.0 + jnp.exp(-ga))
        gb = jnp.dot(h, wgb_ref[...], preferred_element_type=F32)
        yb = jnp.dot(ob_ref[rows, :], wbn_ref[...], preferred_element_type=F32)
        merged = merged + yb / (1.0 + jnp.exp(-gb))
        z = jnp.dot(merged.astype(BF16), wout_ref[...], preferred_element_type=F32)
        x1 = x_ref[rows, :] + gate_ref[...] * z
        x1_ref[rows, :] = x1
        h2_ref[rows, :] = _rms_modulate(x1, mod_ref).astype(BF16)


def _merge(x, o1, o4, o16, l1, l4, l16, ob, mod1, w_in_b, wbd, wbn, wout, gate1, mod2):
    tm = TM_MERGE
    nt = SEQ // tm
    row = lambda w: pl.BlockSpec((tm, w), lambda i: (i, 0))
    cls = lambda d, w: pl.BlockSpec((d, tm // d, w), lambda i: (0, i, 0))
    const = lambda shape: pl.BlockSpec(shape, lambda i: (0, 0))
    spread = (jnp.arange(LANES)[:, None] == jnp.arange(WIDTH)[None, :] // HEAD_DIM).astype(BF16)
    return pl.pallas_call(
        _merge_kernel,
        grid=(nt,),
        in_specs=[row(D_MODEL), row(WIDTH), cls(4, WIDTH), cls(16, WIDTH),
                  row(LANES), cls(4, LANES), cls(16, LANES), row(WIDTH), const((3, D_MODEL)),
                  pl.BlockSpec((D_MODEL, D_MODEL), lambda i: (0, 6 * WIDTH // D_MODEL)),
                  pl.BlockSpec((D_MODEL, D_MODEL), lambda i: (0, 6 * WIDTH // D_MODEL + 1)),
                  const((WIDTH, D_MODEL)), const((WIDTH, D_MODEL)), const((D_MODEL, D_MODEL)),
                  const((LANES, WIDTH)), const((1, D_MODEL)), const((3, D_MODEL))],
        out_specs=[row(D_MODEL), row(D_MODEL)],
        out_shape=[jax.ShapeDtypeStruct((SEQ, D_MODEL), F32), jax.ShapeDtypeStruct((SEQ, D_MODEL), BF16)],
        scratch_shapes=[pltpu.VMEM((WIDTH // LANES, tm, LANES), F32), pltpu.VMEM((WIDTH // LANES, tm, LANES), F32),
                        pltpu.VMEM((WIDTH // LANES, 4, tm // 4, LANES), F32),
                        pltpu.VMEM((1, tm, LANES), F32), pltpu.VMEM((1, tm, LANES), F32),
                        pltpu.VMEM((1, 4, tm // 4, LANES), F32),
                        pltpu.VMEM((tm // SUB_MERGE, SUB_MERGE, WIDTH), BF16)],
        compiler_params=_params(("arbitrary",)),
        name="merge",
    )(x, o1, o4.reshape(4, SEQ // 4, WIDTH), o16.reshape(16, SEQ // 16, WIDTH),
      l1, l4.reshape(4, SEQ // 4, LANES), l16.reshape(16, SEQ // 16, LANES), ob, mod1, w_in_b, w_in_b,
      wbd, wbn, wout, spread, gate1, mod2)


def _ffn_kernel(x1_ref, hm_ref, hp_ref, hn_ref, wup_ref, cw_ref, cb_ref, wd_ref, gate_ref, gfin_ref,
                y_ref, hext, act):
    tm = x1_ref.shape[0]
    n = tm + 2 * HALO
    i = pl.program_id(0)
    hext[0:HALO, :] = jnp.where(i == 0, jnp.zeros_like(hp_ref), hp_ref[...])
    hext[HALO:HALO + tm, :] = hm_ref[...]
    hext[HALO + tm:n, :] = jnp.where(i == pl.num_programs(0) - 1, jnp.zeros_like(hn_ref), hn_ref[...])

    def conv(cols):
        p = jnp.dot(hext[...], wup_ref[:, cols], preferred_element_type=F32)
        prev = pltpu.roll(p, 1, 0)[HALO:HALO + tm]
        nxt = pltpu.roll(p, n - 1, 0)[HALO:HALO + tm]
        return (cw_ref[0:1, cols] * prev + cw_ref[1:2, cols] * p[HALO:HALO + tm] + cw_ref[2:3, cols] * nxt
                + cb_ref[:, cols])

    for c in range(D_FF // TF_FFN):
        val = slice(c * TF_FFN, (c + 1) * TF_FFN)
        gate = slice(D_FF + c * TF_FFN, D_FF + (c + 1) * TF_FFN)
        act[:, val] = (jax.nn.gelu(conv(gate), approximate=True) * conv(val)).astype(BF16)

    x2 = x1_ref[...] + gate_ref[...] * jnp.dot(act[...], wd_ref[...], preferred_element_type=F32)
    y = x2 * lax.rsqrt(jnp.mean(x2 * x2, axis=-1, keepdims=True) + RMS_EPS)
    y_ref[...] = y * gfin_ref[...]


def _ffn(x1, h2, w_up, conv_w, conv_b, w_down, gate2, g_final):
    tm = TM_FFN
    nt = SEQ // tm
    per = tm // HALO
    nhalo = SEQ // HALO
    row = lambda: pl.BlockSpec((tm, D_MODEL), lambda i: (i, 0))
    const = lambda a: pl.BlockSpec(a.shape, lambda i: (0, 0), pipeline_mode=pl.Buffered(1))
    return pl.pallas_call(
        _ffn_kernel,
        grid=(nt,),
        in_specs=[
            row(), row(),
            pl.BlockSpec((HALO, D_MODEL), lambda i: (jnp.maximum(i * per - 1, 0), 0)),
            pl.BlockSpec((HALO, D_MODEL), lambda i: (jnp.minimum((i + 1) * per, nhalo - 1), 0)),
            const(w_up), const(conv_w), const(conv_b), const(w_down), const(gate2), const(g_final),
        ],
        out_specs=row(),
        out_shape=jax.ShapeDtypeStruct((SEQ, D_MODEL), F32),
        scratch_shapes=[pltpu.VMEM((tm + 2 * HALO, D_MODEL), BF16), pltpu.VMEM((tm, D_FF), BF16)],
        compiler_params=_params(("arbitrary",)),
        name="ffn",
    )(x1, h2, h2, h2, w_up, conv_w, conv_b, w_down, gate2, g_final)


def kernel(x, c, w_ada, b_ada, g_mix, w_in, rel_bias, na_rpb, w_branch_dil, w_branch_na, w_out, g_ffn,
           w_up, conv_w, conv_b, w_down, g_final):
    assert x.shape == (1, SEQ, D_MODEL) and w_ada.shape[0] == 1
    xs = x[0]
    ada = _ada(c, w_ada[0], b_ada[0])
    shift1, scale1, gate1, shift2, scale2, gate2 = [ada[:, k * D_MODEL:(k + 1) * D_MODEL] for k in range(6)]
    mod1 = jnp.concatenate([g_mix[0][None], scale1, shift1], axis=0)
    mod2 = jnp.concatenate([g_ffn[0][None], scale2, shift2], axis=0)

    a_nat, a_m4, a_m16, b_qkv = _inproj(xs, mod1, w_in[0])

    (o1, l1), (o4, l4), (o16, l16) = _band_attention(
        [arr.reshape(SEQ, 3 * WIDTH) for arr in (a_nat, a_m4, a_m16)],
        [_band_bias(rel_bias, d) for d in DILATIONS], [SEQ // d for d in DILATIONS])
    ob, wbd, wbn, wout, wup, wdown, w_in_b = _na_attention(
        b_qkv, _na_bias(na_rpb[0]), (w_branch_dil[0], w_branch_na[0], w_out[0], w_up[0], w_down[0], w_in[0]))

    x1, h2 = _merge(xs, o1, o4, o16, l1, l4, l16, ob, mod1, w_in_b, wbd, wbn, wout, gate1, mod2)
    y = _ffn(x1, h2, wup, conv_w[0], conv_b[0].reshape(1, 2 * D_FF), wdown, gate2, g_final.reshape(1, D_MODEL))
    return y[None]
```
